```python
import math
import jax
import jax.numpy as jnp
from jax import lax
import numpy as np

D_MODEL = 2048
BATCH = 32
SEQ = 256
DEPTH = 4
DEC_BATCH = 2
DEC_SEQ = 2048
PAST_LEN = 256

GRID_W = 64
N_EVEN = (DEPTH + 1) // 2
N_ODD = DEPTH // 2
H_A = 6
DH_A = 128
W_A = H_A * 2 * DH_A
G_B = 4
DG_B = 128
W_B = G_B * DG_B
H_C = 16
Q_LORA = 1536
KV_LORA = 512
NOPE_C = 128
ROPE_C = 64
DQK_C = NOPE_C + ROPE_C
DV_C = 128
D_FF = 5632
N_MOD = 9
ROPE_BASE = 10000.0
Q_BLOCK = 128
EPS = 1e-6

kernel_name = 'hybrid_diffusion_prefix_step'


def rms_norm(x, g):
    xf = x.astype(jnp.float32)
    y = xf * lax.rsqrt(jnp.mean(xf * xf, axis=-1, keepdims=True) + EPS)
    return (y * g.astype(jnp.float32)).astype(x.dtype)


def swiglu(h, w_gate, w_up, w_down):
    return (jax.nn.silu(h @ w_gate) * (h @ w_up)) @ w_down


def axial_rope_tables(rows, rot_dim):
    row = jnp.repeat(jnp.arange(rows, dtype=jnp.float32), GRID_W)
    col = jnp.tile(jnp.arange(GRID_W, dtype=jnp.float32), rows)
    half = rot_dim // 2
    inv_freq = ROPE_BASE ** (-jnp.arange(0, half, 2, dtype=jnp.float32) / half)
    ang_r = row[:, None] * inv_freq[None, :]
    ang_c = col[:, None] * inv_freq[None, :]
    ang_r = jnp.concatenate([ang_r, ang_r], axis=-1)
    ang_c = jnp.concatenate([ang_c, ang_c], axis=-1)
    return (jnp.cos(ang_r), jnp.sin(ang_r), jnp.cos(ang_c), jnp.sin(ang_c))


def rotate(x, cos, sin):
    x1, x2 = jnp.split(x, 2, axis=-1)
    return x * cos + jnp.concatenate([-x2, x1], axis=-1) * sin


def apply_axial_rope(x, tables, n_mid):
    cos_r, sin_r, cos_c, sin_c = [t.reshape(t.shape[0], *([1] * n_mid), t.shape[1]) for t in tables]
    xf = x.astype(jnp.float32)
    xr, xc = jnp.split(xf, 2, axis=-1)
    out = jnp.concatenate([rotate(xr, cos_r, sin_r), rotate(xc, cos_c, sin_c)], axis=-1)
    return out.astype(x.dtype)


def sweep_query_blocks(block_fn, q):
    b, h, sq = q.shape[:3]
    nb = sq // Q_BLOCK
    qb = jnp.moveaxis(q.reshape(b, h, nb, Q_BLOCK, *q.shape[3:]), 2, 0)
    out = lax.map(block_fn, qb)
    return jnp.moveaxis(out, 0, 2).reshape(b, h, sq, out.shape[-1])


def softmax_f32(s):
    return jax.nn.softmax(s.astype(jnp.float32), axis=-1)


def diff_fourier_mixer(h, w_in, w_out, g_q, g_k, lam_vecs, g_sub, lam_init, ctx_k, ctx_v, rope):
    b, s, _ = h.shape
    q, k, v, f = jnp.split(h @ w_in, [W_A, 2 * W_A, 3 * W_A], axis=-1)
    q = rms_norm(q.reshape(b, s, H_A, 2, DH_A), g_q)
    k = rms_norm(k.reshape(b, s, H_A, 2, DH_A), g_k)
    v = v.reshape(b, s, H_A, 2 * DH_A)
    if rope is not None:
        q = apply_axial_rope(q, rope, 2)
        k = apply_axial_rope(k, rope, 2)
    q = q.transpose(0, 2, 1, 3, 4)
    k = k.transpose(0, 2, 1, 3, 4)
    v = v.transpose(0, 2, 1, 3)
    new_k = k.reshape(b, H_A, s, 2 * DH_A)
    new_v = v
    if ctx_k is not None:
        k = jnp.concatenate([k, ctx_k.reshape(b, H_A, ctx_k.shape[2], 2, DH_A)], axis=2)
        v = jnp.concatenate([v, ctx_v], axis=2)
    lam = (jnp.exp(jnp.sum(lam_vecs[0].astype(jnp.float32) * lam_vecs[1].astype(jnp.float32)))
           - jnp.exp(jnp.sum(lam_vecs[2].astype(jnp.float32) * lam_vecs[3].astype(jnp.float32))) + lam_init)
    scale = DH_A ** -0.5

    def block(qb):
        p = softmax_f32(jnp.einsum('bhqcd,bhkcd->cbhqk', qb, k) * scale)
        attn = (p[0] - lam * p[1]).astype(v.dtype)
        return jnp.einsum('bhqk,bhkv->bhqv', attn, v)

    o = sweep_query_blocks(block, q)
    o = rms_norm(o, g_sub) * (1.0 - lam_init)
    o = o.transpose(0, 2, 1, 3).reshape(b, s, W_A)
    fg = f.reshape(b, s, G_B, DG_B).astype(jnp.float32)
    fo = jnp.real(jnp.fft.fft2(fg, axes=(1, 3), norm='ortho')).astype(h.dtype).reshape(b, s, W_B)
    y = jnp.concatenate([o, fo], axis=-1) @ w_out
    return y, new_k, new_v


def mla_mixer(h, w_in, g_q_lora, w_uq, g_kv_lora, w_ukv, g_q, g_k, w_o, ctx_ckv, ctx_kpe, rope):
    b, s, _ = h.shape
    c_q, c_kv, k_pe = jnp.split(h @ w_in, [Q_LORA, Q_LORA + KV_LORA], axis=-1)
    c_kv = rms_norm(c_kv, g_kv_lora)
    q = rms_norm((rms_norm(c_q, g_q_lora) @ w_uq).reshape(b, s, H_C, DQK_C), g_q)

    def expand(ckv, kpe):
        n = ckv.shape[1]
        kv = (ckv @ w_ukv).reshape(b, n, H_C, NOPE_C + DV_C)
        k_nope, val = jnp.split(kv, [NOPE_C], axis=-1)
        kpe_h = jnp.broadcast_to(kpe[:, :, None, :], (b, n, H_C, ROPE_C))
        return rms_norm(jnp.concatenate([k_nope, kpe_h], axis=-1), g_k), val

    k, v = expand(c_kv, k_pe)
    if rope is not None:
        q = jnp.concatenate([q[..., :NOPE_C], apply_axial_rope(q[..., NOPE_C:], rope, 1)], axis=-1)
        k = jnp.concatenate([k[..., :NOPE_C], apply_axial_rope(k[..., NOPE_C:], rope, 1)], axis=-1)
    if ctx_ckv is not None:
        k_c, v_c = expand(ctx_ckv, ctx_kpe)
        k = jnp.concatenate([k, k_c], axis=1)
        v = jnp.concatenate([v, v_c], axis=1)
    q = q.transpose(0, 2, 1, 3)
    k = k.transpose(0, 2, 1, 3)
    v = v.transpose(0, 2, 1, 3)
    scale = DQK_C ** -0.5

    def block(qb):
        p = softmax_f32(jnp.einsum('bhqd,bhkd->bhqk', qb, k) * scale).astype(v.dtype)
        return jnp.einsum('bhqk,bhkv->bhqv', p, v)

    o = sweep_query_blocks(block, q).transpose(0, 2, 1, 3).reshape(b, s, H_C * DV_C)
    return o @ w_o, c_kv, k_pe


def setup_inputs(seed: int = 0) -> dict:
    key = jax.random.key(seed)
    ks = iter(jax.random.split(key, 40))

    def nrm(shape, scale):
        return jax.random.normal(next(ks), shape, jnp.float32) * scale

    def gain(shape):
        return 1.0 + 0.05 * jax.random.normal(next(ks), shape, jnp.float32)

    return {
        'x_prompt': nrm((BATCH, SEQ, D_MODEL), 1.0),
        'x_sample': nrm((DEC_BATCH, DEC_SEQ, D_MODEL), 1.0),
        'c': nrm((DEC_BATCH, D_MODEL), 1.0),
        'cache_diff_k': nrm((DEC_BATCH, N_EVEN, H_A, PAST_LEN, 2 * DH_A), 1.0),
        'cache_diff_v': nrm((DEC_BATCH, N_EVEN, H_A, PAST_LEN, 2 * DH_A), 1.0),
        'cache_mla_ckv': nrm((DEC_BATCH, N_ODD, PAST_LEN, KV_LORA), 1.0),
        'cache_mla_kpe': nrm((DEC_BATCH, N_ODD, PAST_LEN, ROPE_C), 1.0),
        'c_ctx': nrm((D_MODEL,), 1.0),
        'w_mod': nrm((DEPTH, D_MODEL, N_MOD * D_MODEL), 0.5 * D_MODEL ** -0.5),
        'b_mod': nrm((DEPTH, N_MOD * D_MODEL), 0.01),
        'g_norm': gain((DEPTH, 3, D_MODEL)),
        'w_ffn_gate': nrm((DEPTH, 2, D_MODEL, D_FF), D_MODEL ** -0.5),
        'w_ffn_up': nrm((DEPTH, 2, D_MODEL, D_FF), D_MODEL ** -0.5),
        'w_ffn_down': nrm((DEPTH, 2, D_FF, D_MODEL), D_FF ** -0.5),
        'w_in_ab': nrm((N_EVEN, D_MODEL, 3 * W_A + W_B), D_MODEL ** -0.5),
        'w_out_ab': nrm((N_EVEN, W_A + W_B, D_MODEL), (W_A + W_B) ** -0.5),
        'g_qk_diff': gain((N_EVEN, 2, DH_A)),
        'diff_lambda': nrm((N_EVEN, 4, DH_A), 0.1),
        'g_diff_sub': gain((N_EVEN, 2 * DH_A)),
        'w_in_mla': nrm((N_ODD, D_MODEL, Q_LORA + KV_LORA + ROPE_C), D_MODEL ** -0.5),
        'g_q_lora': gain((N_ODD, Q_LORA)),
        'w_uq': nrm((N_ODD, Q_LORA, H_C * DQK_C), Q_LORA ** -0.5),
        'g_kv_lora': gain((N_ODD, KV_LORA)),
        'w_ukv': nrm((N_ODD, KV_LORA, H_C * (NOPE_C + DV_C)), KV_LORA ** -0.5),
        'g_qk_mla': gain((N_ODD, 2, DQK_C)),
        'w_o_mla': nrm((N_ODD, H_C * DV_C, D_MODEL), (H_C * DV_C) ** -0.5),
    }


def reference(x_prompt, x_sample, c, cache_diff_k, cache_diff_v, cache_mla_ckv, cache_mla_kpe, c_ctx,
              w_mod, b_mod, g_norm, w_ffn_gate, w_ffn_up, w_ffn_down,
              w_in_ab, w_out_ab, g_qk_diff, diff_lambda, g_diff_sub,
              w_in_mla, g_q_lora, w_uq, g_kv_lora, w_ukv, g_qk_mla, w_o_mla):
    ROWS = x_sample.shape[1] // GRID_W
    rope_a = axial_rope_tables(ROWS, DH_A)
    rope_c = axial_rope_tables(ROWS, ROPE_C)

    def layer(l, x, cond, ctx0, ctx1, rp_a, rp_c):
        mod = (jax.nn.silu(cond) @ w_mod[l] + b_mod[l])[:, None, :]
        sh1, sc1, gt1, sh2, sc2, gt2, sh3, sc3, gt3 = jnp.split(mod, N_MOD, axis=-1)
        h = rms_norm(x, g_norm[l, 0]) * (1.0 + sc1) + sh1
        x = x + 0.5 * gt1 * swiglu(h, w_ffn_gate[l, 0], w_ffn_up[l, 0], w_ffn_down[l, 0])
        h = rms_norm(x, g_norm[l, 1]) * (1.0 + sc2) + sh2
        if l % 2 == 0:
            e = l // 2
            lam_init = 0.8 - 0.6 * math.exp(-0.3 * l)
            y, st0, st1 = diff_fourier_mixer(h, w_in_ab[e], w_out_ab[e], g_qk_diff[e, 0], g_qk_diff[e, 1],
                                             diff_lambda[e], g_diff_sub[e], lam_init, ctx0, ctx1, rp_a)
        else:
            o = l // 2
            y, st0, st1 = mla_mixer(h, w_in_mla[o], g_q_lora[o], w_uq[o], g_kv_lora[o], w_ukv[o],
                                    g_qk_mla[o, 0], g_qk_mla[o, 1], w_o_mla[o], ctx0, ctx1, rp_c)
        x = x + gt2 * y
        h = rms_norm(x, g_norm[l, 2]) * (1.0 + sc3) + sh3
        x = x + 0.5 * gt3 * swiglu(h, w_ffn_gate[l, 1], w_ffn_up[l, 1], w_ffn_down[l, 1])
        return x, st0, st1

    y_prompt = x_prompt
    cond_ctx = c_ctx[None, :]
    diff_k, diff_v, mla_ckv, mla_kpe = [], [], [], []
    for l in range(DEPTH):
        y_prompt, st0, st1 = layer(l, y_prompt, cond_ctx, None, None, None, None)
        if l % 2 == 0:
            diff_k.append(st0)
            diff_v.append(st1)
        else:
            mla_ckv.append(st0)
            mla_kpe.append(st1)

    y_sample = x_sample
    for l in range(DEPTH):
        j = l // 2
        if l % 2 == 0:
            ctx0, ctx1 = cache_diff_k[:, j], cache_diff_v[:, j]
        else:
            ctx0, ctx1 = cache_mla_ckv[:, j], cache_mla_kpe[:, j]
        y_sample, _, _ = layer(l, y_sample, c, ctx0, ctx1, rope_a, rope_c)

    new_diff_k = jnp.stack(diff_k, axis=1)
    new_diff_v = jnp.stack(diff_v, axis=1)
    new_mla_ckv = jnp.stack(mla_ckv, axis=1)
    new_mla_kpe = jnp.stack(mla_kpe, axis=1)
    return (y_prompt, y_sample, new_diff_k, new_diff_v, new_mla_ckv, new_mla_kpe)
```

```python
import functools
import math

import jax
import jax.numpy as jnp
from jax import lax
from jax.experimental import pallas as pl
from jax.experimental.pallas import tpu as pltpu

F32 = jnp.float32
BF16 = jnp.bfloat16

D_MODEL = 2048
BATCH = 32
SEQ = 256
DEPTH = 4
DEC_BATCH = 2
DEC_SEQ = 2048
PAST_LEN = 256
GRID_W = 64
N_EVEN = 2
N_ODD = 2
H_A = 6
DH_A = 128
W_A = H_A * 2 * DH_A
G_B = 4
DG_B = 128
W_B = G_B * DG_B
H_C = 16
Q_LORA = 1536
KV_LORA = 512
NOPE_C = 128
ROPE_C = 64
DQK_C = NOPE_C + ROPE_C
DV_C = 128
D_FF = 5632
N_MOD = 9
ROPE_BASE = 10000.0
EPS = 1e-6

T_CTX = BATCH * SEQ
T_SMP = DEC_BATCH * DEC_SEQ
T_ALL = T_CTX + T_SMP
N_COND = 1 + DEC_BATCH
LANE = 128
HEAD_PAD = 256

VMEM_LIMIT = 52 * 1024 * 1024


def _cparams(sem):
    return pltpu.CompilerParams(dimension_semantics=sem, vmem_limit_bytes=VMEM_LIMIT)


def _cond_row(i, tm):
    n_ctx = T_CTX // tm
    per = DEC_SEQ // tm
    return jnp.where(i < n_ctx, 0, 1 + (i - n_ctx) // per)


def _dot(a, b):
    return jnp.dot(a, b, preferred_element_type=F32)


def _dot_nt(a, b):
    return lax.dot_general(a, b, (((1,), (1,)), ((), ())), preferred_element_type=F32)


def _silu(x):
    return x / (1.0 + jnp.exp(-x))


def _rms(x, denom):
    return lax.rsqrt(jnp.sum(x * x, axis=-1, keepdims=True) * (1.0 / denom) + EPS)


def _norm_mod(x, g, mod_ref, base):
    sh = mod_ref[base:base + 1, :]
    sc = mod_ref[base + 1:base + 2, :]
    return (x * _rms(x, D_MODEL) * g) * (1.0 + sc) + sh


def _rope(y, cos, sin_a, sin_b, q):
    return y * cos + pltpu.roll(y, LANE - q, 1) * sin_a + pltpu.roll(y, q, 1) * sin_b


def _mod_kernel(c_ref, w_ref, b_ref, o_ref):
    a = _silu(c_ref[...]).astype(BF16)
    o_ref[...] = _dot(a, w_ref[...].astype(BF16)) + b_ref[...]


def _modulation(cond8, w_mod, b_mod):
    tn = 1024
    n = N_MOD * D_MODEL
    return pl.pallas_call(
        _mod_kernel,
        out_shape=jax.ShapeDtypeStruct((DEPTH, 8, n), F32),
        grid=(DEPTH, n // tn),
        in_specs=[
            pl.BlockSpec((8, D_MODEL), lambda l, j: (0, 0)),
            pl.BlockSpec((None, D_MODEL, tn), lambda l, j: (l, 0, j)),
            pl.BlockSpec((None, 1, tn), lambda l, j: (l, 0, j)),
        ],
        out_specs=pl.BlockSpec((None, 8, tn), lambda l, j: (l, 0, j)),
        compiler_params=_cparams(("parallel", "parallel")),
        name="modulation",
    )(cond8, w_mod, b_mod.reshape(DEPTH, 1, n))


def _ffn_kernel(x_ref, mod_ref, g_ref, wg_ref, wu_ref, wd_ref, o_ref, h_ref, *, base, nf):
    f = pl.program_id(1)

    @pl.when(f == 0)
    def _():
        h_ref[...] = _norm_mod(x_ref[...], g_ref[...], mod_ref, base).astype(BF16)

    h = h_ref[...]
    gate = _dot(h, wg_ref[...])
    up = _dot(h, wu_ref[...])
    part = _dot((_silu(gate) * up).astype(BF16), wd_ref[...])

    @pl.when(f == 0)
    def _():
        o_ref[...] = part

    @pl.when(f > 0)
    def _():
        o_ref[...] += part

    @pl.when(f == nf - 1)
    def _():
        gt = mod_ref[base + 2:base + 3, :]
        o_ref[...] = x_ref[...] + (0.5 * gt) * o_ref[...]


def _ffn(x, mod_l, g, wg, wu, wd, base):
    tm, tf = 512, 512
    nf = D_FF // tf
    return pl.pallas_call(
        functools.partial(_ffn_kernel, base=base, nf=nf),
        out_shape=jax.ShapeDtypeStruct((T_ALL, D_MODEL), F32),
        grid=(T_ALL // tm, nf),
        in_specs=[
            pl.BlockSpec((tm, D_MODEL), lambda i, f: (i, 0)),
            pl.BlockSpec((None, N_MOD, D_MODEL), lambda i, f: (_cond_row(i, tm), 0, 0)),
            pl.BlockSpec((1, D_MODEL), lambda i, f: (0, 0)),
            pl.BlockSpec((D_MODEL, tf), lambda i, f: (0, f)),
            pl.BlockSpec((D_MODEL, tf), lambda i, f: (0, f)),
            pl.BlockSpec((tf, D_MODEL), lambda i, f: (f, 0)),
        ],
        out_specs=pl.BlockSpec((tm, D_MODEL), lambda i, f: (i, 0)),
        scratch_shapes=[pltpu.VMEM((tm, D_MODEL), BF16)],
        compiler_params=_cparams(("parallel", "arbitrary")),
        name="ffn",
    )(x, mod_l, g, wg, wu, wd)


def _inproj_even_kernel(x_ref, mod_ref, g_ref, w_ref, gq_ref, gk_ref, cos_ref, sa_ref, sb_ref,
                        o_ref, h_ref, *, tn):
    j = pl.program_id(1)
    n_qk = 2 * W_A // tn
    n_q = W_A // tn

    @pl.when(j == 0)
    def _():
        h_ref[...] = _norm_mod(x_ref[...], g_ref[...], mod_ref, 3).astype(BF16)

    res = _dot(h_ref[...], w_ref[...])

    @pl.when(j < n_qk)
    def _():
        g = jnp.where(j < n_q, gq_ref[...], gk_ref[...])
        cos, sa, sb = cos_ref[...], sa_ref[...], sb_ref[...]
        for c in range(tn // DH_A):
            xc = res[:, c * DH_A:(c + 1) * DH_A]
            y = xc * _rms(xc, DH_A) * g
            o_ref[:, c * DH_A:(c + 1) * DH_A] = _rope(y, cos, sa, sb, DH_A // 4)

    @pl.when(j >= n_qk)
    def _():
        o_ref[...] = res


def _inproj_even(x, mod_l, g, w, gq, gk, tabs):
    tm, tn = 512, 512
    n_out = 3 * W_A + W_B
    tab_spec = pl.BlockSpec((tm, LANE), lambda i, j: (i, 0))
    return pl.pallas_call(
        functools.partial(_inproj_even_kernel, tn=tn),
        out_shape=jax.ShapeDtypeStruct((T_ALL, n_out), F32),
        grid=(T_ALL // tm, n_out // tn),
        in_specs=[
            pl.BlockSpec((tm, D_MODEL), lambda i, j: (i, 0)),
            pl.BlockSpec((None, N_MOD, D_MODEL), lambda i, j: (_cond_row(i, tm), 0, 0)),
            pl.BlockSpec((1, D_MODEL), lambda i, j: (0, 0)),
            pl.BlockSpec((D_MODEL, tn), lambda i, j: (0, j)),
            pl.BlockSpec((1, DH_A), lambda i, j: (0, 0)),
            pl.BlockSpec((1, DH_A), lambda i, j: (0, 0)),
            tab_spec, tab_spec, tab_spec,
        ],
        out_specs=pl.BlockSpec((tm, tn), lambda i, j: (i, j)),
        scratch_shapes=[pltpu.VMEM((tm, D_MODEL), BF16)],
        compiler_params=_cparams(("parallel", "arbitrary")),
        name="inproj_even",
    )(x, mod_l, g, w, gq, gk, *tabs)


def _softmax_pair(s, s2):
    m = jnp.max(s, axis=-1, keepdims=True)
    if s2 is not None:
        m = jnp.maximum(m, jnp.max(s2, axis=-1, keepdims=True))
    e = jnp.exp(s - m)
    l = jnp.sum(e, axis=-1, keepdims=True)
    if s2 is None:
        return e * (1.0 / l), None
    e2 = jnp.exp(s2 - m)
    inv = 1.0 / (l + jnp.sum(e2, axis=-1, keepdims=True))
    return e * inv, e2 * inv


def _diff_attn_kernel(lam_ref, gsub_ref, q_ref, k_ref, v_ref, *rest, lam_init, heads, has_ctx):
    if has_ctx:
        ck_ref, cv_ref, o_ref = rest
    else:
        (o_ref,) = rest
    lv = lam_ref[...]
    lam = (jnp.exp(jnp.sum(lv[0:1] * lv[1:2], axis=-1, keepdims=True))
           - jnp.exp(jnp.sum(lv[2:3] * lv[3:4], axis=-1, keepdims=True)) + lam_init)
    scale = DH_A ** -0.5
    dh2 = 2 * DH_A
    for hh in range(heads):
        q = q_ref[:, hh * dh2:(hh + 1) * dh2].astype(BF16)
        k = k_ref[:, hh * dh2:(hh + 1) * dh2].astype(BF16)
        v = v_ref[:, hh * dh2:(hh + 1) * dh2].astype(BF16)
        if has_ctx:
            ck = ck_ref[hh].astype(BF16)
            cv = cv_ref[hh].astype(BF16)
        p, p2 = [], []
        for c in range(2):
            sl = slice(c * DH_A, (c + 1) * DH_A)
            s = _dot_nt(q[:, sl], k[:, sl]) * scale
            s2 = _dot_nt(q[:, sl], ck[:, sl]) * scale if has_ctx else None
            a, a2 = _softmax_pair(s, s2)
            p.append(a)
            p2.append(a2)
        o = _dot((p[0] - lam * p[1]).astype(BF16), v)
        if has_ctx:
            o = o + _dot((p2[0] - lam * p2[1]).astype(BF16), cv)
        o = o * _rms(o, dh2) * gsub_ref[...] * (1.0 - lam_init)
        o_ref[:, hh * dh2:(hh + 1) * dh2] = o.astype(BF16)


def _diff_attn_ctx(qkvf, lam_vecs, gsub, lam_init):
    w = W_A
    return pl.pallas_call(
        functools.partial(_diff_attn_kernel, lam_init=lam_init, heads=H_A, has_ctx=False),
        out_shape=jax.ShapeDtypeStruct((T_CTX, W_A), BF16),
        grid=(BATCH,),
        in_specs=[
            pl.BlockSpec((4, DH_A), lambda b: (0, 0)),
            pl.BlockSpec((1, 2 * DH_A), lambda b: (0, 0)),
            pl.BlockSpec((SEQ, w), lambda b: (b, 0)),
            pl.BlockSpec((SEQ, w), lambda b: (b, 1)),
            pl.BlockSpec((SEQ, w), lambda b: (b, 2)),
        ],
        out_specs=pl.BlockSpec((SEQ, w), lambda b: (b, 0)),
        compiler_params=_cparams(("parallel",)),
        name="diff_attn_ctx",
    )(lam_vecs, gsub, qkvf, qkvf, qkvf)


def _diff_attn_smp(qkvf, cache_k, cache_v, e, lam_vecs, gsub, lam_init):
    tq = 256
    w = 2 * DH_A
    nq = DEC_SEQ // tq
    q0 = T_CTX // tq
    kb0 = T_CTX // DEC_SEQ
    cache_spec = pl.BlockSpec((None, None, 1, PAST_LEN, w), lambda b, h, i: (b, e, h, 0, 0))
    return pl.pallas_call(
        functools.partial(_diff_attn_kernel, lam_init=lam_init, heads=1, has_ctx=True),
        out_shape=jax.ShapeDtypeStruct((T_SMP, W_A), BF16),
        grid=(DEC_BATCH, H_A, nq),
        in_specs=[
            pl.BlockSpec((4, DH_A), lambda b, h, i: (0, 0)),
            pl.BlockSpec((1, w), lambda b, h, i: (0, 0)),
            pl.BlockSpec((tq, w), lambda b, h, i: (q0 + b * nq + i, h)),
            pl.BlockSpec((DEC_SEQ, w), lambda b, h, i: (kb0 + b, H_A + h)),
            pl.BlockSpec((DEC_SEQ, w), lambda b, h, i: (kb0 + b, 2 * H_A + h)),
            cache_spec, cache_spec,
        ],
        out_specs=pl.BlockSpec((tq, w), lambda b, h, i: (b * nq + i, h)),
        compiler_params=_cparams(("parallel", "parallel", "arbitrary")),
        name="diff_attn_smp",
    )(lam_vecs, gsub, qkvf, qkvf, qkvf, cache_k, cache_v)


def _fourier_kernel(x_ref, bc_ref, bs_ref, cs_ref, ss_ref, o_ref, y1_ref, y2_ref, *, scale):
    @pl.when(pl.program_id(1) == 0)
    def _():
        x = x_ref[...].astype(BF16)
        y1_ref[...] = _dot(x, bc_ref[...]).astype(BF16)
        y2_ref[...] = _dot(x, bs_ref[...]).astype(BF16)

    out = _dot(cs_ref[...], y1_ref[...]) - _dot(ss_ref[...], y2_ref[...])
    o_ref[...] = (out * scale).astype(BF16)


def _fourier(qkvf, s, nb, row0, dft_ch, dft_seq):
    tr = 256
    col = (3 * W_A) // W_B
    bc, bs = dft_ch
    cs, ss = dft_seq
    return pl.pallas_call(
        functools.partial(_fourier_kernel, scale=1.0 / math.sqrt(s * DG_B)),
        out_shape=jax.ShapeDtypeStruct((nb * s, W_B), BF16),
        grid=(nb, s // tr),
        in_specs=[
            pl.BlockSpec((s, W_B), lambda b, r: (row0 // s + b, col)),
            pl.BlockSpec((W_B, W_B), lambda b, r: (0, 0)),
            pl.BlockSpec((W_B, W_B), lambda b, r: (0, 0)),
            pl.BlockSpec((tr, s), lambda b, r: (r, 0)),
            pl.BlockSpec((tr, s), lambda b, r: (r, 0)),
        ],
        out_specs=pl.BlockSpec((tr, W_B), lambda b, r: (b * (s // tr) + r, 0)),
        scratch_shapes=[pltpu.VMEM((s, W_B), BF16), pltpu.VMEM((s, W_B), BF16)],
        compiler_params=_cparams(("parallel", "arbitrary")),
        name="fourier",
    )(qkvf, bc, bs, cs, ss)


def _dft_mats(n):
    idx = jnp.arange(n, dtype=jnp.int32)
    ang = ((idx[:, None] * idx[None, :]) % n).astype(F32) * (2.0 * math.pi / n)
    return jnp.cos(ang), jnp.sin(ang)


def _dft_channel_mats():
    c, s = _dft_mats(DG_B)
    eye = jnp.eye(G_B, dtype=F32)
    return jnp.kron(eye, c).astype(BF16), jnp.kron(eye, s).astype(BF16)


def _outproj_kernel(*refs, n_act, tm):
    x_ref, mod_ref = refs[0], refs[1]
    act = refs[2:2 + 3 * n_act]
    o_ref = refs[2 + 3 * n_act]
    is_ctx = pl.program_id(0) < T_CTX // tm
    y = None
    for a in range(n_act):
        ctx_ref, smp_ref, w_ref = act[3 * a:3 * a + 3]
        lhs = jnp.where(is_ctx, ctx_ref[...], smp_ref[...])
        t = _dot(lhs, w_ref[...])
        y = t if y is None else y + t
    o_ref[...] = x_ref[...] + mod_ref[5:6, :] * y


def _outproj(x, mod_l, acts):
    tm, tn = 512, 1024
    n_ctx = T_CTX // tm
    in_specs = [
        pl.BlockSpec((tm, tn), lambda i, j: (i, j)),
        pl.BlockSpec((None, N_MOD, tn), lambda i, j: (_cond_row(i, tm), 0, j)),
    ]
    args = [x, mod_l]
    for ctx, smp, w in acts:
        kdim = w.shape[0]
        in_specs += [
            pl.BlockSpec((tm, kdim), lambda i, j: (jnp.minimum(i, n_ctx - 1), 0)),
            pl.BlockSpec((tm, kdim), lambda i, j: (jnp.maximum(i - n_ctx, 0), 0)),
            pl.BlockSpec((kdim, tn), lambda i, j: (0, j)),
        ]
        args += [ctx, smp, w]
    return pl.pallas_call(
        functools.partial(_outproj_kernel, n_act=len(acts), tm=tm),
        out_shape=jax.ShapeDtypeStruct((T_ALL, D_MODEL), F32),
        grid=(T_ALL // tm, D_MODEL // tn),
        in_specs=in_specs,
        out_specs=pl.BlockSpec((tm, tn), lambda i, j: (i, j)),
        compiler_params=_cparams(("parallel", "parallel")),
        name="outproj",
    )(*args)


def _mla_in_kernel(x_ref, mod_ref, g_ref, w_ref, gq_ref, gkv_ref, cq_ref, ckv_ref, kpe_ref):
    h = _norm_mod(x_ref[...], g_ref[...], mod_ref, 3).astype(BF16)
    res = _dot(h, w_ref[...])
    cq = res[:, :Q_LORA]
    cq_ref[...] = (cq * _rms(cq, Q_LORA) * gq_ref[...]).astype(BF16)
    ckv = res[:, Q_LORA:Q_LORA + KV_LORA]
    ckv_ref[...] = ckv * _rms(ckv, KV_LORA) * gkv_ref[...]
    kpe_ref[...] = res[:, Q_LORA + KV_LORA:]


def _mla_in(x, mod_l, g, w_pad, gq, gkv):
    tm = 256
    n = Q_LORA + KV_LORA + LANE
    return pl.pallas_call(
        _mla_in_kernel,
        out_shape=(jax.ShapeDtypeStruct((T_ALL, Q_LORA), BF16),
                   jax.ShapeDtypeStruct((T_ALL, KV_LORA), F32),
                   jax.ShapeDtypeStruct((T_ALL, LANE), F32)),
        grid=(T_ALL // tm,),
        in_specs=[
            pl.BlockSpec((tm, D_MODEL), lambda i: (i, 0)),
            pl.BlockSpec((None, N_MOD, D_MODEL), lambda i: (_cond_row(i, tm), 0, 0)),
            pl.BlockSpec((1, D_MODEL), lambda i: (0, 0)),
            pl.BlockSpec((D_MODEL, n), lambda i: (0, 0)),
            pl.BlockSpec((1, Q_LORA), lambda i: (0, 0)),
            pl.BlockSpec((1, KV_LORA), lambda i: (0, 0)),
        ],
        out_specs=(pl.BlockSpec((tm, Q_LORA), lambda i: (i, 0)),
                   pl.BlockSpec((tm, KV_LORA), lambda i: (i, 0)),
                   pl.BlockSpec((tm, LANE), lambda i: (i, 0))),
        compiler_params=_cparams(("parallel",)),
        name="mla_in",
    )(x, mod_l, g, w_pad, gq, gkv)


def _mla_q_kernel(cq_ref, w_ref, g_ref, cos_ref, sa_ref, sb_ref, o_ref, *, heads):
    res = _dot(cq_ref[...], w_ref[...])
    cos, sa, sb = cos_ref[...], sa_ref[...], sb_ref[...]
    for hh in range(heads):
        xh = res[:, hh * HEAD_PAD:(hh + 1) * HEAD_PAD]
        y = xh * _rms(xh, DQK_C) * g_ref[...]
        o_ref[:, hh * HEAD_PAD:hh * HEAD_PAD + LANE] = y[:, :LANE].astype(BF16)
        o_ref[:, hh * HEAD_PAD + LANE:(hh + 1) * HEAD_PAD] = _rope(
            y[:, LANE:], cos, sa, sb, ROPE_C // 4).astype(BF16)


def _mla_q(cqn, w_uq_pad, gq_pad, tabs):
    tm, heads = 512, 2
    tn = heads * HEAD_PAD
    tab_spec = pl.BlockSpec((tm, LANE), lambda i, j: (i, 0))
    return pl.pallas_call(
        functools.partial(_mla_q_kernel, heads=heads),
        out_shape=jax.ShapeDtypeStruct((T_ALL, H_C * HEAD_PAD), BF16),
        grid=(T_ALL // tm, H_C // heads),
        in_specs=[
            pl.BlockSpec((tm, Q_LORA), lambda i, j: (i, 0)),
            pl.BlockSpec((Q_LORA, tn), lambda i, j: (0, j)),
            pl.BlockSpec((1, HEAD_PAD), lambda i, j: (0, 0)),
            tab_spec, tab_spec, tab_spec,
        ],
        out_specs=pl.BlockSpec((tm, tn), lambda i, j: (i, j)),
        compiler_params=_cparams(("parallel", "parallel")),
        name="mla_q",
    )(cqn, w_uq_pad, gq_pad, *tabs)


def _mla_kv_kernel(ckv_ref, kpe_ref, w_ref, g_ref, *rest, heads, use_rope):
    if use_rope:
        cos_ref, sa_ref, sb_ref, k_ref, v_ref = rest
    else:
        k_ref, v_ref = rest
    res = _dot(ckv_ref[...].astype(BF16), w_ref[...])
    kpe = kpe_ref[...]
    ss_pe = jnp.sum(kpe * kpe, axis=-1, keepdims=True)
    g = g_ref[...]
    for hh in range(heads):
        k_nope = res[:, hh * 2 * LANE:hh * 2 * LANE + LANE]
        val = res[:, hh * 2 * LANE + LANE:(hh + 1) * 2 * LANE]
        ss = jnp.sum(k_nope * k_nope, axis=-1, keepdims=True) + ss_pe
        r = lax.rsqrt(ss * (1.0 / DQK_C) + EPS)
        kr = kpe * r * g[:, LANE:]
        if use_rope:
            kr = _rope(kr, cos_ref[...], sa_ref[...], sb_ref[...], ROPE_C // 4)
        k_ref[:, hh * HEAD_PAD:hh * HEAD_PAD + LANE] = (k_nope * r * g[:, :LANE]).astype(BF16)
        k_ref[:, hh * HEAD_PAD + LANE:(hh + 1) * HEAD_PAD] = kr.astype(BF16)
        v_ref[:, hh * DV_C:(hh + 1) * DV_C] = val.astype(BF16)


def _mla_kv(ckv, kpe_pad, w_ukv, gk_pad, tabs):
    rows = ckv.shape[0]
    tm, heads = 512, 2
    use_rope = tabs is not None
    tab_spec = pl.BlockSpec((tm, LANE), lambda i, j: (i, 0))
    in_specs = [
        pl.BlockSpec((tm, KV_LORA), lambda i, j: (i, 0)),
        pl.BlockSpec((tm, LANE), lambda i, j: (i, 0)),
        pl.BlockSpec((KV_LORA, heads * 2 * LANE), lambda i, j: (0, j)),
        pl.BlockSpec((1, HEAD_PAD), lambda i, j: (0, 0)),
    ]
    args = [ckv, kpe_pad, w_ukv, gk_pad]
    if use_rope:
        in_specs += [tab_spec] * 3
        args += list(tabs)
    return pl.pallas_call(
        functools.partial(_mla_kv_kernel, heads=heads, use_rope=use_rope),
        out_shape=(jax.ShapeDtypeStruct((rows, H_C * HEAD_PAD), BF16),
                   jax.ShapeDtypeStruct((rows, H_C * DV_C), BF16)),
        grid=(rows // tm, H_C // heads),
        in_specs=in_specs,
        out_specs=(pl.BlockSpec((tm, heads * HEAD_PAD), lambda i, j: (i, j)),
                   pl.BlockSpec((tm, heads * DV_C), lambda i, j: (i, j))),
        compiler_params=_cparams(("parallel", "parallel")),
        name="mla_kv",
    )(*args)


def _mla_attn_kernel(q_ref, k_ref, v_ref, *rest, heads, has_ctx):
    if has_ctx:
        ck_ref, cv_ref, o_ref = rest
    else:
        (o_ref,) = rest
    scale = DQK_C ** -0.5
    for hh in range(heads):
        ks = slice(hh * HEAD_PAD, (hh + 1) * HEAD_PAD)
        vs = slice(hh * DV_C, (hh + 1) * DV_C)
        q = q_ref[:, ks]
        s = _dot_nt(q, k_ref[:, ks]) * scale
        s2 = _dot_nt(q, ck_ref[:, ks]) * scale if has_ctx else None
        p, p2 = _softmax_pair(s, s2)
        o = _dot(p.astype(BF16), v_ref[:, vs])
        if has_ctx:
            o = o + _dot(p2.astype(BF16), cv_ref[:, vs])
        o_ref[:, vs] = o.astype(BF16)


def _mla_attn_ctx(q, k, v):
    heads = 8
    return pl.pallas_call(
        functools.partial(_mla_attn_kernel, heads=heads, has_ctx=False),
        out_shape=jax.ShapeDtypeStruct((T_CTX, H_C * DV_C), BF16),
        grid=(BATCH, H_C // heads),
        in_specs=[
            pl.BlockSpec((SEQ, heads * HEAD_PAD), lambda b, j: (b, j)),
            pl.BlockSpec((SEQ, heads * HEAD_PAD), lambda b, j: (b, j)),
            pl.BlockSpec((SEQ, heads * DV_C), lambda b, j: (b, j)),
        ],
        out_specs=pl.BlockSpec((SEQ, heads * DV_C), lambda b, j: (b, j)),
        compiler_params=_cparams(("parallel", "parallel")),
        name="mla_attn_ctx",
    )(q, k, v)


def _mla_attn_smp(q, k, v, ck, cv):
    tq = 512
    nq = DEC_SEQ // tq
    q0 = T_CTX // tq
    kb0 = T_CTX // DEC_SEQ
    return pl.pallas_call(
        functools.partial(_mla_attn_kernel, heads=1, has_ctx=True),
        out_shape=jax.ShapeDtypeStruct((T_SMP, H_C * DV_C), BF16),
        grid=(DEC_BATCH, H_C, nq),
        in_specs=[
            pl.BlockSpec((tq, HEAD_PAD), lambda b, h, i: (q0 + b * nq + i, h)),
            pl.BlockSpec((DEC_SEQ, HEAD_PAD), lambda b, h, i: (kb0 + b, h)),
            pl.BlockSpec((DEC_SEQ, DV_C), lambda b, h, i: (kb0 + b, h)),
            pl.BlockSpec((PAST_LEN, HEAD_PAD), lambda b, h, i: (b, h)),
            pl.BlockSpec((PAST_LEN, DV_C), lambda b, h, i: (b, h)),
        ],
        out_specs=pl.BlockSpec((tq, DV_C), lambda b, h, i: (b * nq + i, h)),
        compiler_params=_cparams(("parallel", "parallel", "arbitrary")),
        name="mla_attn_smp",
    )(q, k, v, ck, cv)


def _axial_angles(rows, rot_dim):
    row = jnp.repeat(jnp.arange(rows, dtype=F32), GRID_W)
    col = jnp.tile(jnp.arange(GRID_W, dtype=F32), rows)
    half = rot_dim // 2
    inv_freq = ROPE_BASE ** (-jnp.arange(0, half, 2, dtype=F32) / half)
    ang_r = row[:, None] * inv_freq[None, :]
    ang_c = col[:, None] * inv_freq[None, :]
    return jnp.concatenate([ang_r, ang_r, ang_c, ang_c], axis=-1)


def _rope_tables(rows, rot_dim):
    ang = _axial_angles(rows, rot_dim)
    q = rot_dim // 4
    blk = (jnp.arange(rot_dim) // q) % 2
    cos, sin = jnp.cos(ang), jnp.sin(ang)
    sin_a = jnp.where(blk == 0, -sin, 0.0)
    sin_b = jnp.where(blk == 1, sin, 0.0)

    def full(t, fill):
        t = jnp.pad(t, ((0, 0), (0, LANE - rot_dim)), constant_values=fill)
        return jnp.concatenate([jnp.full((T_CTX, LANE), fill, F32), jnp.tile(t, (DEC_BATCH, 1))], axis=0)

    return full(cos, 1.0), full(sin_a, 0.0), full(sin_b, 0.0)


def kernel(x_prompt, x_sample, c, cache_diff_k, cache_diff_v, cache_mla_ckv, cache_mla_kpe, c_ctx, w_mod, b_mod, g_norm, w_ffn_gate, w_ffn_up, w_ffn_down, w_in_ab, w_out_ab, g_qk_diff, diff_lambda, g_diff_sub, w_in_mla, g_q_lora, w_uq, g_kv_lora, w_ukv, g_qk_mla, w_o_mla):
    rows = DEC_SEQ // GRID_W
    tabs_a = _rope_tables(rows, DH_A)
    tabs_c = _rope_tables(rows, ROPE_C)
    dft_ch = _dft_channel_mats()
    dft_ctx = tuple(m.astype(BF16) for m in _dft_mats(SEQ))
    dft_smp = tuple(m.astype(BF16) for m in _dft_mats(DEC_SEQ))

    cond8 = jnp.concatenate([c_ctx[None, :], c, jnp.zeros((8 - N_COND, D_MODEL), F32)], axis=0)
    mod = _modulation(cond8, w_mod, b_mod)[:, :N_COND].reshape(DEPTH, N_COND, N_MOD, D_MODEL)

    x = jnp.concatenate([x_prompt.reshape(T_CTX, D_MODEL), x_sample.reshape(T_SMP, D_MODEL)], axis=0)
    diff_k, diff_v, mla_ckv, mla_kpe = [], [], [], []

    for l in range(DEPTH):
        mod_l = mod[l]
        x = _ffn(x, mod_l, g_norm[l, 0][None, :], w_ffn_gate[l, 0].astype(BF16),
                 w_ffn_up[l, 0].astype(BF16), w_ffn_down[l, 0].astype(BF16), 0)
        if l % 2 == 0:
            e = l // 2
            lam_init = 0.8 - 0.6 * math.exp(-0.3 * l)
            qkvf = _inproj_even(x, mod_l, g_norm[l, 1][None, :], w_in_ab[e].astype(BF16),
                                g_qk_diff[e, 0][None, :], g_qk_diff[e, 1][None, :], tabs_a)
            gsub = g_diff_sub[e][None, :]
            o_ctx = _diff_attn_ctx(qkvf, diff_lambda[e], gsub, lam_init)
            o_smp = _diff_attn_smp(qkvf, cache_diff_k, cache_diff_v, e, diff_lambda[e], gsub, lam_init)
            f_ctx = _fourier(qkvf, SEQ, BATCH, 0, dft_ch, dft_ctx)
            f_smp = _fourier(qkvf, DEC_SEQ, DEC_BATCH, T_CTX, dft_ch, dft_smp)
            w_out = w_out_ab[e].astype(BF16)
            x = _outproj(x, mod_l, [(o_ctx, o_smp, w_out[:W_A]), (f_ctx, f_smp, w_out[W_A:])])
            kv = qkvf[:T_CTX, W_A:3 * W_A].reshape(BATCH, SEQ, 2, H_A, 2 * DH_A)
            diff_k.append(kv[:, :, 0].transpose(0, 2, 1, 3))
            diff_v.append(kv[:, :, 1].transpose(0, 2, 1, 3))
        else:
            o = l // 2
            w_in_pad = jnp.pad(w_in_mla[o].astype(BF16), ((0, 0), (0, LANE - ROPE_C)))
            cqn, ckvn, kpe_pad = _mla_in(x, mod_l, g_norm[l, 1][None, :], w_in_pad,
                                         g_q_lora[o][None, :], g_kv_lora[o][None, :])
            w_uq_pad = jnp.pad(w_uq[o].astype(BF16).reshape(Q_LORA, H_C, DQK_C),
                               ((0, 0), (0, 0), (0, HEAD_PAD - DQK_C))).reshape(Q_LORA, H_C * HEAD_PAD)
            gq_pad = jnp.pad(g_qk_mla[o, 0], (0, HEAD_PAD - DQK_C))[None, :]
            gk_pad = jnp.pad(g_qk_mla[o, 1], (0, HEAD_PAD - DQK_C))[None, :]
            q = _mla_q(cqn, w_uq_pad, gq_pad, tabs_c)
            w_ukv_b = w_ukv[o].astype(BF16)
            k, v = _mla_kv(ckvn, kpe_pad, w_ukv_b, gk_pad, tabs_c)
            cache_ckv = cache_mla_ckv[:, o].reshape(DEC_BATCH * PAST_LEN, KV_LORA)
            cache_kpe = jnp.pad(cache_mla_kpe[:, o].reshape(DEC_BATCH * PAST_LEN, ROPE_C),
                                ((0, 0), (0, LANE - ROPE_C)))
            ck, cv = _mla_kv(cache_ckv, cache_kpe, w_ukv_b, gk_pad, None)
            a_ctx = _mla_attn_ctx(q, k, v)
            a_smp = _mla_attn_smp(q, k, v, ck, cv)
            x = _outproj(x, mod_l, [(a_ctx, a_smp, w_o_mla[o].astype(BF16))])
            mla_ckv.append(ckvn[:T_CTX].reshape(BATCH, SEQ, KV_LORA))
            mla_kpe.append(kpe_pad[:T_CTX, :ROPE_C].reshape(BATCH, SEQ, ROPE_C))
        x = _ffn(x, mod_l, g_norm[l, 2][None, :], w_ffn_gate[l, 1].astype(BF16),
                 w_ffn_up[l, 1].astype(BF16), w_ffn_down[l, 1].astype(BF16), 6)

    y_prompt = x[:T_CTX].reshape(BATCH, SEQ, D_MODEL)
    y_sample = x[T_CTX:].reshape(DEC_BATCH, DEC_SEQ, D_MODEL)
    return (y_prompt, y_sample, jnp.stack(diff_k, axis=1), jnp.stack(diff_v, axis=1),
            jnp.stack(mla_ckv, axis=1), jnp.stack(mla_kpe, axis=1))
```

```python
import functools
import math

import jax
import jax.numpy as jnp
from jax import lax
from jax.experimental import pallas as pl
from jax.experimental.pallas import tpu as pltpu

F32 = jnp.float32
BF16 = jnp.bfloat16

D_MODEL = 2048
BATCH = 32
SEQ = 256
DEPTH = 4
DEC_BATCH = 2
DEC_SEQ = 2048
PAST_LEN = 256
GRID_W = 64
N_EVEN = 2
N_ODD = 2
H_A = 6
DH_A = 128
W_A = H_A * 2 * DH_A
G_B = 4
DG_B = 128
W_B = G_B * DG_B
H_C = 16
Q_LORA = 1536
KV_LORA = 512
NOPE_C = 128
ROPE_C = 64
DQK_C = NOPE_C + ROPE_C
DV_C = 128
D_FF = 5632
N_MOD = 9
ROPE_BASE = 10000.0
EPS = 1e-6
LOG2E = 1.4426950408889634

T_CTX = BATCH * SEQ
T_SMP = DEC_BATCH * DEC_SEQ
T_ALL = T_CTX + T_SMP
N_COND = 1 + DEC_BATCH
LANE = 128
HEAD_PAD = 256
TM = 512

VMEM_LIMIT = 52 * 1024 * 1024


def _cparams(sem):
    return pltpu.CompilerParams(dimension_semantics=sem, vmem_limit_bytes=VMEM_LIMIT)


def _cond_row(i):
    n_ctx = T_CTX // TM
    per = DEC_SEQ // TM
    return jnp.where(i < n_ctx, 0, 1 + (i - n_ctx) // per)


class _Pass:
    def __init__(self, latent):
        self.latent = latent
        self.rows = T_SMP if latent else T_CTX
        self.tile0 = (T_CTX // TM) if latent else 0
        self.tiles = self.rows // TM
        self.seq = DEC_SEQ if latent else SEQ
        self.batch = DEC_BATCH if latent else BATCH

    def cond(self, i):
        return 1 + i // (DEC_SEQ // TM) if self.latent else 0


def _dot(a, b):
    return jnp.dot(a, b, preferred_element_type=F32)


def _dot_nt(a, b):
    return lax.dot_general(a, b, (((1,), (1,)), ((), ())), preferred_element_type=F32)


def _silu(x):
    return x / (1.0 + jnp.exp(-x))


def _rms(x, denom):
    return lax.rsqrt(jnp.sum(x * x, axis=-1, keepdims=True) * (1.0 / denom) + EPS)


def _norm_mod(x, g, mod_ref, base):
    sh = mod_ref[base:base + 1, :]
    sc = mod_ref[base + 1:base + 2, :]
    return (x * _rms(x, D_MODEL) * g) * (1.0 + sc) + sh


def _rope(y, cos, sin_a, sin_b, q):
    return y * cos + pltpu.roll(y, LANE - q, 1) * sin_a + pltpu.roll(y, q, 1) * sin_b


def _softmax_exp(s, s2, c):
    m = jnp.max(s, axis=-1, keepdims=True)
    if s2 is not None:
        m = jnp.maximum(m, jnp.max(s2, axis=-1, keepdims=True))
    e = jnp.exp2((s - m) * c)
    l = jnp.sum(e, axis=-1, keepdims=True)
    e2 = None
    if s2 is not None:
        e2 = jnp.exp2((s2 - m) * c)
        l = l + jnp.sum(e2, axis=-1, keepdims=True)
    return e, e2, 1.0 / l


def _mod_kernel(c_ref, w_ref, b_ref, o_ref):
    a = _silu(c_ref[...]).astype(BF16)
    o_ref[...] = _dot(a, w_ref[...].astype(BF16)) + b_ref[...]


def _modulation(cond8, w_mod, b_mod):
    tn = 1024
    n = N_MOD * D_MODEL
    return pl.pallas_call(
        _mod_kernel,
        out_shape=jax.ShapeDtypeStruct((DEPTH, 8, n), F32),
        grid=(DEPTH, n // tn),
        in_specs=[
            pl.BlockSpec((8, D_MODEL), lambda l, j: (0, 0)),
            pl.BlockSpec((None, D_MODEL, tn), lambda l, j: (l, 0, j)),
            pl.BlockSpec((None, 1, tn), lambda l, j: (l, 0, j)),
        ],
        out_specs=pl.BlockSpec((None, 8, tn), lambda l, j: (l, 0, j)),
        compiler_params=_cparams(("parallel", "parallel")),
        name="modulation",
    )(cond8, w_mod, b_mod.reshape(DEPTH, 1, n))


def _ffn_kernel(x_ref, mod_ref, g_ref, wg_ref, wu_ref, wd_ref, o_ref, h_ref, *, base, nf):
    f = pl.program_id(1)

    @pl.when(f == 0)
    def _():
        h_ref[...] = _norm_mod(x_ref[...], g_ref[...], mod_ref, base).astype(BF16)
        o_ref[...] = jnp.zeros_like(o_ref)

    h = h_ref[...]
    gate = _dot(h, wg_ref[...])
    up = _dot(h, wu_ref[...])
    o_ref[...] += _dot((_silu(gate) * up).astype(BF16), wd_ref[...])

    @pl.when(f == nf - 1)
    def _():
        gt = mod_ref[base + 2:base + 3, :]
        o_ref[...] = x_ref[...] + (0.5 * gt) * o_ref[...]


def _ffn(x, mod_l, g, wg, wu, wd, l, k):
    tf = 512
    nf = D_FF // tf
    base = 6 * k
    return pl.pallas_call(
        functools.partial(_ffn_kernel, base=base, nf=nf),
        out_shape=jax.ShapeDtypeStruct((T_ALL, D_MODEL), F32),
        grid=(T_ALL // TM, nf),
        in_specs=[
            pl.BlockSpec((TM, D_MODEL), lambda i, f: (i, 0)),
            pl.BlockSpec((None, N_MOD, D_MODEL), lambda i, f: (_cond_row(i), 0, 0)),
            pl.BlockSpec((None, 1, D_MODEL), lambda i, f: (2 * k, 0, 0)),
            pl.BlockSpec((None, None, D_MODEL, tf), lambda i, f: (l, k, 0, f)),
            pl.BlockSpec((None, None, D_MODEL, tf), lambda i, f: (l, k, 0, f)),
            pl.BlockSpec((None, None, tf, D_MODEL), lambda i, f: (l, k, f, 0)),
        ],
        out_specs=pl.BlockSpec((TM, D_MODEL), lambda i, f: (i, 0)),
        scratch_shapes=[pltpu.VMEM((TM, D_MODEL), BF16)],
        compiler_params=_cparams(("parallel", "arbitrary")),
        name="ffn",
    )(x, mod_l, g, wg, wu, wd)


QKV_TN = W_A // 2
F_TN = W_B // 2


def _inproj_even_kernel(*refs, latent):
    x_ref, mod_ref, g_ref, wq_ref, wk_ref, wv_ref, wf_ref, gq_ref, gk_ref = refs[:9]
    refs = refs[9:]
    if latent:
        cos_ref, sa_ref, sb_ref, q_ref, k_ref, v_ref, f_ref, h_ref = refs
    else:
        q_ref, k_ref, v_ref, f_ref, nk_ref, nv_ref, h_ref = refs[-7:]

    @pl.when(pl.program_id(1) == 0)
    def _():
        h_ref[...] = _norm_mod(x_ref[...], g_ref[...], mod_ref, 3).astype(BF16)

    h = h_ref[...]
    for w_ref, gn_ref, o_ref, is_k in ((wq_ref, gq_ref, q_ref, False), (wk_ref, gk_ref, k_ref, True)):
        res = _dot(h, w_ref[...])
        g = gn_ref[...]
        for ch in range(QKV_TN // DH_A):
            xc = res[:, ch * DH_A:(ch + 1) * DH_A]
            y = xc * _rms(xc, DH_A) * g
            if latent:
                y = _rope(y, cos_ref[...], sa_ref[...], sb_ref[...], DH_A // 4)
            o_ref[:, ch * DH_A:(ch + 1) * DH_A] = y.astype(BF16)
            if is_k and not latent:
                for b in range(TM // SEQ):
                    nk_ref[b, ch // 2, :, (ch % 2) * DH_A:(ch % 2 + 1) * DH_A] = y[b * SEQ:(b + 1) * SEQ, :]
    res = _dot(h, wv_ref[...])
    v_ref[...] = res.astype(BF16)
    if not latent:
        for b in range(TM // SEQ):
            for hh in range(QKV_TN // (2 * DH_A)):
                nv_ref[b, hh] = res[b * SEQ:(b + 1) * SEQ, hh * 2 * DH_A:(hh + 1) * 2 * DH_A]
    f_ref[...] = _dot(h, wf_ref[...]).astype(BF16)


def _inproj_even(ps, x, mod_l, g_norm_l, w_in, gqk, e, tabs, prev_kv):
    nj = W_A // QKV_TN
    hb = QKV_TN // (2 * DH_A)
    in_specs = [
        pl.BlockSpec((TM, D_MODEL), lambda i, j: (ps.tile0 + i, 0)),
        pl.BlockSpec((None, N_MOD, D_MODEL), lambda i, j: (ps.cond(i), 0, 0)),
        pl.BlockSpec((None, 1, D_MODEL), lambda i, j: (1, 0, 0)),
        pl.BlockSpec((None, D_MODEL, QKV_TN), lambda i, j: (e, 0, j)),
        pl.BlockSpec((None, D_MODEL, QKV_TN), lambda i, j: (e, 0, nj + j)),
        pl.BlockSpec((None, D_MODEL, QKV_TN), lambda i, j: (e, 0, 2 * nj + j)),
        pl.BlockSpec((None, D_MODEL, F_TN), lambda i, j: (e, 0, 3 * W_A // F_TN + j)),
        pl.BlockSpec((None, None, 1, DH_A), lambda i, j: (e, 0, 0, 0)),
        pl.BlockSpec((None, None, 1, DH_A), lambda i, j: (e, 1, 0, 0)),
    ]
    args = [x, mod_l, g_norm_l, w_in, w_in, w_in, w_in, gqk, gqk]
    out_shape = [jax.ShapeDtypeStruct((ps.rows, W_A), BF16)] * 3 + [jax.ShapeDtypeStruct((ps.rows, W_B), BF16)]
    out_specs = [pl.BlockSpec((TM, QKV_TN), lambda i, j: (i, j))] * 3 + [pl.BlockSpec((TM, F_TN), lambda i, j: (i, j))]
    aliases = {}
    if ps.latent:
        tab_spec = pl.BlockSpec((TM, LANE), lambda i, j: (i % (DEC_SEQ // TM), 0))
        in_specs += [tab_spec] * 3
        args += list(tabs)
    else:
        kv_shape = jax.ShapeDtypeStruct((BATCH, N_EVEN, H_A, SEQ, 2 * DH_A), F32)
        kv_spec = pl.BlockSpec((TM // SEQ, None, hb, SEQ, 2 * DH_A), lambda i, j: (i, e, j, 0, 0))
        out_shape += [kv_shape, kv_shape]
        out_specs += [kv_spec, kv_spec]
        if prev_kv is not None:
            aliases = {len(args): 4, len(args) + 1: 5}
            in_specs += [pl.BlockSpec(memory_space=pl.ANY)] * 2
            args += list(prev_kv)
    return pl.pallas_call(
        functools.partial(_inproj_even_kernel, latent=ps.latent),
        out_shape=out_shape,
        grid=(ps.tiles, nj),
        in_specs=in_specs,
        out_specs=out_specs,
        scratch_shapes=[pltpu.VMEM((TM, D_MODEL), BF16)],
        input_output_aliases=aliases,
        compiler_params=_cparams(("parallel", "arbitrary")),
        name="inproj_even",
    )(*args)


def _diff_attn_kernel(lam_ref, gsub_ref, q_ref, k_ref, v_ref, *rest, lam_init, heads, has_ctx, sub):
    if has_ctx:
        ck_ref, cv_ref, o_ref = rest
    else:
        (o_ref,) = rest
    lv = lam_ref[...]
    lam = (jnp.exp(jnp.sum(lv[0:1] * lv[1:2], axis=-1, keepdims=True))
           - jnp.exp(jnp.sum(lv[2:3] * lv[3:4], axis=-1, keepdims=True)) + lam_init)
    c = DH_A ** -0.5 * LOG2E
    dh2 = 2 * DH_A
    gsub = gsub_ref[...] * (1.0 - lam_init)
    for hh in range(heads):
        hs = slice(hh * dh2, (hh + 1) * dh2)
        k = k_ref[:, hs]
        v = v_ref[:, hs]
        if has_ctx:
            ck = ck_ref[hh].astype(BF16)
            cv = cv_ref[hh].astype(BF16)
        for r in range(q_ref.shape[0] // sub):
            rs = slice(r * sub, (r + 1) * sub)
            q = q_ref[rs, hs]
            a, a2 = None, None
            for m in range(2):
                sl = slice(m * DH_A, (m + 1) * DH_A)
                s = _dot_nt(q[:, sl], k[:, sl])
                s2 = _dot_nt(q[:, sl], ck[:, sl]) if has_ctx else None
                e, e2, inv = _softmax_exp(s, s2, c)
                w = inv if m == 0 else -(lam * inv)
                a = e * w if a is None else a + e * w
                if has_ctx:
                    a2 = e2 * w if a2 is None else a2 + e2 * w
            o = _dot(a.astype(BF16), v)
            if has_ctx:
                o = o + _dot(a2.astype(BF16), cv)
            o_ref[rs, hs] = (o * _rms(o, dh2) * gsub).astype(BF16)


def _diff_attn_ctx(q, k, v, lam_vecs, gsub, e, lam_init):
    spec = pl.BlockSpec((SEQ, W_A), lambda b: (b, 0))
    return pl.pallas_call(
        functools.partial(_diff_attn_kernel, lam_init=lam_init, heads=H_A, has_ctx=False, sub=SEQ),
        out_shape=jax.ShapeDtypeStruct((T_CTX, W_A), BF16),
        grid=(BATCH,),
        in_specs=[
            pl.BlockSpec((None, 4, DH_A), lambda b: (e, 0, 0)),
            pl.BlockSpec((None, 1, 2 * DH_A), lambda b: (e, 0, 0)),
            spec, spec, spec,
        ],
        out_specs=spec,
        compiler_params=_cparams(("parallel",)),
        name="diff_attn_ctx",
    )(lam_vecs, gsub, q, k, v)


def _diff_attn_smp(q, k, v, cache_k, cache_v, lam_vecs, gsub, e, lam_init):
    tq, sub = 512, 256
    w = 2 * DH_A
    nq = DEC_SEQ // tq
    cache_spec = pl.BlockSpec((None, None, 1, PAST_LEN, w), lambda b, h, i: (b, e, h, 0, 0))
    return pl.pallas_call(
        functools.partial(_diff_attn_kernel, lam_init=lam_init, heads=1, has_ctx=True, sub=sub),
        out_shape=jax.ShapeDtypeStruct((T_SMP, W_A), BF16),
        grid=(DEC_BATCH, H_A, nq),
        in_specs=[
            pl.BlockSpec((None, 4, DH_A), lambda b, h, i: (e, 0, 0)),
            pl.BlockSpec((None, 1, w), lambda b, h, i: (e, 0, 0)),
            pl.BlockSpec((tq, w), lambda b, h, i: (b * nq + i, h)),
            pl.BlockSpec((DEC_SEQ, w), lambda b, h, i: (b, h)),
            pl.BlockSpec((DEC_SEQ, w), lambda b, h, i: (b, h)),
            cache_spec, cache_spec,
        ],
        out_specs=pl.BlockSpec((tq, w), lambda b, h, i: (b * nq + i, h)),
        compiler_params=_cparams(("parallel", "parallel", "arbitrary")),
        name="diff_attn_smp",
    )(lam_vecs, gsub, q, k, v, cache_k, cache_v)


def _fourier_kernel(x_ref, bc_ref, bs_ref, cs_ref, ss_ref, o_ref, y1_ref, y2_ref, *, scale):
    @pl.when(pl.program_id(1) == 0)
    def _():
        x = x_ref[...]
        y1_ref[...] = _dot(x, bc_ref[...]).astype(BF16)
        y2_ref[...] = _dot(x, bs_ref[...]).astype(BF16)

    out = _dot(cs_ref[...], y1_ref[...]) - _dot(ss_ref[...], y2_ref[...])
    o_ref[...] = (out * scale).astype(BF16)


def _fourier(f, s, dft_ch, dft_seq):
    tr = 256
    nb = f.shape[0] // s
    bc, bs = dft_ch
    cs, ss = dft_seq
    return pl.pallas_call(
        functools.partial(_fourier_kernel, scale=1.0 / math.sqrt(s * DG_B)),
        out_shape=jax.ShapeDtypeStruct(f.shape, BF16),
        grid=(nb, s // tr),
        in_specs=[
            pl.BlockSpec((s, W_B), lambda b, r: (b, 0)),
            pl.BlockSpec((W_B, W_B), lambda b, r: (0, 0)),
            pl.BlockSpec((W_B, W_B), lambda b, r: (0, 0)),
            pl.BlockSpec((tr, s), lambda b, r: (r, 0)),
            pl.BlockSpec((tr, s), lambda b, r: (r, 0)),
        ],
        out_specs=pl.BlockSpec((tr, W_B), lambda b, r: (b * (s // tr) + r, 0)),
        scratch_shapes=[pltpu.VMEM((s, W_B), BF16), pltpu.VMEM((s, W_B), BF16)],
        compiler_params=_cparams(("parallel", "arbitrary")),
        name="fourier",
    )(f, bc, bs, cs, ss)


def _dft_mats(n):
    idx = jnp.arange(n, dtype=jnp.int32)
    ang = ((idx[:, None] * idx[None, :]) % n).astype(F32) * (2.0 * math.pi / n)
    return jnp.cos(ang), jnp.sin(ang)


def _dft_channel_mats():
    c, s = _dft_mats(DG_B)
    eye = jnp.eye(G_B, dtype=F32)
    return jnp.kron(eye, c).astype(BF16), jnp.kron(eye, s).astype(BF16)


def _outproj_kernel(*refs, n_act):
    x_ref, mod_ref = refs[0], refs[1]
    act = refs[2:2 + 3 * n_act]
    o_ref = refs[2 + 3 * n_act]
    is_ctx = pl.program_id(0) < T_CTX // TM
    y = None
    for a in range(n_act):
        ctx_ref, smp_ref, w_ref = act[3 * a:3 * a + 3]
        lhs = jnp.where(is_ctx, ctx_ref[...], smp_ref[...])
        t = _dot(lhs, w_ref[...])
        y = t if y is None else y + t
    o_ref[...] = x_ref[...] + mod_ref[5:6, :] * y


def _outproj(x, mod_l, acts):
    n_ctx = T_CTX // TM
    in_specs = [
        pl.BlockSpec((TM, D_MODEL), lambda i: (i, 0)),
        pl.BlockSpec((None, N_MOD, D_MODEL), lambda i: (_cond_row(i), 0, 0)),
    ]
    args = [x, mod_l]
    for ctx, smp, w, idx, rb in acts:
        kdim = ctx.shape[1]
        in_specs += [
            pl.BlockSpec((TM, kdim), lambda i: (jnp.minimum(i, n_ctx - 1), 0)),
            pl.BlockSpec((TM, kdim), lambda i: (jnp.maximum(i - n_ctx, 0), 0)),
            pl.BlockSpec((None, kdim, D_MODEL), lambda i, idx=idx, rb=rb: (idx, rb, 0)),
        ]
        args += [ctx, smp, w]
    return pl.pallas_call(
        functools.partial(_outproj_kernel, n_act=len(acts)),
        out_shape=jax.ShapeDtypeStruct((T_ALL, D_MODEL), F32),
        grid=(T_ALL // TM,),
        in_specs=in_specs,
        out_specs=pl.BlockSpec((TM, D_MODEL), lambda i: (i, 0)),
        compiler_params=_cparams(("parallel",)),
        name="outproj",
    )(*args)


def _mla_in_kernel(x_ref, mod_ref, g_ref, w_ref, gq_ref, gkv_ref, cq_ref, ckv_ref, kpe_ref):
    h = _norm_mod(x_ref[...], g_ref[...], mod_ref, 3).astype(BF16)
    res = _dot(h, w_ref[...])
    cq = res[:, :Q_LORA]
    cq_ref[...] = (cq * _rms(cq, Q_LORA) * gq_ref[...]).astype(BF16)
    ckv = res[:, Q_LORA:Q_LORA + KV_LORA]
    ckv_ref[...] = ckv * _rms(ckv, KV_LORA) * gkv_ref[...]
    kpe_ref[...] = res[:, Q_LORA + KV_LORA:]


def _mla_in(x, mod_l, g_norm_l, w_pad, gq, gkv, o):
    tm = 256
    n = Q_LORA + KV_LORA + LANE
    return pl.pallas_call(
        _mla_in_kernel,
        out_shape=(jax.ShapeDtypeStruct((T_ALL, Q_LORA), BF16),
                   jax.ShapeDtypeStruct((T_ALL, KV_LORA), F32),
                   jax.ShapeDtypeStruct((T_ALL, LANE), F32)),
        grid=(T_ALL // tm,),
        in_specs=[
            pl.BlockSpec((tm, D_MODEL), lambda i: (i, 0)),
            pl.BlockSpec((None, N_MOD, D_MODEL), lambda i: (_cond_row(i // (TM // tm)), 0, 0)),
            pl.BlockSpec((None, 1, D_MODEL), lambda i: (1, 0, 0)),
            pl.BlockSpec((None, D_MODEL, n), lambda i: (o, 0, 0)),
            pl.BlockSpec((None, 1, Q_LORA), lambda i: (o, 0, 0)),
            pl.BlockSpec((None, 1, KV_LORA), lambda i: (o, 0, 0)),
        ],
        out_specs=(pl.BlockSpec((tm, Q_LORA), lambda i: (i, 0)),
                   pl.BlockSpec((tm, KV_LORA), lambda i: (i, 0)),
                   pl.BlockSpec((tm, LANE), lambda i: (i, 0))),
        compiler_params=_cparams(("parallel",)),
        name="mla_in",
    )(x, mod_l, g_norm_l, w_pad, gq, gkv)


MLA_HB = 4


def _mla_q_kernel(cq_ref, w_ref, g_ref, *rest, use_rope):
    if use_rope:
        cos_ref, sa_ref, sb_ref, o_ref = rest
    else:
        (o_ref,) = rest
    cq = cq_ref[...]
    g = g_ref[...]
    for hh in range(MLA_HB):
        hs = slice(hh * HEAD_PAD, (hh + 1) * HEAD_PAD)
        xh = _dot(cq, w_ref[:, hs])
        y = xh * _rms(xh, DQK_C) * g
        yr = y[:, LANE:]
        if use_rope:
            yr = _rope(yr, cos_ref[...], sa_ref[...], sb_ref[...], ROPE_C // 4)
        o_ref[:, hh * HEAD_PAD:hh * HEAD_PAD + LANE] = y[:, :LANE].astype(BF16)
        o_ref[:, hh * HEAD_PAD + LANE:(hh + 1) * HEAD_PAD] = yr.astype(BF16)


def _mla_q(ps, cqn, w_uq_pad, gq_pad, o, tabs):
    tn = MLA_HB * HEAD_PAD
    in_specs = [
        pl.BlockSpec((TM, Q_LORA), lambda i, j: (ps.tile0 + i, 0)),
        pl.BlockSpec((None, Q_LORA, tn), lambda i, j: (o, 0, j)),
        pl.BlockSpec((None, 1, HEAD_PAD), lambda i, j: (o, 0, 0)),
    ]
    args = [cqn, w_uq_pad, gq_pad]
    if ps.latent:
        in_specs += [pl.BlockSpec((TM, LANE), lambda i, j: (i % (DEC_SEQ // TM), 0))] * 3
        args += list(tabs)
    return pl.pallas_call(
        functools.partial(_mla_q_kernel, use_rope=ps.latent),
        out_shape=jax.ShapeDtypeStruct((ps.rows, H_C * HEAD_PAD), BF16),
        grid=(ps.tiles, H_C // MLA_HB),
        in_specs=in_specs,
        out_specs=pl.BlockSpec((TM, tn), lambda i, j: (i, j)),
        compiler_params=_cparams(("parallel", "parallel")),
        name="mla_q",
    )(*args)


def _mla_kv_kernel(ckv_ref, kpe_ref, w_ref, g_ref, *rest, use_rope):
    if use_rope:
        cos_ref, sa_ref, sb_ref, k_ref, v_ref = rest
    else:
        k_ref, v_ref = rest
    ckv = ckv_ref[...].astype(BF16)
    g = g_ref[...]
    kpe = kpe_ref[...]
    ss_pe = jnp.sum(kpe * kpe, axis=-1, keepdims=True)
    kr = kpe * g[:, LANE:]
    if use_rope:
        kr = _rope(kr, cos_ref[...], sa_ref[...], sb_ref[...], ROPE_C // 4)
    for hh in range(MLA_HB):
        res = _dot(ckv, w_ref[:, hh * 2 * LANE:(hh + 1) * 2 * LANE])
        k_nope = res[:, :LANE]
        ss = jnp.sum(k_nope * k_nope, axis=-1, keepdims=True) + ss_pe
        r = lax.rsqrt(ss * (1.0 / DQK_C) + EPS)
        k_ref[:, hh * HEAD_PAD:hh * HEAD_PAD + LANE] = (k_nope * r * g[:, :LANE]).astype(BF16)
        k_ref[:, hh * HEAD_PAD + LANE:(hh + 1) * HEAD_PAD] = (kr * r).astype(BF16)
        v_ref[:, hh * DV_C:(hh + 1) * DV_C] = res[:, LANE:].astype(BF16)


def _mla_kv(ckv, kpe_pad, tile0, rows, w_ukv, gk_pad, o, tabs):
    use_rope = tabs is not None
    in_specs = [
        pl.BlockSpec((TM, KV_LORA), lambda i, j: (tile0 + i, 0)),
        pl.BlockSpec((TM, LANE), lambda i, j: (tile0 + i, 0)),
        pl.BlockSpec((None, KV_LORA, MLA_HB * 2 * LANE), lambda i, j: (o, 0, j)),
        pl.BlockSpec((None, 1, HEAD_PAD), lambda i, j: (o, 0, 0)),
    ]
    args = [ckv, kpe_pad, w_ukv, gk_pad]
    if use_rope:
        in_specs += [pl.BlockSpec((TM, LANE), lambda i, j: (i % (DEC_SEQ // TM), 0))] * 3
        args += list(tabs)
    return pl.pallas_call(
        functools.partial(_mla_kv_kernel, use_rope=use_rope),
        out_shape=(jax.ShapeDtypeStruct((rows, H_C * HEAD_PAD), BF16),
                   jax.ShapeDtypeStruct((rows, H_C * DV_C), BF16)),
        grid=(rows // TM, H_C // MLA_HB),
        in_specs=in_specs,
        out_specs=(pl.BlockSpec((TM, MLA_HB * HEAD_PAD), lambda i, j: (i, j)),
                   pl.BlockSpec((TM, MLA_HB * DV_C), lambda i, j: (i, j))),
        compiler_params=_cparams(("parallel", "parallel")),
        name="mla_kv",
    )(*args)


def _mla_attn_kernel(q_ref, k_ref, v_ref, *rest, heads, has_ctx, sub):
    if has_ctx:
        ck_ref, cv_ref, o_ref = rest
    else:
        (o_ref,) = rest
    c = DQK_C ** -0.5 * LOG2E
    for hh in range(heads):
        ks = slice(hh * HEAD_PAD, (hh + 1) * HEAD_PAD)
        vs = slice(hh * DV_C, (hh + 1) * DV_C)
        for r in range(q_ref.shape[0] // sub):
            rs = slice(r * sub, (r + 1) * sub)
            q = q_ref[rs, ks]
            s = _dot_nt(q, k_ref[:, ks])
            s2 = _dot_nt(q, ck_ref[:, ks]) if has_ctx else None
            e, e2, inv = _softmax_exp(s, s2, c)
            o = _dot(e.astype(BF16), v_ref[:, vs])
            if has_ctx:
                o = o + _dot(e2.astype(BF16), cv_ref[:, vs])
            o_ref[rs, vs] = (o * inv).astype(BF16)


def _mla_attn_ctx(q, k, v):
    heads = 8
    return pl.pallas_call(
        functools.partial(_mla_attn_kernel, heads=heads, has_ctx=False, sub=SEQ),
        out_shape=jax.ShapeDtypeStruct((T_CTX, H_C * DV_C), BF16),
        grid=(BATCH, H_C // heads),
        in_specs=[
            pl.BlockSpec((SEQ, heads * HEAD_PAD), lambda b, j: (b, j)),
            pl.BlockSpec((SEQ, heads * HEAD_PAD), lambda b, j: (b, j)),
            pl.BlockSpec((SEQ, heads * DV_C), lambda b, j: (b, j)),
        ],
        out_specs=pl.BlockSpec((SEQ, heads * DV_C), lambda b, j: (b, j)),
        compiler_params=_cparams(("parallel", "parallel")),
        name="mla_attn_ctx",
    )(q, k, v)


def _mla_attn_smp(q, k, v, ck, cv):
    tq, sub = 512, 256
    nq = DEC_SEQ // tq
    return pl.pallas_call(
        functools.partial(_mla_attn_kernel, heads=1, has_ctx=True, sub=sub),
        out_shape=jax.ShapeDtypeStruct((T_SMP, H_C * DV_C), BF16),
        grid=(DEC_BATCH, H_C, nq),
        in_specs=[
            pl.BlockSpec((tq, HEAD_PAD), lambda b, h, i: (b * nq + i, h)),
            pl.BlockSpec((DEC_SEQ, HEAD_PAD), lambda b, h, i: (b, h)),
            pl.BlockSpec((DEC_SEQ, DV_C), lambda b, h, i: (b, h)),
            pl.BlockSpec((PAST_LEN, HEAD_PAD), lambda b, h, i: (b, h)),
            pl.BlockSpec((PAST_LEN, DV_C), lambda b, h, i: (b, h)),
        ],
        out_specs=pl.BlockSpec((tq, DV_C), lambda b, h, i: (b * nq + i, h)),
        compiler_params=_cparams(("parallel", "parallel", "arbitrary")),
        name="mla_attn_smp",
    )(q, k, v, ck, cv)


def _axial_angles(rows, rot_dim):
    row = jnp.repeat(jnp.arange(rows, dtype=F32), GRID_W)
    col = jnp.tile(jnp.arange(GRID_W, dtype=F32), rows)
    half = rot_dim // 2
    inv_freq = ROPE_BASE ** (-jnp.arange(0, half, 2, dtype=F32) / half)
    ang_r = row[:, None] * inv_freq[None, :]
    ang_c = col[:, None] * inv_freq[None, :]
    return jnp.concatenate([ang_r, ang_r, ang_c, ang_c], axis=-1)


def _rope_tables(rows, rot_dim):
    ang = _axial_angles(rows, rot_dim)
    q = rot_dim // 4
    blk = (jnp.arange(rot_dim) // q) % 2
    cos, sin = jnp.cos(ang), jnp.sin(ang)
    sin_a = jnp.where(blk == 0, -sin, 0.0)
    sin_b = jnp.where(blk == 1, sin, 0.0)

    def full(t, fill):
        return jnp.pad(t, ((0, 0), (0, LANE - rot_dim)), constant_values=fill)

    return full(cos, 1.0), full(sin_a, 0.0), full(sin_b, 0.0)


def kernel(x_prompt, x_sample, c, cache_diff_k, cache_diff_v, cache_mla_ckv, cache_mla_kpe, c_ctx, w_mod, b_mod, g_norm, w_ffn_gate, w_ffn_up, w_ffn_down, w_in_ab, w_out_ab, g_qk_diff, diff_lambda, g_diff_sub, w_in_mla, g_q_lora, w_uq, g_kv_lora, w_ukv, g_qk_mla, w_o_mla):
    rows = DEC_SEQ // GRID_W
    tabs_a = _rope_tables(rows, DH_A)
    tabs_c = _rope_tables(rows, ROPE_C)
    dft_ch = _dft_channel_mats()
    dft_ctx = tuple(m.astype(BF16) for m in _dft_mats(SEQ))
    dft_smp = tuple(m.astype(BF16) for m in _dft_mats(DEC_SEQ))
    p_ctx, p_smp = _Pass(False), _Pass(True)

    wg, wu, wd = w_ffn_gate.astype(BF16), w_ffn_up.astype(BF16), w_ffn_down.astype(BF16)
    w_in_ab_b, w_out_ab_b = w_in_ab.astype(BF16), w_out_ab.astype(BF16)
    w_in_mla_b = jnp.pad(w_in_mla.astype(BF16), ((0, 0), (0, 0), (0, LANE - ROPE_C)))
    w_uq_b = jnp.pad(w_uq.astype(BF16).reshape(N_ODD, Q_LORA, H_C, DQK_C),
                     ((0, 0), (0, 0), (0, 0), (0, HEAD_PAD - DQK_C))).reshape(N_ODD, Q_LORA, H_C * HEAD_PAD)
    w_ukv_b, w_o_b = w_ukv.astype(BF16), w_o_mla.astype(BF16)
    g_qk_mla_pad = jnp.pad(g_qk_mla, ((0, 0), (0, 0), (0, HEAD_PAD - DQK_C)))[:, :, None, :]
    gq_mla, gk_mla = g_qk_mla_pad[:, 0], g_qk_mla_pad[:, 1]
    g_norm4 = g_norm[:, :, None, :]
    gqk_diff = g_qk_diff[:, :, None, :]
    g_sub3 = g_diff_sub[:, None, :]
    g_q_lora3, g_kv_lora3 = g_q_lora[:, None, :], g_kv_lora[:, None, :]

    cond8 = jnp.concatenate([c_ctx[None, :], c, jnp.zeros((8 - N_COND, D_MODEL), F32)], axis=0)
    mod = _modulation(cond8, w_mod, b_mod)[:, :N_COND].reshape(DEPTH, N_COND, N_MOD, D_MODEL)

    x = jnp.concatenate([x_prompt.reshape(T_CTX, D_MODEL), x_sample.reshape(T_SMP, D_MODEL)], axis=0)
    new_kv = None
    mla_ckv, mla_kpe = [], []

    for l in range(DEPTH):
        mod_l = mod[l]
        x = _ffn(x, mod_l, g_norm4[l], wg, wu, wd, l, 0)
        if l % 2 == 0:
            e = l // 2
            lam_init = 0.8 - 0.6 * math.exp(-0.3 * l)
            q_c, k_c, v_c, f_c, nk, nv = _inproj_even(p_ctx, x, mod_l, g_norm4[l], w_in_ab_b, gqk_diff, e,
                                                     None, new_kv)
            new_kv = (nk, nv)
            q_s, k_s, v_s, f_s = _inproj_even(p_smp, x, mod_l, g_norm4[l], w_in_ab_b, gqk_diff, e, tabs_a, None)
            o_ctx = _diff_attn_ctx(q_c, k_c, v_c, diff_lambda, g_sub3, e, lam_init)
            o_smp = _diff_attn_smp(q_s, k_s, v_s, cache_diff_k, cache_diff_v, diff_lambda, g_sub3, e, lam_init)
            fo_ctx = _fourier(f_c, SEQ, dft_ch, dft_ctx)
            fo_smp = _fourier(f_s, DEC_SEQ, dft_ch, dft_smp)
            x = _outproj(x, mod_l, [(o_ctx, o_smp, w_out_ab_b, e, 0), (fo_ctx, fo_smp, w_out_ab_b, e, W_A // W_B)])
        else:
            o = l // 2
            cqn, ckvn, kpe_pad = _mla_in(x, mod_l, g_norm4[l], w_in_mla_b, g_q_lora3, g_kv_lora3, o)
            q_c = _mla_q(p_ctx, cqn, w_uq_b, gq_mla, o, None)
            q_s = _mla_q(p_smp, cqn, w_uq_b, gq_mla, o, tabs_c)
            k_c, v_c = _mla_kv(ckvn, kpe_pad, p_ctx.tile0, T_CTX, w_ukv_b, gk_mla, o, None)
            k_s, v_s = _mla_kv(ckvn, kpe_pad, p_smp.tile0, T_SMP, w_ukv_b, gk_mla, o, tabs_c)
            cache_ckv = cache_mla_ckv[:, o].reshape(DEC_BATCH * PAST_LEN, KV_LORA)
            cache_kpe = jnp.pad(cache_mla_kpe[:, o].reshape(DEC_BATCH * PAST_LEN, ROPE_C),
                                ((0, 0), (0, LANE - ROPE_C)))
            ck, cv = _mla_kv(cache_ckv, cache_kpe, 0, DEC_BATCH * PAST_LEN, w_ukv_b, gk_mla, o, None)
            a_ctx = _mla_attn_ctx(q_c, k_c, v_c)
            a_smp = _mla_attn_smp(q_s, k_s, v_s, ck, cv)
            x = _outproj(x, mod_l, [(a_ctx, a_smp, w_o_b, o, 0)])
            mla_ckv.append(ckvn[:T_CTX].reshape(BATCH, SEQ, KV_LORA))
            mla_kpe.append(kpe_pad[:T_CTX, :ROPE_C].reshape(BATCH, SEQ, ROPE_C))
        x = _ffn(x, mod_l, g_norm4[l], wg, wu, wd, l, 1)

    y_prompt = x[:T_CTX].reshape(BATCH, SEQ, D_MODEL)
    y_sample = x[T_CTX:].reshape(DEC_BATCH, DEC_SEQ, D_MODEL)
    return (y_prompt, y_sample, new_kv[0], new_kv[1], jnp.stack(mla_ckv, axis=1), jnp.stack(mla_kpe, axis=1))
```

```python
import functools
import math

import jax
import jax.numpy as jnp
from jax import lax
from jax.experimental import pallas as pl
from jax.experimental.pallas import tpu as pltpu

F32 = jnp.float32
BF16 = jnp.bfloat16

D_MODEL = 2048
BATCH = 32
SEQ = 256
DEPTH = 4
DEC_BATCH = 2
DEC_SEQ = 2048
PAST_LEN = 256
GRID_W = 64
N_EVEN = 2
N_ODD = 2
H_A = 6
DH_A = 128
W_A = H_A * 2 * DH_A
G_B = 4
DG_B = 128
W_B = G_B * DG_B
H_C = 16
Q_LORA = 1536
KV_LORA = 512
NOPE_C = 128
ROPE_C = 64
DQK_C = NOPE_C + ROPE_C
DV_C = 128
D_FF = 5632
N_MOD = 9
ROPE_BASE = 10000.0
EPS = 1e-6
LOG2E = 1.4426950408889634

T_CTX = BATCH * SEQ
T_SMP = DEC_BATCH * DEC_SEQ
T_ALL = T_CTX + T_SMP
N_COND = 1 + DEC_BATCH
LANE = 128
HEAD_PAD = 256
TM = 512

VMEM_LIMIT = 52 * 1024 * 1024
FFN_TM = 1024
FFN_VMEM_LIMIT = 60 * 1024 * 1024


def _cparams(sem):
    return pltpu.CompilerParams(dimension_semantics=sem, vmem_limit_bytes=VMEM_LIMIT)


def _cond_row(i, tm=TM):
    n_ctx = T_CTX // tm
    per = DEC_SEQ // tm
    return jnp.where(i < n_ctx, 0, 1 + (i - n_ctx) // per)


class _Pass:
    def __init__(self, latent):
        self.latent = latent
        self.rows = T_SMP if latent else T_CTX
        self.tile0 = (T_CTX // TM) if latent else 0
        self.tiles = self.rows // TM
        self.seq = DEC_SEQ if latent else SEQ
        self.batch = DEC_BATCH if latent else BATCH

    def cond(self, i):
        return 1 + i // (DEC_SEQ // TM) if self.latent else 0


def _dot(a, b):
    return jnp.dot(a, b, preferred_element_type=F32)


def _dot_nt(a, b):
    return lax.dot_general(a, b, (((1,), (1,)), ((), ())), preferred_element_type=F32)


def _silu(x):
    return x / (1.0 + jnp.exp(-x))


def _rms(x, denom):
    return lax.rsqrt(jnp.sum(x * x, axis=-1, keepdims=True) * (1.0 / denom) + EPS)


def _norm_mod(x, g, mod_ref, base):
    sh = mod_ref[base:base + 1, :]
    sc = mod_ref[base + 1:base + 2, :]
    return (x * _rms(x, D_MODEL) * g) * (1.0 + sc) + sh


def _rope(y, cos, sin_a, sin_b, q):
    return y * cos + pltpu.roll(y, LANE - q, 1) * sin_a + pltpu.roll(y, q, 1) * sin_b


def _softmax_exp(s, s2, c):
    m = jnp.max(s, axis=-1, keepdims=True)
    if s2 is not None:
        m = jnp.maximum(m, jnp.max(s2, axis=-1, keepdims=True))
    e = jnp.exp2((s - m) * c)
    l = jnp.sum(e, axis=-1, keepdims=True)
    e2 = None
    if s2 is not None:
        e2 = jnp.exp2((s2 - m) * c)
        l = l + jnp.sum(e2, axis=-1, keepdims=True)
    return e, e2, 1.0 / l


def _mod_kernel(c_ref, w_ref, b_ref, o_ref):
    a = _silu(c_ref[...]).astype(BF16)
    o_ref[...] = _dot(a, w_ref[...].astype(BF16)) + b_ref[...]


def _modulation(cond8, w_mod, b_mod):
    tn = 1024
    n = N_MOD * D_MODEL
    return pl.pallas_call(
        _mod_kernel,
        out_shape=jax.ShapeDtypeStruct((DEPTH, 8, n), F32),
        grid=(DEPTH, n // tn),
        in_specs=[
            pl.BlockSpec((8, D_MODEL), lambda l, j: (0, 0)),
            pl.BlockSpec((None, D_MODEL, tn), lambda l, j: (l, 0, j)),
            pl.BlockSpec((None, 1, tn), lambda l, j: (l, 0, j)),
        ],
        out_specs=pl.BlockSpec((None, 8, tn), lambda l, j: (l, 0, j)),
        compiler_params=_cparams(("parallel", "parallel")),
        name="modulation",
    )(cond8, w_mod, b_mod.reshape(DEPTH, 1, n))


FFN_TF = 512
FFN_XC = 256
FFN_DC = 512
FFN_NR = 128
CAST_GU_ROWS = 16
CAST_D_ROWS = 64


def _prenorm_kernel(x_ref, mod_ref, g_ref, h_ref):
    h_ref[...] = _norm_mod(x_ref[...], g_ref[...], mod_ref, 0).astype(BF16)


def _prenorm(x, mod_l, g):
    return pl.pallas_call(
        _prenorm_kernel,
        out_shape=jax.ShapeDtypeStruct((T_ALL, D_MODEL), BF16),
        grid=(T_ALL // TM,),
        in_specs=[
            pl.BlockSpec((TM, D_MODEL), lambda i: (i, 0)),
            pl.BlockSpec((None, N_MOD, D_MODEL), lambda i: (_cond_row(i), 0, 0)),
            pl.BlockSpec((None, 1, D_MODEL), lambda i: (0, 0, 0)),
        ],
        out_specs=pl.BlockSpec((TM, D_MODEL), lambda i: (i, 0)),
        compiler_params=_cparams(("parallel",)),
        name="prenorm",
    )(x, mod_l, g)


def _ffn_kernel(*refs, base, nf, cast_next, emit_h):
    it = iter(refs)
    h_ref, x_ref, mod_ref, wg_ref, wu_ref, wd_ref = [next(it) for _ in range(6)]
    nxt = [next(it) for _ in range(3)] if cast_next else []
    if emit_h:
        modn_ref, gn_ref = next(it), next(it)
    o_ref = next(it)
    cst = [next(it) for _ in range(3)] if cast_next else []
    if emit_h:
        hn_ref = next(it)
    f = pl.program_id(1)

    @pl.when(f == 0)
    def _():
        o_ref[...] = jnp.zeros_like(o_ref)

    h = h_ref[...]
    gate = _dot(h, wg_ref[...])
    up = _dot(h, wu_ref[...])
    a = (_silu(gate) * up).astype(BF16)
    for c in range(D_MODEL // FFN_DC):
        cs = slice(c * FFN_DC, (c + 1) * FFN_DC)
        o_ref[:, cs] += (0.5 * mod_ref[base + 2:base + 3, cs]) * _dot(a, wd_ref[:, cs])

    for c in range(D_MODEL // FFN_XC):
        @pl.when(f == c)
        def _(c=c):
            o_ref[:, c * FFN_XC:(c + 1) * FFN_XC] += x_ref[...]

    for src, dst in zip(nxt, cst):
        dst[...] = src[...].astype(BF16)

    if emit_h:
        @pl.when(f == nf - 1)
        def _():
            def rows(r, carry):
                rs = pl.ds(pl.multiple_of(r * FFN_NR, FFN_NR), FFN_NR)
                hn_ref[rs, :] = _norm_mod(o_ref[rs, :], gn_ref[...], modn_ref, 0).astype(BF16)
                return carry
            lax.fori_loop(0, o_ref.shape[0] // FFN_NR, rows, 0)


def _ffn(h, x, mod_l, w_b, k, nxt, hn):
    tm, tf = FFN_TM, FFN_TF
    nf = D_FF // tf
    n_gu, n_d = D_MODEL // CAST_GU_ROWS, D_FF // CAST_D_ROWS
    assert (T_ALL // tm) * nf >= max(n_gu, n_d) and nf >= D_MODEL // FFN_XC
    wg, wu, wd = w_b
    in_specs = [
        pl.BlockSpec((tm, D_MODEL), lambda i, f: (i, 0)),
        pl.BlockSpec((tm, FFN_XC), lambda i, f: (i, jnp.minimum(f, D_MODEL // FFN_XC - 1))),
        pl.BlockSpec((None, N_MOD, D_MODEL), lambda i, f: (_cond_row(i, tm), 0, 0)),
        pl.BlockSpec((D_MODEL, tf), lambda i, f: (0, f)),
        pl.BlockSpec((D_MODEL, tf), lambda i, f: (0, f)),
        pl.BlockSpec((tf, D_MODEL), lambda i, f: (f, 0)),
    ]
    args = [h, x, mod_l, wg, wu, wd]
    out_shape = [jax.ShapeDtypeStruct((T_ALL, D_MODEL), F32)]
    out_specs = [pl.BlockSpec((tm, D_MODEL), lambda i, f: (i, 0))]
    if nxt is not None:
        ng, nu, nd, l2, k2 = nxt
        gu_in = pl.BlockSpec((None, None, CAST_GU_ROWS, D_FF),
                             lambda i, f: (l2, k2, jnp.minimum(i * nf + f, n_gu - 1), 0))
        gu_out = pl.BlockSpec((CAST_GU_ROWS, D_FF), lambda i, f: (jnp.minimum(i * nf + f, n_gu - 1), 0))
        in_specs += [gu_in, gu_in, pl.BlockSpec((None, None, CAST_D_ROWS, D_MODEL),
                                                lambda i, f: (l2, k2, jnp.minimum(i * nf + f, n_d - 1), 0))]
        args += [ng, nu, nd]
        out_shape += [jax.ShapeDtypeStruct((D_MODEL, D_FF), BF16)] * 2 + [jax.ShapeDtypeStruct((D_FF, D_MODEL), BF16)]
        out_specs += [gu_out, gu_out,
                      pl.BlockSpec((CAST_D_ROWS, D_MODEL), lambda i, f: (jnp.minimum(i * nf + f, n_d - 1), 0))]
    if hn is not None:
        in_specs += [pl.BlockSpec((None, N_MOD, D_MODEL), lambda i, f: (_cond_row(i, tm), 0, 0)),
                     pl.BlockSpec((None, 1, D_MODEL), lambda i, f: (0, 0, 0))]
        args += list(hn)
        out_shape += [jax.ShapeDtypeStruct((T_ALL, D_MODEL), BF16)]
        out_specs += [pl.BlockSpec((tm, D_MODEL), lambda i, f: (i, 0), pipeline_mode=pl.Buffered(1))]
    return pl.pallas_call(
        functools.partial(_ffn_kernel, base=6 * k, nf=nf, cast_next=nxt is not None, emit_h=hn is not None),
        out_shape=out_shape,
        grid=(T_ALL // tm, nf),
        in_specs=in_specs,
        out_specs=out_specs,
        compiler_params=pltpu.CompilerParams(dimension_semantics=("arbitrary", "arbitrary"),
                                             vmem_limit_bytes=FFN_VMEM_LIMIT),
        name="ffn",
    )(*args)


QKV_TN = W_A // 2
F_TN = W_B // 2


def _inproj_even_kernel(*refs, latent):
    x_ref, mod_ref, g_ref, wq_ref, wk_ref, wv_ref, wf_ref, gq_ref, gk_ref = refs[:9]
    refs = refs[9:]
    if latent:
        cos_ref, sa_ref, sb_ref, q_ref, k_ref, v_ref, f_ref, h_ref = refs
    else:
        q_ref, k_ref, v_ref, f_ref, nk_ref, nv_ref, h_ref = refs[-7:]

    @pl.when(pl.program_id(1) == 0)
    def _():
        h_ref[...] = _norm_mod(x_ref[...], g_ref[...], mod_ref, 3).astype(BF16)

    h = h_ref[...]
    for w_ref, gn_ref, o_ref, is_k in ((wq_ref, gq_ref, q_ref, False), (wk_ref, gk_ref, k_ref, True)):
        res = _dot(h, w_ref[...])
        g = gn_ref[...]
        for ch in range(QKV_TN // DH_A):
            xc = res[:, ch * DH_A:(ch + 1) * DH_A]
            y = xc * _rms(xc, DH_A) * g
            if latent:
                y = _rope(y, cos_ref[...], sa_ref[...], sb_ref[...], DH_A // 4)
            o_ref[:, ch * DH_A:(ch + 1) * DH_A] = y.astype(BF16)
            if is_k and not latent:
                for b in range(TM // SEQ):
                    nk_ref[b, ch // 2, :, (ch % 2) * DH_A:(ch % 2 + 1) * DH_A] = y[b * SEQ:(b + 1) * SEQ, :]
    res = _dot(h, wv_ref[...])
    v_ref[...] = res.astype(BF16)
    if not latent:
        for b in range(TM // SEQ):
            for hh in range(QKV_TN // (2 * DH_A)):
                nv_ref[b, hh] = res[b * SEQ:(b + 1) * SEQ, hh * 2 * DH_A:(hh + 1) * 2 * DH_A]
    f_ref[...] = _dot(h, wf_ref[...]).astype(BF16)


def _inproj_even(ps, x, mod_l, g_norm_l, w_in, gqk, e, tabs, prev_kv):
    nj = W_A // QKV_TN
    hb = QKV_TN // (2 * DH_A)
    in_specs = [
        pl.BlockSpec((TM, D_MODEL), lambda i, j: (ps.tile0 + i, 0)),
        pl.BlockSpec((None, N_MOD, D_MODEL), lambda i, j: (ps.cond(i), 0, 0)),
        pl.BlockSpec((None, 1, D_MODEL), lambda i, j: (1, 0, 0)),
        pl.BlockSpec((None, D_MODEL, QKV_TN), lambda i, j: (e, 0, j)),
        pl.BlockSpec((None, D_MODEL, QKV_TN), lambda i, j: (e, 0, nj + j)),
        pl.BlockSpec((None, D_MODEL, QKV_TN), lambda i, j: (e, 0, 2 * nj + j)),
        pl.BlockSpec((None, D_MODEL, F_TN), lambda i, j: (e, 0, 3 * W_A // F_TN + j)),
        pl.BlockSpec((None, None, 1, DH_A), lambda i, j: (e, 0, 0, 0)),
        pl.BlockSpec((None, None, 1, DH_A), lambda i, j: (e, 1, 0, 0)),
    ]
    args = [x, mod_l, g_norm_l, w_in, w_in, w_in, w_in, gqk, gqk]
    out_shape = [jax.ShapeDtypeStruct((ps.rows, W_A), BF16)] * 3 + [jax.ShapeDtypeStruct((ps.rows, W_B), BF16)]
    out_specs = [pl.BlockSpec((TM, QKV_TN), lambda i, j: (i, j))] * 3 + [pl.BlockSpec((TM, F_TN), lambda i, j: (i, j))]
    aliases = {}
    if ps.latent:
        tab_spec = pl.BlockSpec((TM, LANE), lambda i, j: (i % (DEC_SEQ // TM), 0))
        in_specs += [tab_spec] * 3
        args += list(tabs)
    else:
        kv_shape = jax.ShapeDtypeStruct((BATCH, N_EVEN, H_A, SEQ, 2 * DH_A), F32)
        kv_spec = pl.BlockSpec((TM // SEQ, None, hb, SEQ, 2 * DH_A), lambda i, j: (i, e, j, 0, 0))
        out_shape += [kv_shape, kv_shape]
        out_specs += [kv_spec, kv_spec]
        if prev_kv is not None:
            aliases = {len(args): 4, len(args) + 1: 5}
            in_specs += [pl.BlockSpec(memory_space=pl.ANY)] * 2
            args += list(prev_kv)
    return pl.pallas_call(
        functools.partial(_inproj_even_kernel, latent=ps.latent),
        out_shape=out_shape,
        grid=(ps.tiles, nj),
        in_specs=in_specs,
        out_specs=out_specs,
        scratch_shapes=[pltpu.VMEM((TM, D_MODEL), BF16)],
        input_output_aliases=aliases,
        compiler_params=_cparams(("parallel", "arbitrary")),
        name="inproj_even",
    )(*args)


def _diff_attn_kernel(lam_ref, gsub_ref, q_ref, k_ref, v_ref, *rest, lam_init, heads, has_ctx, sub):
    if has_ctx:
        ck_ref, cv_ref, o_ref = rest
    else:
        (o_ref,) = rest
    lv = lam_ref[...]
    lam = (jnp.exp(jnp.sum(lv[0:1] * lv[1:2], axis=-1, keepdims=True))
           - jnp.exp(jnp.sum(lv[2:3] * lv[3:4], axis=-1, keepdims=True)) + lam_init)
    c = DH_A ** -0.5 * LOG2E
    dh2 = 2 * DH_A
    gsub = gsub_ref[...] * (1.0 - lam_init)
    for hh in range(heads):
        hs = slice(hh * dh2, (hh + 1) * dh2)
        k = k_ref[:, hs]
        v = v_ref[:, hs]
        if has_ctx:
            ck = ck_ref[hh].astype(BF16)
            cv = cv_ref[hh].astype(BF16)
        for r in range(q_ref.shape[0] // sub):
            rs = slice(r * sub, (r + 1) * sub)
            q = q_ref[rs, hs]
            a, a2 = None, None
            for m in range(2):
                sl = slice(m * DH_A, (m + 1) * DH_A)
                s = _dot_nt(q[:, sl], k[:, sl])
                s2 = _dot_nt(q[:, sl], ck[:, sl]) if has_ctx else None
                e, e2, inv = _softmax_exp(s, s2, c)
                w = inv if m == 0 else -(lam * inv)
                a = e * w if a is None else a + e * w
                if has_ctx:
                    a2 = e2 * w if a2 is None else a2 + e2 * w
            o = _dot(a.astype(BF16), v)
            if has_ctx:
                o = o + _dot(a2.astype(BF16), cv)
            o_ref[rs, hs] = (o * _rms(o, dh2) * gsub).astype(BF16)


def _diff_attn_ctx(q, k, v, lam_vecs, gsub, e, lam_init):
    spec = pl.BlockSpec((SEQ, W_A), lambda b: (b, 0))
    return pl.pallas_call(
        functools.partial(_diff_attn_kernel, lam_init=lam_init, heads=H_A, has_ctx=False, sub=SEQ),
        out_shape=jax.ShapeDtypeStruct((T_CTX, W_A), BF16),
        grid=(BATCH,),
        in_specs=[
            pl.BlockSpec((None, 4, DH_A), lambda b: (e, 0, 0)),
            pl.BlockSpec((None, 1, 2 * DH_A), lambda b: (e, 0, 0)),
            spec, spec, spec,
        ],
        out_specs=spec,
        compiler_params=_cparams(("parallel",)),
        name="diff_attn_ctx",
    )(lam_vecs, gsub, q, k, v)


def _diff_attn_smp(q, k, v, cache_k, cache_v, lam_vecs, gsub, e, lam_init):
    tq, sub = 1024, 256
    w = 2 * DH_A
    nq = DEC_SEQ // tq
    cache_spec = pl.BlockSpec((None, None, 1, PAST_LEN, w), lambda b, h, i: (b, e, h, 0, 0))
    return pl.pallas_call(
        functools.partial(_diff_attn_kernel, lam_init=lam_init, heads=1, has_ctx=True, sub=sub),
        out_shape=jax.ShapeDtypeStruct((T_SMP, W_A), BF16),
        grid=(DEC_BATCH, H_A, nq),
        in_specs=[
            pl.BlockSpec((None, 4, DH_A), lambda b, h, i: (e, 0, 0)),
            pl.BlockSpec((None, 1, w), lambda b, h, i: (e, 0, 0)),
            pl.BlockSpec((tq, w), lambda b, h, i: (b * nq + i, h)),
            pl.BlockSpec((DEC_SEQ, w), lambda b, h, i: (b, h)),
            pl.BlockSpec((DEC_SEQ, w), lambda b, h, i: (b, h)),
            cache_spec, cache_spec,
        ],
        out_specs=pl.BlockSpec((tq, w), lambda b, h, i: (b * nq + i, h)),
        compiler_params=_cparams(("parallel", "parallel", "arbitrary")),
        name="diff_attn_smp",
    )(lam_vecs, gsub, q, k, v, cache_k, cache_v)


def _fourier_kernel(x_ref, bc_ref, bs_ref, cs_ref, ss_ref, o_ref, y1_ref, y2_ref, *, scale):
    @pl.when(pl.program_id(1) == 0)
    def _():
        x = x_ref[...]
        y1_ref[...] = _dot(x, bc_ref[...]).astype(BF16)
        y2_ref[...] = _dot(x, bs_ref[...]).astype(BF16)

    out = _dot(cs_ref[...], y1_ref[...]) - _dot(ss_ref[...], y2_ref[...])
    o_ref[...] = (out * scale).astype(BF16)


def _fourier(f, s, dft_ch, dft_seq):
    tr = 256
    nb = f.shape[0] // s
    bc, bs = dft_ch
    cs, ss = dft_seq
    return pl.pallas_call(
        functools.partial(_fourier_kernel, scale=1.0 / math.sqrt(s * DG_B)),
        out_shape=jax.ShapeDtypeStruct(f.shape, BF16),
        grid=(nb, s // tr),
        in_specs=[
            pl.BlockSpec((s, W_B), lambda b, r: (b, 0)),
            pl.BlockSpec((W_B, W_B), lambda b, r: (0, 0)),
            pl.BlockSpec((W_B, W_B), lambda b, r: (0, 0)),
            pl.BlockSpec((tr, s), lambda b, r: (r, 0)),
            pl.BlockSpec((tr, s), lambda b, r: (r, 0)),
        ],
        out_specs=pl.BlockSpec((tr, W_B), lambda b, r: (b * (s // tr) + r, 0)),
        scratch_shapes=[pltpu.VMEM((s, W_B), BF16), pltpu.VMEM((s, W_B), BF16)],
        compiler_params=_cparams(("parallel", "arbitrary")),
        name="fourier",
    )(f, bc, bs, cs, ss)


def _dft_angles(rows, stride, n):
    j = jnp.arange(rows, dtype=jnp.int32)[:, None] * stride
    k = jnp.arange(n, dtype=jnp.int32)[None, :]
    return ((j * k) % n).astype(F32) * (2.0 * math.pi / n)


def _dft_mats(n):
    lo = min(n, 64)
    a0 = _dft_angles(lo, 1, n)
    c0, s0 = jnp.cos(a0), jnp.sin(a0)
    if lo == n:
        return c0, s0
    a1 = _dft_angles(n // lo, lo, n)
    c1, s1 = jnp.cos(a1)[:, None, :], jnp.sin(a1)[:, None, :]
    c = c1 * c0[None] - s1 * s0[None]
    s = s1 * c0[None] + c1 * s0[None]
    return c.reshape(n, n), s.reshape(n, n)


def _dft_channel_mats():
    c, s = _dft_mats(DG_B)
    eye = jnp.eye(G_B, dtype=F32)
    return jnp.kron(eye, c).astype(BF16), jnp.kron(eye, s).astype(BF16)


def _outproj_kernel(*refs, n_act):
    x_ref, mod_ref, g_ref = refs[:3]
    act = refs[3:3 + 3 * n_act]
    o_ref, hn_ref = refs[3 + 3 * n_act:]
    is_ctx = pl.program_id(0) < T_CTX // TM
    y = None
    for a in range(n_act):
        ctx_ref, smp_ref, w_ref = act[3 * a:3 * a + 3]
        lhs = jnp.where(is_ctx, ctx_ref[...], smp_ref[...])
        t = _dot(lhs, w_ref[...])
        y = t if y is None else y + t
    x_new = x_ref[...] + mod_ref[5:6, :] * y
    o_ref[...] = x_new
    hn_ref[...] = _norm_mod(x_new, g_ref[...], mod_ref, 6).astype(BF16)


def _outproj(x, mod_l, g_norm_l, acts):
    n_ctx = T_CTX // TM
    in_specs = [
        pl.BlockSpec((TM, D_MODEL), lambda i: (i, 0)),
        pl.BlockSpec((None, N_MOD, D_MODEL), lambda i: (_cond_row(i), 0, 0)),
        pl.BlockSpec((None, 1, D_MODEL), lambda i: (2, 0, 0)),
    ]
    args = [x, mod_l, g_norm_l]
    for ctx, smp, w, idx, rb in acts:
        kdim = ctx.shape[1]
        in_specs += [
            pl.BlockSpec((TM, kdim), lambda i: (jnp.minimum(i, n_ctx - 1), 0)),
            pl.BlockSpec((TM, kdim), lambda i: (jnp.maximum(i - n_ctx, 0), 0)),
            pl.BlockSpec((None, kdim, D_MODEL), lambda i, idx=idx, rb=rb: (idx, rb, 0)),
        ]
        args += [ctx, smp, w]
    row_spec = pl.BlockSpec((TM, D_MODEL), lambda i: (i, 0))
    return pl.pallas_call(
        functools.partial(_outproj_kernel, n_act=len(acts)),
        out_shape=(jax.ShapeDtypeStruct((T_ALL, D_MODEL), F32), jax.ShapeDtypeStruct((T_ALL, D_MODEL), BF16)),
        grid=(T_ALL // TM,),
        in_specs=in_specs,
        out_specs=(row_spec, row_spec),
        compiler_params=_cparams(("parallel",)),
        name="outproj",
    )(*args)


def _mla_in_kernel(x_ref, mod_ref, g_ref, w_ref, gq_ref, gkv_ref, cq_ref, ckv_ref, kpe_ref):
    h = _norm_mod(x_ref[...], g_ref[...], mod_ref, 3).astype(BF16)
    res = _dot(h, w_ref[...])
    cq = res[:, :Q_LORA]
    cq_ref[...] = (cq * _rms(cq, Q_LORA) * gq_ref[...]).astype(BF16)
    ckv = res[:, Q_LORA:Q_LORA + KV_LORA]
    ckv_ref[...] = ckv * _rms(ckv, KV_LORA) * gkv_ref[...]
    kpe_ref[...] = res[:, Q_LORA + KV_LORA:]


def _mla_in(x, mod_l, g_norm_l, w_pad, gq, gkv, o):
    tm = 256
    n = Q_LORA + KV_LORA + LANE
    return pl.pallas_call(
        _mla_in_kernel,
        out_shape=(jax.ShapeDtypeStruct((T_ALL, Q_LORA), BF16),
                   jax.ShapeDtypeStruct((T_ALL, KV_LORA), F32),
                   jax.ShapeDtypeStruct((T_ALL, LANE), F32)),
        grid=(T_ALL // tm,),
        in_specs=[
            pl.BlockSpec((tm, D_MODEL), lambda i: (i, 0)),
            pl.BlockSpec((None, N_MOD, D_MODEL), lambda i: (_cond_row(i // (TM // tm)), 0, 0)),
            pl.BlockSpec((None, 1, D_MODEL), lambda i: (1, 0, 0)),
            pl.BlockSpec((None, D_MODEL, n), lambda i: (o, 0, 0)),
            pl.BlockSpec((None, 1, Q_LORA), lambda i: (o, 0, 0)),
            pl.BlockSpec((None, 1, KV_LORA), lambda i: (o, 0, 0)),
        ],
        out_specs=(pl.BlockSpec((tm, Q_LORA), lambda i: (i, 0)),
                   pl.BlockSpec((tm, KV_LORA), lambda i: (i, 0)),
                   pl.BlockSpec((tm, LANE), lambda i: (i, 0))),
        compiler_params=_cparams(("parallel",)),
        name="mla_in",
    )(x, mod_l, g_norm_l, w_pad, gq, gkv)


MLA_HB = 4


def _mla_q_kernel(cq_ref, w_ref, g_ref, *rest, use_rope):
    if use_rope:
        cos_ref, sa_ref, sb_ref, o_ref = rest
    else:
        (o_ref,) = rest
    cq = cq_ref[...]
    g = g_ref[...]
    for hh in range(MLA_HB):
        hs = slice(hh * HEAD_PAD, (hh + 1) * HEAD_PAD)
        xh = _dot(cq, w_ref[:, hs])
        y = xh * _rms(xh, DQK_C) * g
        yr = y[:, LANE:]
        if use_rope:
            yr = _rope(yr, cos_ref[...], sa_ref[...], sb_ref[...], ROPE_C // 4)
        o_ref[:, hh * HEAD_PAD:hh * HEAD_PAD + LANE] = y[:, :LANE].astype(BF16)
        o_ref[:, hh * HEAD_PAD + LANE:(hh + 1) * HEAD_PAD] = yr.astype(BF16)


def _mla_q(ps, cqn, w_uq_pad, gq_pad, o, tabs):
    tn = MLA_HB * HEAD_PAD
    in_specs = [
        pl.BlockSpec((TM, Q_LORA), lambda i, j: (ps.tile0 + i, 0)),
        pl.BlockSpec((None, Q_LORA, tn), lambda i, j: (o, 0, j)),
        pl.BlockSpec((None, 1, HEAD_PAD), lambda i, j: (o, 0, 0)),
    ]
    args = [cqn, w_uq_pad, gq_pad]
    if ps.latent:
        in_specs += [pl.BlockSpec((TM, LANE), lambda i, j: (i % (DEC_SEQ // TM), 0))] * 3
        args += list(tabs)
    return pl.pallas_call(
        functools.partial(_mla_q_kernel, use_rope=ps.latent),
        out_shape=jax.ShapeDtypeStruct((ps.rows, H_C * HEAD_PAD), BF16),
        grid=(ps.tiles, H_C // MLA_HB),
        in_specs=in_specs,
        out_specs=pl.BlockSpec((TM, tn), lambda i, j: (i, j)),
        compiler_params=_cparams(("parallel", "parallel")),
        name="mla_q",
    )(*args)


def _mla_kv_kernel(ckv_ref, kpe_ref, w_ref, g_ref, *rest, use_rope):
    if use_rope:
        cos_ref, sa_ref, sb_ref, k_ref, v_ref = rest
    else:
        k_ref, v_ref = rest
    ckv = ckv_ref[...].astype(BF16)
    g = g_ref[...]
    kpe = kpe_ref[...]
    ss_pe = jnp.sum(kpe * kpe, axis=-1, keepdims=True)
    kr = kpe * g[:, LANE:]
    if use_rope:
        kr = _rope(kr, cos_ref[...], sa_ref[...], sb_ref[...], ROPE_C // 4)
    for hh in range(MLA_HB):
        res = _dot(ckv, w_ref[:, hh * 2 * LANE:(hh + 1) * 2 * LANE])
        k_nope = res[:, :LANE]
        ss = jnp.sum(k_nope * k_nope, axis=-1, keepdims=True) + ss_pe
        r = lax.rsqrt(ss * (1.0 / DQK_C) + EPS)
        k_ref[:, hh * HEAD_PAD:hh * HEAD_PAD + LANE] = (k_nope * r * g[:, :LANE]).astype(BF16)
        k_ref[:, hh * HEAD_PAD + LANE:(hh + 1) * HEAD_PAD] = (kr * r).astype(BF16)
        v_ref[:, hh * DV_C:(hh + 1) * DV_C] = res[:, LANE:].astype(BF16)


def _mla_kv(ckv, kpe_pad, tile0, rows, w_ukv, gk_pad, o, tabs):
    use_rope = tabs is not None
    in_specs = [
        pl.BlockSpec((TM, KV_LORA), lambda i, j: (tile0 + i, 0)),
        pl.BlockSpec((TM, LANE), lambda i, j: (tile0 + i, 0)),
        pl.BlockSpec((None, KV_LORA, MLA_HB * 2 * LANE), lambda i, j: (o, 0, j)),
        pl.BlockSpec((None, 1, HEAD_PAD), lambda i, j: (o, 0, 0)),
    ]
    args = [ckv, kpe_pad, w_ukv, gk_pad]
    if use_rope:
        in_specs += [pl.BlockSpec((TM, LANE), lambda i, j: (i % (DEC_SEQ // TM), 0))] * 3
        args += list(tabs)
    return pl.pallas_call(
        functools.partial(_mla_kv_kernel, use_rope=use_rope),
        out_shape=(jax.ShapeDtypeStruct((rows, H_C * HEAD_PAD), BF16),
                   jax.ShapeDtypeStruct((rows, H_C * DV_C), BF16)),
        grid=(rows // TM, H_C // MLA_HB),
        in_specs=in_specs,
        out_specs=(pl.BlockSpec((TM, MLA_HB * HEAD_PAD), lambda i, j: (i, j)),
                   pl.BlockSpec((TM, MLA_HB * DV_C), lambda i, j: (i, j))),
        compiler_params=_cparams(("parallel", "parallel")),
        name="mla_kv",
    )(*args)


def _mla_attn_kernel(q_ref, k_ref, v_ref, *rest, heads, has_ctx, sub):
    if has_ctx:
        ck_ref, cv_ref, o_ref = rest
    else:
        (o_ref,) = rest
    c = DQK_C ** -0.5 * LOG2E
    for hh in range(heads):
        ks = slice(hh * HEAD_PAD, (hh + 1) * HEAD_PAD)
        vs = slice(hh * DV_C, (hh + 1) * DV_C)
        for r in range(q_ref.shape[0] // sub):
            rs = slice(r * sub, (r + 1) * sub)
            q = q_ref[rs, ks]
            s = _dot_nt(q, k_ref[:, ks])
            s2 = _dot_nt(q, ck_ref[:, ks]) if has_ctx else None
            e, e2, inv = _softmax_exp(s, s2, c)
            o = _dot(e.astype(BF16), v_ref[:, vs])
            if has_ctx:
                o = o + _dot(e2.astype(BF16), cv_ref[:, vs])
            o_ref[rs, vs] = (o * inv).astype(BF16)


def _mla_attn_ctx(q, k, v):
    heads = 8
    return pl.pallas_call(
        functools.partial(_mla_attn_kernel, heads=heads, has_ctx=False, sub=SEQ),
        out_shape=jax.ShapeDtypeStruct((T_CTX, H_C * DV_C), BF16),
        grid=(BATCH, H_C // heads),
        in_specs=[
            pl.BlockSpec((SEQ, heads * HEAD_PAD), lambda b, j: (b, j)),
            pl.BlockSpec((SEQ, heads * HEAD_PAD), lambda b, j: (b, j)),
            pl.BlockSpec((SEQ, heads * DV_C), lambda b, j: (b, j)),
        ],
        out_specs=pl.BlockSpec((SEQ, heads * DV_C), lambda b, j: (b, j)),
        compiler_params=_cparams(("parallel", "parallel")),
        name="mla_attn_ctx",
    )(q, k, v)


def _mla_attn_smp(q, k, v, ck, cv):
    tq, sub = 1024, 256
    nq = DEC_SEQ // tq
    return pl.pallas_call(
        functools.partial(_mla_attn_kernel, heads=1, has_ctx=True, sub=sub),
        out_shape=jax.ShapeDtypeStruct((T_SMP, H_C * DV_C), BF16),
        grid=(DEC_BATCH, H_C, nq),
        in_specs=[
            pl.BlockSpec((tq, HEAD_PAD), lambda b, h, i: (b * nq + i, h)),
            pl.BlockSpec((DEC_SEQ, HEAD_PAD), lambda b, h, i: (b, h)),
            pl.BlockSpec((DEC_SEQ, DV_C), lambda b, h, i: (b, h)),
            pl.BlockSpec((PAST_LEN, HEAD_PAD), lambda b, h, i: (b, h)),
            pl.BlockSpec((PAST_LEN, DV_C), lambda b, h, i: (b, h)),
        ],
        out_specs=pl.BlockSpec((tq, DV_C), lambda b, h, i: (b * nq + i, h)),
        compiler_params=_cparams(("parallel", "parallel", "arbitrary")),
        name="mla_attn_smp",
    )(q, k, v, ck, cv)


def _axial_angles(rows, rot_dim):
    row = jnp.repeat(jnp.arange(rows, dtype=F32), GRID_W)
    col = jnp.tile(jnp.arange(GRID_W, dtype=F32), rows)
    half = rot_dim // 2
    inv_freq = ROPE_BASE ** (-jnp.arange(0, half, 2, dtype=F32) / half)
    ang_r = row[:, None] * inv_freq[None, :]
    ang_c = col[:, None] * inv_freq[None, :]
    return jnp.concatenate([ang_r, ang_r, ang_c, ang_c], axis=-1)


def _rope_tables(rows, rot_dim):
    ang = _axial_angles(rows, rot_dim)
    q = rot_dim // 4
    blk = (jnp.arange(rot_dim) // q) % 2
    cos, sin = jnp.cos(ang), jnp.sin(ang)
    sin_a = jnp.where(blk == 0, -sin, 0.0)
    sin_b = jnp.where(blk == 1, sin, 0.0)

    def full(t, fill):
        return jnp.pad(t, ((0, 0), (0, LANE - rot_dim)), constant_values=fill)

    return full(cos, 1.0), full(sin_a, 0.0), full(sin_b, 0.0)


def kernel(x_prompt, x_sample, c, cache_diff_k, cache_diff_v, cache_mla_ckv, cache_mla_kpe, c_ctx, w_mod, b_mod, g_norm, w_ffn_gate, w_ffn_up, w_ffn_down, w_in_ab, w_out_ab, g_qk_diff, diff_lambda, g_diff_sub, w_in_mla, g_q_lora, w_uq, g_kv_lora, w_ukv, g_qk_mla, w_o_mla):
    rows = DEC_SEQ // GRID_W
    tabs_a = _rope_tables(rows, DH_A)
    tabs_c = _rope_tables(rows, ROPE_C)
    dft_ch = _dft_channel_mats()
    dft_ctx = tuple(m.astype(BF16) for m in _dft_mats(SEQ))
    dft_smp = tuple(m.astype(BF16) for m in _dft_mats(DEC_SEQ))
    p_ctx, p_smp = _Pass(False), _Pass(True)

    w_ffn = (w_ffn_gate[0, 0].astype(BF16), w_ffn_up[0, 0].astype(BF16), w_ffn_down[0, 0].astype(BF16))
    w_in_ab_b, w_out_ab_b = w_in_ab.astype(BF16), w_out_ab.astype(BF16)
    w_in_mla_b = jnp.pad(w_in_mla.astype(BF16), ((0, 0), (0, 0), (0, LANE - ROPE_C)))
    w_uq_b = jnp.pad(w_uq.astype(BF16).reshape(N_ODD, Q_LORA, H_C, DQK_C),
                     ((0, 0), (0, 0), (0, 0), (0, HEAD_PAD - DQK_C))).reshape(N_ODD, Q_LORA, H_C * HEAD_PAD)
    w_ukv_b, w_o_b = w_ukv.astype(BF16), w_o_mla.astype(BF16)
    g_qk_mla_pad = jnp.pad(g_qk_mla, ((0, 0), (0, 0), (0, HEAD_PAD - DQK_C)))[:, :, None, :]
    gq_mla, gk_mla = g_qk_mla_pad[:, 0], g_qk_mla_pad[:, 1]
    g_norm4 = g_norm[:, :, None, :]
    gqk_diff = g_qk_diff[:, :, None, :]
    g_sub3 = g_diff_sub[:, None, :]
    g_q_lora3, g_kv_lora3 = g_q_lora[:, None, :], g_kv_lora[:, None, :]

    cond8 = jnp.concatenate([c_ctx[None, :], c, jnp.zeros((8 - N_COND, D_MODEL), F32)], axis=0)
    mod = _modulation(cond8, w_mod, b_mod)[:, :N_COND].reshape(DEPTH, N_COND, N_MOD, D_MODEL)

    x = jnp.concatenate([x_prompt.reshape(T_CTX, D_MODEL), x_sample.reshape(T_SMP, D_MODEL)], axis=0)
    new_kv = None
    mla_ckv, mla_kpe = [], []
    h = _prenorm(x, mod[0], g_norm4[0])

    for l in range(DEPTH):
        mod_l = mod[l]
        x, *w_ffn = _ffn(h, x, mod_l, w_ffn, 0, (w_ffn_gate, w_ffn_up, w_ffn_down, l, 1), None)
        if l % 2 == 0:
            e = l // 2
            lam_init = 0.8 - 0.6 * math.exp(-0.3 * l)
            q_c, k_c, v_c, f_c, nk, nv = _inproj_even(p_ctx, x, mod_l, g_norm4[l], w_in_ab_b, gqk_diff, e,
                                                     None, new_kv)
            new_kv = (nk, nv)
            q_s, k_s, v_s, f_s = _inproj_even(p_smp, x, mod_l, g_norm4[l], w_in_ab_b, gqk_diff, e, tabs_a, None)
            o_ctx = _diff_attn_ctx(q_c, k_c, v_c, diff_lambda, g_sub3, e, lam_init)
            o_smp = _diff_attn_smp(q_s, k_s, v_s, cache_diff_k, cache_diff_v, diff_lambda, g_sub3, e, lam_init)
            fo_ctx = _fourier(f_c, SEQ, dft_ch, dft_ctx)
            fo_smp = _fourier(f_s, DEC_SEQ, dft_ch, dft_smp)
            x, h = _outproj(x, mod_l, g_norm4[l],
                            [(o_ctx, o_smp, w_out_ab_b, e, 0), (fo_ctx, fo_smp, w_out_ab_b, e, W_A // W_B)])
        else:
            o = l // 2
            cqn, ckvn, kpe_pad = _mla_in(x, mod_l, g_norm4[l], w_in_mla_b, g_q_lora3, g_kv_lora3, o)
            q_c = _mla_q(p_ctx, cqn, w_uq_b, gq_mla, o, None)
            q_s = _mla_q(p_smp, cqn, w_uq_b, gq_mla, o, tabs_c)
            k_c, v_c = _mla_kv(ckvn, kpe_pad, p_ctx.tile0, T_CTX, w_ukv_b, gk_mla, o, None)
            k_s, v_s = _mla_kv(ckvn, kpe_pad, p_smp.tile0, T_SMP, w_ukv_b, gk_mla, o, tabs_c)
            cache_ckv = cache_mla_ckv[:, o].reshape(DEC_BATCH * PAST_LEN, KV_LORA)
            cache_kpe = jnp.pad(cache_mla_kpe[:, o].reshape(DEC_BATCH * PAST_LEN, ROPE_C),
                                ((0, 0), (0, LANE - ROPE_C)))
            ck, cv = _mla_kv(cache_ckv, cache_kpe, 0, DEC_BATCH * PAST_LEN, w_ukv_b, gk_mla, o, None)
            a_ctx = _mla_attn_ctx(q_c, k_c, v_c)
            a_smp = _mla_attn_smp(q_s, k_s, v_s, ck, cv)
            x, h = _outproj(x, mod_l, g_norm4[l], [(a_ctx, a_smp, w_o_b, o, 0)])
            mla_ckv.append(ckvn[:T_CTX].reshape(BATCH, SEQ, KV_LORA))
            mla_kpe.append(kpe_pad[:T_CTX, :ROPE_C].reshape(BATCH, SEQ, ROPE_C))
        if l + 1 < DEPTH:
            x, *w_ffn, h = _ffn(h, x, mod_l, w_ffn, 1, (w_ffn_gate, w_ffn_up, w_ffn_down, l + 1, 0),
                                (mod[l + 1], g_norm4[l + 1]))
        else:
            (x,) = _ffn(h, x, mod_l, w_ffn, 1, None, None)

    y_prompt = x[:T_CTX].reshape(BATCH, SEQ, D_MODEL)
    y_sample = x[T_CTX:].reshape(DEC_BATCH, DEC_SEQ, D_MODEL)
    return (y_prompt, y_sample, new_kv[0], new_kv[1], jnp.stack(mla_ckv, axis=1), jnp.stack(mla_kpe, axis=1))
```

```python
import functools
import math

import jax
import jax.numpy as jnp
from jax import lax
from jax.experimental import pallas as pl
from jax.experimental.pallas import tpu as pltpu

F32 = jnp.float32
BF16 = jnp.bfloat16

D_MODEL = 2048
BATCH = 32
SEQ = 256
DEPTH = 4
DEC_BATCH = 2
DEC_SEQ = 2048
PAST_LEN = 256
GRID_W = 64
N_EVEN = 2
N_ODD = 2
H_A = 6
DH_A = 128
W_A = H_A * 2 * DH_A
G_B = 4
DG_B = 128
W_B = G_B * DG_B
H_C = 16
Q_LORA = 1536
KV_LORA = 512
NOPE_C = 128
ROPE_C = 64
DQK_C = NOPE_C + ROPE_C
DV_C = 128
D_FF = 5632
N_MOD = 9
ROPE_BASE = 10000.0
EPS = 1e-6
LOG2E = 1.4426950408889634

T_CTX = BATCH * SEQ
T_SMP = DEC_BATCH * DEC_SEQ
T_ALL = T_CTX + T_SMP
N_COND = 1 + DEC_BATCH
LANE = 128
HEAD_PAD = 256
TM = 512

VMEM_LIMIT = 52 * 1024 * 1024
FFN_TM = 1024
FFN_VMEM_LIMIT = 60 * 1024 * 1024


def _cparams(sem):
    return pltpu.CompilerParams(dimension_semantics=sem, vmem_limit_bytes=VMEM_LIMIT)


def _cond_row(i, tm=TM):
    n_ctx = T_CTX // tm
    per = DEC_SEQ // tm
    return jnp.where(i < n_ctx, 0, 1 + (i - n_ctx) // per)


class _Pass:
    def __init__(self, latent):
        self.latent = latent
        self.rows = T_SMP if latent else T_CTX
        self.tile0 = (T_CTX // TM) if latent else 0
        self.tiles = self.rows // TM
        self.seq = DEC_SEQ if latent else SEQ
        self.batch = DEC_BATCH if latent else BATCH

    def cond(self, i):
        return 1 + i // (DEC_SEQ // TM) if self.latent else 0


def _dot(a, b):
    return jnp.dot(a, b, preferred_element_type=F32)


def _dot_nt(a, b):
    return lax.dot_general(a, b, (((1,), (1,)), ((), ())), preferred_element_type=F32)


def _silu(x):
    return x / (1.0 + jnp.exp(-x))


def _rms(x, denom):
    return lax.rsqrt(jnp.sum(x * x, axis=-1, keepdims=True) * (1.0 / denom) + EPS)


def _norm_mod(x, g, mod_ref, base):
    sh = mod_ref[base:base + 1, :]
    sc = mod_ref[base + 1:base + 2, :]
    return (x * _rms(x, D_MODEL) * g) * (1.0 + sc) + sh


def _rope(y, cos, sin_a, sin_b, q):
    return y * cos + pltpu.roll(y, LANE - q, 1) * sin_a + pltpu.roll(y, q, 1) * sin_b


def _softmax_exp(s, s2, c):
    m = jnp.max(s, axis=-1, keepdims=True)
    if s2 is not None:
        m = jnp.maximum(m, jnp.max(s2, axis=-1, keepdims=True))
    e = jnp.exp2((s - m) * c)
    l = jnp.sum(e, axis=-1, keepdims=True)
    e2 = None
    if s2 is not None:
        e2 = jnp.exp2((s2 - m) * c)
        l = l + jnp.sum(e2, axis=-1, keepdims=True)
    return e, e2, 1.0 / l


def _mod_kernel(c_ref, w_ref, b_ref, o_ref):
    a = _silu(c_ref[...]).astype(BF16)
    o_ref[...] = _dot(a, w_ref[...].astype(BF16)) + b_ref[...]


def _modulation(cond8, w_mod, b_mod):
    tn = 1024
    n = N_MOD * D_MODEL
    return pl.pallas_call(
        _mod_kernel,
        out_shape=jax.ShapeDtypeStruct((DEPTH, 8, n), F32),
        grid=(DEPTH, n // tn),
        in_specs=[
            pl.BlockSpec((8, D_MODEL), lambda l, j: (0, 0)),
            pl.BlockSpec((None, D_MODEL, tn), lambda l, j: (l, 0, j)),
            pl.BlockSpec((None, 1, tn), lambda l, j: (l, 0, j)),
        ],
        out_specs=pl.BlockSpec((None, 8, tn), lambda l, j: (l, 0, j)),
        compiler_params=_cparams(("parallel", "parallel")),
        name="modulation",
    )(cond8, w_mod, b_mod.reshape(DEPTH, 1, n))


FFN_TF = 512
FFN_XC = 256
FFN_DC = 512
BF16_ROWS = 16


def _prenorm_kernel(*refs, split):
    if split:
        xc_ref, xs_ref, mod_ref, g_ref, h_ref = refs
        x = jnp.where(pl.program_id(0) < T_CTX // TM, xc_ref[...], xs_ref[...])
    else:
        x_ref, mod_ref, g_ref, h_ref = refs
        x = x_ref[...]
    h_ref[...] = _norm_mod(x, g_ref[...], mod_ref, 0).astype(BF16)


def _prenorm(xs, mod_l, g):
    n_ctx = T_CTX // TM
    if len(xs) == 2:
        x_specs = [pl.BlockSpec((TM, D_MODEL), lambda i: (jnp.minimum(i, n_ctx - 1), 0)),
                   pl.BlockSpec((TM, D_MODEL), lambda i: (jnp.maximum(i - n_ctx, 0), 0))]
    else:
        x_specs = [pl.BlockSpec((TM, D_MODEL), lambda i: (i, 0))]
    return pl.pallas_call(
        functools.partial(_prenorm_kernel, split=len(xs) == 2),
        out_shape=jax.ShapeDtypeStruct((T_ALL, D_MODEL), BF16),
        grid=(T_ALL // TM,),
        in_specs=x_specs + [
            pl.BlockSpec((None, N_MOD, D_MODEL), lambda i: (_cond_row(i), 0, 0)),
            pl.BlockSpec((None, 1, D_MODEL), lambda i: (0, 0, 0)),
        ],
        out_specs=pl.BlockSpec((TM, D_MODEL), lambda i: (i, 0)),
        compiler_params=_cparams(("parallel",)),
        name="prenorm",
    )(*xs, mod_l, g)


def _ffn_kernel(*refs, base, cast_next, in_place):
    it = iter(refs)
    h_ref, x_ref, mod_ref, wg_ref, wu_ref, wd_ref = [next(it) for _ in range(6)]
    nxt = [next(it) for _ in range(3)] if cast_next else []
    if in_place:
        next(it)
    o_ref = next(it)
    cst = [next(it) for _ in range(3)] if cast_next else []
    f = pl.program_id(1)

    @pl.when(f == 0)
    def _():
        o_ref[...] = jnp.zeros_like(o_ref)

    h = h_ref[...]
    gate = _dot(h, wg_ref[...])
    up = _dot(h, wu_ref[...])
    a = (_silu(gate) * up).astype(BF16)
    for c in range(D_MODEL // FFN_DC):
        cs = slice(c * FFN_DC, (c + 1) * FFN_DC)
        o_ref[:, cs] += (0.5 * mod_ref[base + 2:base + 3, cs]) * _dot(a, wd_ref[:, cs])

    for c in range(D_MODEL // FFN_XC):
        @pl.when(f == c)
        def _(c=c):
            o_ref[:, c * FFN_XC:(c + 1) * FFN_XC] += x_ref[...]

    for src, dst in zip(nxt, cst):
        dst[...] = src[...].astype(BF16)


def _cast_rows(total, steps):
    rows = BF16_ROWS
    while total % rows or total // rows > steps:
        rows += BF16_ROWS
    return rows


def _ffn(h, x, x_tile0, tile0, tiles, mod_l, w_b, k, nxt, out):
    tm, tf = FFN_TM, FFN_TF
    nf = D_FF // tf
    assert nf >= D_MODEL // FFN_XC
    wg, wu, wd = w_b
    in_specs = [
        pl.BlockSpec((tm, D_MODEL), lambda i, f: (tile0 + i, 0)),
        pl.BlockSpec((tm, FFN_XC), lambda i, f: (x_tile0 + i, jnp.minimum(f, D_MODEL // FFN_XC - 1))),
        pl.BlockSpec((None, N_MOD, D_MODEL), lambda i, f: (_cond_row(tile0 + i, tm), 0, 0)),
        pl.BlockSpec((D_MODEL, tf), lambda i, f: (0, f)),
        pl.BlockSpec((D_MODEL, tf), lambda i, f: (0, f)),
        pl.BlockSpec((tf, D_MODEL), lambda i, f: (f, 0)),
    ]
    args = [h, x, mod_l, wg, wu, wd]
    if out == "own":
        out_shape = [jax.ShapeDtypeStruct((tiles * tm, D_MODEL), F32)]
        out_specs = [pl.BlockSpec((tm, D_MODEL), lambda i, f: (i, 0))]
    else:
        out_shape = [jax.ShapeDtypeStruct((T_ALL, D_MODEL), F32)]
        out_specs = [pl.BlockSpec((tm, D_MODEL), lambda i, f: (tile0 + i, 0))]
    if nxt is not None:
        ng, nu, nd, l2, k2 = nxt
        gu_rows, d_rows = _cast_rows(D_MODEL, tiles * nf), _cast_rows(D_FF, tiles * nf)
        n_gu, n_d = D_MODEL // gu_rows, D_FF // d_rows
        gu_in = pl.BlockSpec((None, None, gu_rows, D_FF),
                             lambda i, f: (l2, k2, jnp.minimum(i * nf + f, n_gu - 1), 0))
        gu_out = pl.BlockSpec((gu_rows, D_FF), lambda i, f: (jnp.minimum(i * nf + f, n_gu - 1), 0))
        in_specs += [gu_in, gu_in, pl.BlockSpec((None, None, d_rows, D_MODEL),
                                                lambda i, f: (l2, k2, jnp.minimum(i * nf + f, n_d - 1), 0))]
        args += [ng, nu, nd]
        out_shape += [jax.ShapeDtypeStruct((D_MODEL, D_FF), BF16)] * 2 + [jax.ShapeDtypeStruct((D_FF, D_MODEL), BF16)]
        out_specs += [gu_out, gu_out,
                      pl.BlockSpec((d_rows, D_MODEL), lambda i, f: (jnp.minimum(i * nf + f, n_d - 1), 0))]
    aliases = {}
    if isinstance(out, tuple):
        aliases = {len(args): 0}
        in_specs += [pl.BlockSpec(memory_space=pl.ANY)]
        args += [out[1]]
    return pl.pallas_call(
        functools.partial(_ffn_kernel, base=6 * k, cast_next=nxt is not None, in_place=bool(aliases)),
        out_shape=out_shape,
        grid=(tiles, nf),
        in_specs=in_specs,
        out_specs=out_specs,
        input_output_aliases=aliases,
        compiler_params=pltpu.CompilerParams(dimension_semantics=("arbitrary", "arbitrary"),
                                             vmem_limit_bytes=FFN_VMEM_LIMIT),
        name="ffn",
    )(*args)


QKV_TN = W_A // 2
F_TN = W_B // 2


def _inproj_even_kernel(*refs, latent):
    x_ref, mod_ref, g_ref, wq_ref, wk_ref, wv_ref, wf_ref, gq_ref, gk_ref = refs[:9]
    refs = refs[9:]
    if latent:
        cos_ref, sa_ref, sb_ref, q_ref, k_ref, v_ref, f_ref, h_ref = refs
    else:
        q_ref, k_ref, v_ref, f_ref, nk_ref, nv_ref, h_ref = refs[-7:]

    @pl.when(pl.program_id(1) == 0)
    def _():
        h_ref[...] = _norm_mod(x_ref[...], g_ref[...], mod_ref, 3).astype(BF16)

    h = h_ref[...]
    for w_ref, gn_ref, o_ref, is_k in ((wq_ref, gq_ref, q_ref, False), (wk_ref, gk_ref, k_ref, True)):
        res = _dot(h, w_ref[...])
        g = gn_ref[...]
        for ch in range(QKV_TN // DH_A):
            xc = res[:, ch * DH_A:(ch + 1) * DH_A]
            y = xc * _rms(xc, DH_A) * g
            if latent:
                y = _rope(y, cos_ref[...], sa_ref[...], sb_ref[...], DH_A // 4)
            o_ref[:, ch * DH_A:(ch + 1) * DH_A] = y.astype(BF16)
            if is_k and not latent:
                for b in range(TM // SEQ):
                    nk_ref[b, ch // 2, :, (ch % 2) * DH_A:(ch % 2 + 1) * DH_A] = y[b * SEQ:(b + 1) * SEQ, :]
    res = _dot(h, wv_ref[...])
    v_ref[...] = res.astype(BF16)
    if not latent:
        for b in range(TM // SEQ):
            for hh in range(QKV_TN // (2 * DH_A)):
                nv_ref[b, hh] = res[b * SEQ:(b + 1) * SEQ, hh * 2 * DH_A:(hh + 1) * 2 * DH_A]
    f_ref[...] = _dot(h, wf_ref[...]).astype(BF16)


def _inproj_even(ps, x, mod_l, g_norm_l, w_in, gqk, e, tabs, prev_kv):
    nj = W_A // QKV_TN
    hb = QKV_TN // (2 * DH_A)
    in_specs = [
        pl.BlockSpec((TM, D_MODEL), lambda i, j: (ps.tile0 + i, 0)),
        pl.BlockSpec((None, N_MOD, D_MODEL), lambda i, j: (ps.cond(i), 0, 0)),
        pl.BlockSpec((None, 1, D_MODEL), lambda i, j: (1, 0, 0)),
        pl.BlockSpec((None, D_MODEL, QKV_TN), lambda i, j: (e, 0, j)),
        pl.BlockSpec((None, D_MODEL, QKV_TN), lambda i, j: (e, 0, nj + j)),
        pl.BlockSpec((None, D_MODEL, QKV_TN), lambda i, j: (e, 0, 2 * nj + j)),
        pl.BlockSpec((None, D_MODEL, F_TN), lambda i, j: (e, 0, 3 * W_A // F_TN + j)),
        pl.BlockSpec((None, None, 1, DH_A), lambda i, j: (e, 0, 0, 0)),
        pl.BlockSpec((None, None, 1, DH_A), lambda i, j: (e, 1, 0, 0)),
    ]
    args = [x, mod_l, g_norm_l, w_in, w_in, w_in, w_in, gqk, gqk]
    out_shape = [jax.ShapeDtypeStruct((ps.rows, W_A), BF16)] * 3 + [jax.ShapeDtypeStruct((ps.rows, W_B), BF16)]
    out_specs = [pl.BlockSpec((TM, QKV_TN), lambda i, j: (i, j))] * 3 + [pl.BlockSpec((TM, F_TN), lambda i, j: (i, j))]
    aliases = {}
    if ps.latent:
        tab_spec = pl.BlockSpec((TM, LANE), lambda i, j: (i % (DEC_SEQ // TM), 0))
        in_specs += [tab_spec] * 3
        args += list(tabs)
    else:
        kv_shape = jax.ShapeDtypeStruct((BATCH, N_EVEN, H_A, SEQ, 2 * DH_A), F32)
        kv_spec = pl.BlockSpec((TM // SEQ, None, hb, SEQ, 2 * DH_A), lambda i, j: (i, e, j, 0, 0))
        out_shape += [kv_shape, kv_shape]
        out_specs += [kv_spec, kv_spec]
        if prev_kv is not None:
            aliases = {len(args): 4, len(args) + 1: 5}
            in_specs += [pl.BlockSpec(memory_space=pl.ANY)] * 2
            args += list(prev_kv)
    return pl.pallas_call(
        functools.partial(_inproj_even_kernel, latent=ps.latent),
        out_shape=out_shape,
        grid=(ps.tiles, nj),
        in_specs=in_specs,
        out_specs=out_specs,
        scratch_shapes=[pltpu.VMEM((TM, D_MODEL), BF16)],
        input_output_aliases=aliases,
        compiler_params=_cparams(("parallel", "arbitrary")),
        name="inproj_even",
    )(*args)


def _diff_attn_kernel(lam_ref, gsub_ref, q_ref, k_ref, v_ref, *rest, lam_init, heads, has_ctx, sub):
    if has_ctx:
        ck_ref, cv_ref, o_ref = rest
    else:
        (o_ref,) = rest
    lv = lam_ref[...]
    lam = (jnp.exp(jnp.sum(lv[0:1] * lv[1:2], axis=-1, keepdims=True))
           - jnp.exp(jnp.sum(lv[2:3] * lv[3:4], axis=-1, keepdims=True)) + lam_init)
    c = DH_A ** -0.5 * LOG2E
    dh2 = 2 * DH_A
    gsub = gsub_ref[...] * (1.0 - lam_init)
    for hh in range(heads):
        hs = slice(hh * dh2, (hh + 1) * dh2)
        k = k_ref[:, hs]
        v = v_ref[:, hs]
        if has_ctx:
            ck = ck_ref[hh].astype(BF16)
            cv = cv_ref[hh].astype(BF16)
        for r in range(q_ref.shape[0] // sub):
            rs = slice(r * sub, (r + 1) * sub)
            q = q_ref[rs, hs]
            a, a2 = None, None
            for m in range(2):
                sl = slice(m * DH_A, (m + 1) * DH_A)
                s = _dot_nt(q[:, sl], k[:, sl])
                s2 = _dot_nt(q[:, sl], ck[:, sl]) if has_ctx else None
                e, e2, inv = _softmax_exp(s, s2, c)
                w = inv if m == 0 else -(lam * inv)
                a = e * w if a is None else a + e * w
                if has_ctx:
                    a2 = e2 * w if a2 is None else a2 + e2 * w
            o = _dot(a.astype(BF16), v)
            if has_ctx:
                o = o + _dot(a2.astype(BF16), cv)
            o_ref[rs, hs] = (o * _rms(o, dh2) * gsub).astype(BF16)


def _diff_attn_ctx(q, k, v, lam_vecs, gsub, e, lam_init):
    spec = pl.BlockSpec((SEQ, W_A), lambda b: (b, 0))
    return pl.pallas_call(
        functools.partial(_diff_attn_kernel, lam_init=lam_init, heads=H_A, has_ctx=False, sub=SEQ),
        out_shape=jax.ShapeDtypeStruct((T_CTX, W_A), BF16),
        grid=(BATCH,),
        in_specs=[
            pl.BlockSpec((None, 4, DH_A), lambda b: (e, 0, 0)),
            pl.BlockSpec((None, 1, 2 * DH_A), lambda b: (e, 0, 0)),
            spec, spec, spec,
        ],
        out_specs=spec,
        compiler_params=_cparams(("parallel",)),
        name="diff_attn_ctx",
    )(lam_vecs, gsub, q, k, v)


def _diff_attn_smp(q, k, v, cache_k, cache_v, lam_vecs, gsub, e, lam_init):
    tq, sub = 1024, 256
    w = 2 * DH_A
    nq = DEC_SEQ // tq
    cache_spec = pl.BlockSpec((None, None, 1, PAST_LEN, w), lambda b, h, i: (b, e, h, 0, 0))
    return pl.pallas_call(
        functools.partial(_diff_attn_kernel, lam_init=lam_init, heads=1, has_ctx=True, sub=sub),
        out_shape=jax.ShapeDtypeStruct((T_SMP, W_A), BF16),
        grid=(DEC_BATCH, H_A, nq),
        in_specs=[
            pl.BlockSpec((None, 4, DH_A), lambda b, h, i: (e, 0, 0)),
            pl.BlockSpec((None, 1, w), lambda b, h, i: (e, 0, 0)),
            pl.BlockSpec((tq, w), lambda b, h, i: (b * nq + i, h)),
            pl.BlockSpec((DEC_SEQ, w), lambda b, h, i: (b, h)),
            pl.BlockSpec((DEC_SEQ, w), lambda b, h, i: (b, h)),
            cache_spec, cache_spec,
        ],
        out_specs=pl.BlockSpec((tq, w), lambda b, h, i: (b * nq + i, h)),
        compiler_params=_cparams(("parallel", "parallel", "arbitrary")),
        name="diff_attn_smp",
    )(lam_vecs, gsub, q, k, v, cache_k, cache_v)


def _fourier_kernel(x_ref, bc_ref, bs_ref, cs_ref, ss_ref, o_ref, y1_ref, y2_ref, *, scale):
    @pl.when(pl.program_id(1) == 0)
    def _():
        x = x_ref[...]
        y1_ref[...] = _dot(x, bc_ref[...]).astype(BF16)
        y2_ref[...] = _dot(x, bs_ref[...]).astype(BF16)

    out = _dot(cs_ref[...], y1_ref[...]) - _dot(ss_ref[...], y2_ref[...])
    o_ref[...] = (out * scale).astype(BF16)


def _fourier(f, s, dft_ch, dft_seq):
    tr = 256
    nb = f.shape[0] // s
    bc, bs = dft_ch
    cs, ss = dft_seq
    return pl.pallas_call(
        functools.partial(_fourier_kernel, scale=1.0 / math.sqrt(s * DG_B)),
        out_shape=jax.ShapeDtypeStruct(f.shape, BF16),
        grid=(nb, s // tr),
        in_specs=[
            pl.BlockSpec((s, W_B), lambda b, r: (b, 0)),
            pl.BlockSpec((W_B, W_B), lambda b, r: (0, 0)),
            pl.BlockSpec((W_B, W_B), lambda b, r: (0, 0)),
            pl.BlockSpec((tr, s), lambda b, r: (r, 0)),
            pl.BlockSpec((tr, s), lambda b, r: (r, 0)),
        ],
        out_specs=pl.BlockSpec((tr, W_B), lambda b, r: (b * (s // tr) + r, 0)),
        scratch_shapes=[pltpu.VMEM((s, W_B), BF16), pltpu.VMEM((s, W_B), BF16)],
        compiler_params=_cparams(("parallel", "arbitrary")),
        name="fourier",
    )(f, bc, bs, cs, ss)


def _dft_angles(rows, stride, n):
    j = jnp.arange(rows, dtype=jnp.int32)[:, None] * stride
    k = jnp.arange(n, dtype=jnp.int32)[None, :]
    return ((j * k) % n).astype(F32) * (2.0 * math.pi / n)


def _dft_mats(n):
    lo = min(n, 64)
    a0 = _dft_angles(lo, 1, n)
    c0, s0 = jnp.cos(a0), jnp.sin(a0)
    if lo == n:
        return c0, s0
    a1 = _dft_angles(n // lo, lo, n)
    c1, s1 = jnp.cos(a1)[:, None, :], jnp.sin(a1)[:, None, :]
    c = c1 * c0[None] - s1 * s0[None]
    s = s1 * c0[None] + c1 * s0[None]
    return c.reshape(n, n), s.reshape(n, n)


def _dft_channel_mats():
    c, s = _dft_mats(DG_B)
    eye = jnp.eye(G_B, dtype=F32)
    return jnp.kron(eye, c).astype(BF16), jnp.kron(eye, s).astype(BF16)


def _outproj_kernel(*refs, n_act):
    x_ref, mod_ref, g_ref = refs[:3]
    act = refs[3:3 + 3 * n_act]
    o_ref, hn_ref = refs[3 + 3 * n_act:]
    is_ctx = pl.program_id(0) < T_CTX // TM
    y = None
    for a in range(n_act):
        ctx_ref, smp_ref, w_ref = act[3 * a:3 * a + 3]
        lhs = jnp.where(is_ctx, ctx_ref[...], smp_ref[...])
        t = _dot(lhs, w_ref[...])
        y = t if y is None else y + t
    x_new = x_ref[...] + mod_ref[5:6, :] * y
    o_ref[...] = x_new
    hn_ref[...] = _norm_mod(x_new, g_ref[...], mod_ref, 6).astype(BF16)


def _outproj(x, mod_l, g_norm_l, acts):
    n_ctx = T_CTX // TM
    in_specs = [
        pl.BlockSpec((TM, D_MODEL), lambda i: (i, 0)),
        pl.BlockSpec((None, N_MOD, D_MODEL), lambda i: (_cond_row(i), 0, 0)),
        pl.BlockSpec((None, 1, D_MODEL), lambda i: (2, 0, 0)),
    ]
    args = [x, mod_l, g_norm_l]
    for ctx, smp, w, idx, rb in acts:
        kdim = ctx.shape[1]
        in_specs += [
            pl.BlockSpec((TM, kdim), lambda i: (jnp.minimum(i, n_ctx - 1), 0)),
            pl.BlockSpec((TM, kdim), lambda i: (jnp.maximum(i - n_ctx, 0), 0)),
            pl.BlockSpec((None, kdim, D_MODEL), lambda i, idx=idx, rb=rb: (idx, rb, 0)),
        ]
        args += [ctx, smp, w]
    row_spec = pl.BlockSpec((TM, D_MODEL), lambda i: (i, 0))
    return pl.pallas_call(
        functools.partial(_outproj_kernel, n_act=len(acts)),
        out_shape=(jax.ShapeDtypeStruct((T_ALL, D_MODEL), F32), jax.ShapeDtypeStruct((T_ALL, D_MODEL), BF16)),
        grid=(T_ALL // TM,),
        in_specs=in_specs,
        out_specs=(row_spec, row_spec),
        compiler_params=_cparams(("parallel",)),
        name="outproj",
    )(*args)


MLA_TM = 256


def _mla_proj_kernel(x_ref, mod_ref, g_ref, win_ref, wuq_ref, wukv_ref, gql_ref, gkvl_ref, gq_ref, gk_ref,
                     cos_ref, sa_ref, sb_ref, q_ref, k_ref, v_ref, ckv_ref, kpe_ref):
    h = _norm_mod(x_ref[...], g_ref[...], mod_ref, 3).astype(BF16)
    res = _dot(h, win_ref[...])
    cq = res[:, :Q_LORA]
    cqn = (cq * _rms(cq, Q_LORA) * gql_ref[...]).astype(BF16)
    ckv = res[:, Q_LORA:Q_LORA + KV_LORA]
    ckvn = ckv * _rms(ckv, KV_LORA) * gkvl_ref[...]
    ckv_ref[...] = ckvn
    kpe = res[:, Q_LORA + KV_LORA:]
    kpe_ref[...] = kpe
    cos, sa, sb = cos_ref[...], sa_ref[...], sb_ref[...]
    gq, gk = gq_ref[...], gk_ref[...]
    ss_pe = jnp.sum(kpe * kpe, axis=-1, keepdims=True)
    kr = _rope(kpe * gk[:, LANE:], cos, sa, sb, ROPE_C // 4)
    ckvb = ckvn.astype(BF16)
    for hh in range(H_C):
        hs = slice(hh * HEAD_PAD, (hh + 1) * HEAD_PAD)
        lo = slice(hh * HEAD_PAD, hh * HEAD_PAD + LANE)
        hi = slice(hh * HEAD_PAD + LANE, (hh + 1) * HEAD_PAD)
        xh = _dot(cqn, wuq_ref[:, hs])
        y = xh * _rms(xh, DQK_C) * gq
        q_ref[:, lo] = y[:, :LANE].astype(BF16)
        q_ref[:, hi] = _rope(y[:, LANE:], cos, sa, sb, ROPE_C // 4).astype(BF16)
        kv = _dot(ckvb, wukv_ref[:, hs])
        k_nope = kv[:, :LANE]
        ss = jnp.sum(k_nope * k_nope, axis=-1, keepdims=True) + ss_pe
        r = lax.rsqrt(ss * (1.0 / DQK_C) + EPS)
        k_ref[:, lo] = (k_nope * r * gk[:, :LANE]).astype(BF16)
        k_ref[:, hi] = (kr * r).astype(BF16)
        v_ref[:, hh * DV_C:(hh + 1) * DV_C] = kv[:, LANE:].astype(BF16)


def _mla_proj(x, mod_l, g_norm_l, w_in_pad, w_uq_pad, w_ukv, gq_lora, gkv_lora, gq_pad, gk_pad, o, tabs):
    tm = MLA_TM
    n_in = Q_LORA + KV_LORA + LANE
    resident = dict(pipeline_mode=pl.Buffered(1))
    tab_spec = pl.BlockSpec((tm, LANE), lambda i: (i, 0))
    row = lambda n: pl.BlockSpec((tm, n), lambda i: (i, 0))
    return pl.pallas_call(
        _mla_proj_kernel,
        out_shape=(jax.ShapeDtypeStruct((T_ALL, H_C * HEAD_PAD), BF16),
                   jax.ShapeDtypeStruct((T_ALL, H_C * HEAD_PAD), BF16),
                   jax.ShapeDtypeStruct((T_ALL, H_C * DV_C), BF16),
                   jax.ShapeDtypeStruct((T_ALL, KV_LORA), F32),
                   jax.ShapeDtypeStruct((T_ALL, LANE), F32)),
        grid=(T_ALL // tm,),
        in_specs=[
            row(D_MODEL),
            pl.BlockSpec((None, N_MOD, D_MODEL), lambda i: (_cond_row(i, tm), 0, 0)),
            pl.BlockSpec((None, 1, D_MODEL), lambda i: (1, 0, 0)),
            pl.BlockSpec((None, D_MODEL, n_in), lambda i: (o, 0, 0), **resident),
            pl.BlockSpec((None, Q_LORA, H_C * HEAD_PAD), lambda i: (o, 0, 0), **resident),
            pl.BlockSpec((None, KV_LORA, H_C * HEAD_PAD), lambda i: (o, 0, 0), **resident),
            pl.BlockSpec((None, 1, Q_LORA), lambda i: (o, 0, 0)),
            pl.BlockSpec((None, 1, KV_LORA), lambda i: (o, 0, 0)),
            pl.BlockSpec((None, 1, HEAD_PAD), lambda i: (o, 0, 0)),
            pl.BlockSpec((None, 1, HEAD_PAD), lambda i: (o, 0, 0)),
            tab_spec, tab_spec, tab_spec,
        ],
        out_specs=(row(H_C * HEAD_PAD), row(H_C * HEAD_PAD), row(H_C * DV_C), row(KV_LORA), row(LANE)),
        compiler_params=_cparams(("parallel",)),
        name="mla_proj",
    )(x, mod_l, g_norm_l, w_in_pad, w_uq_pad, w_ukv, gq_lora, gkv_lora, gq_pad, gk_pad, *tabs)


MLA_HB = 4


def _mla_kv_kernel(ckv_ref, kpe_ref, w_ref, g_ref, *rest, use_rope):
    if use_rope:
        cos_ref, sa_ref, sb_ref, k_ref, v_ref = rest
    else:
        k_ref, v_ref = rest
    ckv = ckv_ref[...].astype(BF16)
    g = g_ref[...]
    kpe = kpe_ref[...]
    ss_pe = jnp.sum(kpe * kpe, axis=-1, keepdims=True)
    kr = kpe * g[:, LANE:]
    if use_rope:
        kr = _rope(kr, cos_ref[...], sa_ref[...], sb_ref[...], ROPE_C // 4)
    for hh in range(MLA_HB):
        res = _dot(ckv, w_ref[:, hh * 2 * LANE:(hh + 1) * 2 * LANE])
        k_nope = res[:, :LANE]
        ss = jnp.sum(k_nope * k_nope, axis=-1, keepdims=True) + ss_pe
        r = lax.rsqrt(ss * (1.0 / DQK_C) + EPS)
        k_ref[:, hh * HEAD_PAD:hh * HEAD_PAD + LANE] = (k_nope * r * g[:, :LANE]).astype(BF16)
        k_ref[:, hh * HEAD_PAD + LANE:(hh + 1) * HEAD_PAD] = (kr * r).astype(BF16)
        v_ref[:, hh * DV_C:(hh + 1) * DV_C] = res[:, LANE:].astype(BF16)


def _mla_kv(ckv, kpe_pad, tile0, rows, w_ukv, gk_pad, o, tabs):
    use_rope = tabs is not None
    in_specs = [
        pl.BlockSpec((TM, KV_LORA), lambda i, j: (tile0 + i, 0)),
        pl.BlockSpec((TM, LANE), lambda i, j: (tile0 + i, 0)),
        pl.BlockSpec((None, KV_LORA, MLA_HB * 2 * LANE), lambda i, j: (o, 0, j)),
        pl.BlockSpec((None, 1, HEAD_PAD), lambda i, j: (o, 0, 0)),
    ]
    args = [ckv, kpe_pad, w_ukv, gk_pad]
    if use_rope:
        in_specs += [pl.BlockSpec((TM, LANE), lambda i, j: (i % (DEC_SEQ // TM), 0))] * 3
        args += list(tabs)
    return pl.pallas_call(
        functools.partial(_mla_kv_kernel, use_rope=use_rope),
        out_shape=(jax.ShapeDtypeStruct((rows, H_C * HEAD_PAD), BF16),
                   jax.ShapeDtypeStruct((rows, H_C * DV_C), BF16)),
        grid=(rows // TM, H_C // MLA_HB),
        in_specs=in_specs,
        out_specs=(pl.BlockSpec((TM, MLA_HB * HEAD_PAD), lambda i, j: (i, j)),
                   pl.BlockSpec((TM, MLA_HB * DV_C), lambda i, j: (i, j))),
        compiler_params=_cparams(("parallel", "parallel")),
        name="mla_kv",
    )(*args)


def _mla_attn_kernel(q_ref, k_ref, v_ref, *rest, heads, has_ctx, sub):
    if has_ctx:
        ck_ref, cv_ref, o_ref = rest
    else:
        (o_ref,) = rest
    c = DQK_C ** -0.5 * LOG2E
    for hh in range(heads):
        ks = slice(hh * HEAD_PAD, (hh + 1) * HEAD_PAD)
        vs = slice(hh * DV_C, (hh + 1) * DV_C)
        for r in range(q_ref.shape[0] // sub):
            rs = slice(r * sub, (r + 1) * sub)
            q = q_ref[rs, ks]
            s = _dot_nt(q, k_ref[:, ks])
            s2 = _dot_nt(q, ck_ref[:, ks]) if has_ctx else None
            e, e2, inv = _softmax_exp(s, s2, c)
            o = _dot(e.astype(BF16), v_ref[:, vs])
            if has_ctx:
                o = o + _dot(e2.astype(BF16), cv_ref[:, vs])
            o_ref[rs, vs] = (o * inv).astype(BF16)


def _mla_attn_ctx(q, k, v):
    heads = 8
    return pl.pallas_call(
        functools.partial(_mla_attn_kernel, heads=heads, has_ctx=False, sub=SEQ),
        out_shape=jax.ShapeDtypeStruct((T_CTX, H_C * DV_C), BF16),
        grid=(BATCH, H_C // heads),
        in_specs=[
            pl.BlockSpec((SEQ, heads * HEAD_PAD), lambda b, j: (b, j)),
            pl.BlockSpec((SEQ, heads * HEAD_PAD), lambda b, j: (b, j)),
            pl.BlockSpec((SEQ, heads * DV_C), lambda b, j: (b, j)),
        ],
        out_specs=pl.BlockSpec((SEQ, heads * DV_C), lambda b, j: (b, j)),
        compiler_params=_cparams(("parallel", "parallel")),
        name="mla_attn_ctx",
    )(q, k, v)


def _mla_attn_smp(q, k, v, ck, cv):
    tq, sub = 1024, 256
    nq = DEC_SEQ // tq
    q0, s0 = T_CTX // tq, T_CTX // DEC_SEQ
    return pl.pallas_call(
        functools.partial(_mla_attn_kernel, heads=1, has_ctx=True, sub=sub),
        out_shape=jax.ShapeDtypeStruct((T_SMP, H_C * DV_C), BF16),
        grid=(DEC_BATCH, H_C, nq),
        in_specs=[
            pl.BlockSpec((tq, HEAD_PAD), lambda b, h, i: (q0 + b * nq + i, h)),
            pl.BlockSpec((DEC_SEQ, HEAD_PAD), lambda b, h, i: (s0 + b, h)),
            pl.BlockSpec((DEC_SEQ, DV_C), lambda b, h, i: (s0 + b, h)),
            pl.BlockSpec((PAST_LEN, HEAD_PAD), lambda b, h, i: (b, h)),
            pl.BlockSpec((PAST_LEN, DV_C), lambda b, h, i: (b, h)),
        ],
        out_specs=pl.BlockSpec((tq, DV_C), lambda b, h, i: (b * nq + i, h)),
        compiler_params=_cparams(("parallel", "parallel", "arbitrary")),
        name="mla_attn_smp",
    )(q, k, v, ck, cv)


def _axial_angles(rows, rot_dim):
    row = jnp.repeat(jnp.arange(rows, dtype=F32), GRID_W)
    col = jnp.tile(jnp.arange(GRID_W, dtype=F32), rows)
    half = rot_dim // 2
    inv_freq = ROPE_BASE ** (-jnp.arange(0, half, 2, dtype=F32) / half)
    ang_r = row[:, None] * inv_freq[None, :]
    ang_c = col[:, None] * inv_freq[None, :]
    return jnp.concatenate([ang_r, ang_r, ang_c, ang_c], axis=-1)


def _rope_tables(rows, rot_dim):
    ang = _axial_angles(rows, rot_dim)
    q = rot_dim // 4
    blk = (jnp.arange(rot_dim) // q) % 2
    cos, sin = jnp.cos(ang), jnp.sin(ang)
    sin_a = jnp.where(blk == 0, -sin, 0.0)
    sin_b = jnp.where(blk == 1, sin, 0.0)

    def full(t, fill):
        return jnp.pad(t, ((0, 0), (0, LANE - rot_dim)), constant_values=fill)

    return full(cos, 1.0), full(sin_a, 0.0), full(sin_b, 0.0)


def _joint_tables(tabs):
    def joint(t, fill):
        return jnp.concatenate([jnp.full((T_CTX, LANE), fill, F32), jnp.tile(t, (DEC_BATCH, 1))], axis=0)
    return joint(tabs[0], 1.0), joint(tabs[1], 0.0), joint(tabs[2], 0.0)


def kernel(x_prompt, x_sample, c, cache_diff_k, cache_diff_v, cache_mla_ckv, cache_mla_kpe, c_ctx, w_mod, b_mod, g_norm, w_ffn_gate, w_ffn_up, w_ffn_down, w_in_ab, w_out_ab, g_qk_diff, diff_lambda, g_diff_sub, w_in_mla, g_q_lora, w_uq, g_kv_lora, w_ukv, g_qk_mla, w_o_mla):
    rows = DEC_SEQ // GRID_W
    tabs_a = _rope_tables(rows, DH_A)
    tabs_c_joint = _joint_tables(_rope_tables(rows, ROPE_C))
    dft_ch = _dft_channel_mats()
    dft_ctx = tuple(m.astype(BF16) for m in _dft_mats(SEQ))
    dft_smp = tuple(m.astype(BF16) for m in _dft_mats(DEC_SEQ))
    p_ctx, p_smp = _Pass(False), _Pass(True)

    w_ffn = (w_ffn_gate[0, 0].astype(BF16), w_ffn_up[0, 0].astype(BF16), w_ffn_down[0, 0].astype(BF16))
    w_in_ab_b, w_out_ab_b = w_in_ab.astype(BF16), w_out_ab.astype(BF16)
    w_in_mla_b = jnp.pad(w_in_mla.astype(BF16), ((0, 0), (0, 0), (0, LANE - ROPE_C)))
    w_uq_b = jnp.pad(w_uq.astype(BF16).reshape(N_ODD, Q_LORA, H_C, DQK_C),
                     ((0, 0), (0, 0), (0, 0), (0, HEAD_PAD - DQK_C))).reshape(N_ODD, Q_LORA, H_C * HEAD_PAD)
    w_ukv_b, w_o_b = w_ukv.astype(BF16), w_o_mla.astype(BF16)
    g_qk_mla_pad = jnp.pad(g_qk_mla, ((0, 0), (0, 0), (0, HEAD_PAD - DQK_C)))[:, :, None, :]
    gq_mla, gk_mla = g_qk_mla_pad[:, 0], g_qk_mla_pad[:, 1]
    g_norm4 = g_norm[:, :, None, :]
    gqk_diff = g_qk_diff[:, :, None, :]
    g_sub3 = g_diff_sub[:, None, :]
    g_q_lora3, g_kv_lora3 = g_q_lora[:, None, :], g_kv_lora[:, None, :]

    cond8 = jnp.concatenate([c_ctx[None, :], c, jnp.zeros((8 - N_COND, D_MODEL), F32)], axis=0)
    mod = _modulation(cond8, w_mod, b_mod)[:, :N_COND].reshape(DEPTH, N_COND, N_MOD, D_MODEL)

    x_ctx, x_smp = x_prompt.reshape(T_CTX, D_MODEL), x_sample.reshape(T_SMP, D_MODEL)
    n_ctx, n_smp, n_all = T_CTX // FFN_TM, T_SMP // FFN_TM, T_ALL // FFN_TM
    w_stacks = (w_ffn_gate, w_ffn_up, w_ffn_down)
    new_kv = None
    mla_ckv, mla_kpe = [], []

    for l in range(DEPTH):
        mod_l = mod[l]
        if l == 0:
            h = _prenorm((x_ctx, x_smp), mod_l, g_norm4[l])
            x, *w_next = _ffn(h, x_ctx, 0, 0, n_ctx, mod_l, w_ffn, 0, (*w_stacks, l, 1), "joint")
            (x,) = _ffn(h, x_smp, 0, n_ctx, n_smp, mod_l, w_ffn, 0, None, ("joint", x))
            w_ffn = w_next
        else:
            h = _prenorm((x,), mod_l, g_norm4[l])
            x, *w_ffn = _ffn(h, x, 0, 0, n_all, mod_l, w_ffn, 0, (*w_stacks, l, 1), "joint")
        if l % 2 == 0:
            e = l // 2
            lam_init = 0.8 - 0.6 * math.exp(-0.3 * l)
            q_c, k_c, v_c, f_c, nk, nv = _inproj_even(p_ctx, x, mod_l, g_norm4[l], w_in_ab_b, gqk_diff, e,
                                                     None, new_kv)
            new_kv = (nk, nv)
            q_s, k_s, v_s, f_s = _inproj_even(p_smp, x, mod_l, g_norm4[l], w_in_ab_b, gqk_diff, e, tabs_a, None)
            o_ctx = _diff_attn_ctx(q_c, k_c, v_c, diff_lambda, g_sub3, e, lam_init)
            o_smp = _diff_attn_smp(q_s, k_s, v_s, cache_diff_k, cache_diff_v, diff_lambda, g_sub3, e, lam_init)
            fo_ctx = _fourier(f_c, SEQ, dft_ch, dft_ctx)
            fo_smp = _fourier(f_s, DEC_SEQ, dft_ch, dft_smp)
            x, h = _outproj(x, mod_l, g_norm4[l],
                            [(o_ctx, o_smp, w_out_ab_b, e, 0), (fo_ctx, fo_smp, w_out_ab_b, e, W_A // W_B)])
        else:
            o = l // 2
            q, k, v, ckvn, kpe_pad = _mla_proj(x, mod_l, g_norm4[l], w_in_mla_b, w_uq_b, w_ukv_b, g_q_lora3,
                                               g_kv_lora3, gq_mla, gk_mla, o, tabs_c_joint)
            cache_ckv = cache_mla_ckv[:, o].reshape(DEC_BATCH * PAST_LEN, KV_LORA)
            cache_kpe = jnp.pad(cache_mla_kpe[:, o].reshape(DEC_BATCH * PAST_LEN, ROPE_C),
                                ((0, 0), (0, LANE - ROPE_C)))
            ck, cv = _mla_kv(cache_ckv, cache_kpe, 0, DEC_BATCH * PAST_LEN, w_ukv_b, gk_mla, o, None)
            a_ctx = _mla_attn_ctx(q, k, v)
            a_smp = _mla_attn_smp(q, k, v, ck, cv)
            x, h = _outproj(x, mod_l, g_norm4[l], [(a_ctx, a_smp, w_o_b, o, 0)])
            mla_ckv.append(ckvn[:T_CTX].reshape(BATCH, SEQ, KV_LORA))
            mla_kpe.append(kpe_pad[:T_CTX, :ROPE_C].reshape(BATCH, SEQ, ROPE_C))
        if l + 1 < DEPTH:
            x, *w_ffn = _ffn(h, x, 0, 0, n_all, mod_l, w_ffn, 1, (*w_stacks, l + 1, 0), "joint")
        else:
            (y_ctx,) = _ffn(h, x, 0, 0, n_ctx, mod_l, w_ffn, 1, None, "own")
            (y_smp,) = _ffn(h, x, n_ctx, n_ctx, n_smp, mod_l, w_ffn, 1, None, "own")

    y_prompt = y_ctx.reshape(BATCH, SEQ, D_MODEL)
    y_sample = y_smp.reshape(DEC_BATCH, DEC_SEQ, D_MODEL)
    return (y_prompt, y_sample, new_kv[0], new_kv[1], jnp.stack(mla_ckv, axis=1), jnp.stack(mla_kpe, axis=1))
```

```python
import functools
import math

import jax
import jax.numpy as jnp
from jax import lax
from jax.experimental import pallas as pl
from jax.experimental.pallas import tpu as pltpu

F32 = jnp.float32
BF16 = jnp.bfloat16

D_MODEL = 2048
BATCH = 32
SEQ = 256
DEPTH = 4
DEC_BATCH = 2
DEC_SEQ = 2048
PAST_LEN = 256
GRID_W = 64
N_EVEN = 2
N_ODD = 2
H_A = 6
DH_A = 128
W_A = H_A * 2 * DH_A
G_B = 4
DG_B = 128
W_B = G_B * DG_B
H_C = 16
Q_LORA = 1536
KV_LORA = 512
NOPE_C = 128
ROPE_C = 64
DQK_C = NOPE_C + ROPE_C
DV_C = 128
D_FF = 5632
N_MOD = 9
ROPE_BASE = 10000.0
EPS = 1e-6
LOG2E = 1.4426950408889634

T_CTX = BATCH * SEQ
T_SMP = DEC_BATCH * DEC_SEQ
T_ALL = T_CTX + T_SMP
N_COND = 1 + DEC_BATCH
LANE = 128
HEAD_PAD = 256
TM = 512

VMEM_LIMIT = 52 * 1024 * 1024
FFN_TM = 1024
FFN_VMEM_LIMIT = 60 * 1024 * 1024


def _cparams(sem):
    return pltpu.CompilerParams(dimension_semantics=sem, vmem_limit_bytes=VMEM_LIMIT)


def _cond_row(i, tm=TM):
    n_ctx = T_CTX // tm
    per = DEC_SEQ // tm
    return jnp.where(i < n_ctx, 0, 1 + (i - n_ctx) // per)


class _Pass:
    def __init__(self, latent):
        self.latent = latent
        self.rows = T_SMP if latent else T_CTX
        self.tile0 = (T_CTX // TM) if latent else 0
        self.tiles = self.rows // TM
        self.seq = DEC_SEQ if latent else SEQ
        self.batch = DEC_BATCH if latent else BATCH

    def cond(self, i):
        return 1 + i // (DEC_SEQ // TM) if self.latent else 0


def _dot(a, b):
    return jnp.dot(a, b, preferred_element_type=F32)


def _dot_nt(a, b):
    return lax.dot_general(a, b, (((1,), (1,)), ((), ())), preferred_element_type=F32)


def _silu(x):
    return x / (1.0 + jnp.exp(-x))


def _rms(x, denom):
    return lax.rsqrt(jnp.sum(x * x, axis=-1, keepdims=True) * (1.0 / denom) + EPS)


def _norm_mod(x, g, mod_ref, base):
    sh = mod_ref[base:base + 1, :]
    sc = mod_ref[base + 1:base + 2, :]
    return (x * _rms(x, D_MODEL) * g) * (1.0 + sc) + sh


def _rope(y, cos, sin_a, sin_b, q):
    return y * cos + pltpu.roll(y, LANE - q, 1) * sin_a + pltpu.roll(y, q, 1) * sin_b


def _softmax_exp(s, s2, need_sum):
    m = jnp.max(s, axis=-1, keepdims=True)
    if s2 is not None:
        m = jnp.maximum(m, jnp.max(s2, axis=-1, keepdims=True))
    e = jnp.exp2(s - m)
    e2 = jnp.exp2(s2 - m) if s2 is not None else None
    if not need_sum:
        return e, e2, None
    l = jnp.sum(e, axis=-1, keepdims=True)
    if s2 is not None:
        l = l + jnp.sum(e2, axis=-1, keepdims=True)
    return e, e2, 1.0 / l


def _mod_kernel(c_ref, w_ref, b_ref, o_ref):
    a = _silu(c_ref[...]).astype(BF16)
    o_ref[...] = _dot(a, w_ref[...].astype(BF16)) + b_ref[...]


def _modulation(cond8, w_mod, b_mod):
    tn = 1024
    n = N_MOD * D_MODEL
    return pl.pallas_call(
        _mod_kernel,
        out_shape=jax.ShapeDtypeStruct((DEPTH, 8, n), F32),
        grid=(DEPTH, n // tn),
        in_specs=[
            pl.BlockSpec((8, D_MODEL), lambda l, j: (0, 0)),
            pl.BlockSpec((None, D_MODEL, tn), lambda l, j: (l, 0, j)),
            pl.BlockSpec((None, 1, tn), lambda l, j: (l, 0, j)),
        ],
        out_specs=pl.BlockSpec((None, 8, tn), lambda l, j: (l, 0, j)),
        compiler_params=_cparams(("parallel", "parallel")),
        name="modulation",
    )(cond8, w_mod, b_mod.reshape(DEPTH, 1, n))


FFN_TF = 512
FFN_XC = 256
FFN_DC = 512
BF16_ROWS = 16


def _prenorm_kernel(*refs, split):
    if split:
        xc_ref, xs_ref, mod_ref, g_ref, h_ref = refs
        x = jnp.where(pl.program_id(0) < T_CTX // TM, xc_ref[...], xs_ref[...])
    else:
        x_ref, mod_ref, g_ref, h_ref = refs
        x = x_ref[...]
    h_ref[...] = _norm_mod(x, g_ref[...], mod_ref, 0).astype(BF16)


def _prenorm(xs, mod_l, g):
    n_ctx = T_CTX // TM
    if len(xs) == 2:
        x_specs = [pl.BlockSpec((TM, D_MODEL), lambda i: (jnp.minimum(i, n_ctx - 1), 0)),
                   pl.BlockSpec((TM, D_MODEL), lambda i: (jnp.maximum(i - n_ctx, 0), 0))]
    else:
        x_specs = [pl.BlockSpec((TM, D_MODEL), lambda i: (i, 0))]
    return pl.pallas_call(
        functools.partial(_prenorm_kernel, split=len(xs) == 2),
        out_shape=jax.ShapeDtypeStruct((T_ALL, D_MODEL), BF16),
        grid=(T_ALL // TM,),
        in_specs=x_specs + [
            pl.BlockSpec((None, N_MOD, D_MODEL), lambda i: (_cond_row(i), 0, 0)),
            pl.BlockSpec((None, 1, D_MODEL), lambda i: (0, 0, 0)),
        ],
        out_specs=pl.BlockSpec((TM, D_MODEL), lambda i: (i, 0)),
        compiler_params=_cparams(("parallel",)),
        name="prenorm",
    )(*xs, mod_l, g)


def _ffn_kernel(*refs, base, cast_next, in_place):
    it = iter(refs)
    h_ref, x_ref, mod_ref, wg_ref, wu_ref, wd_ref = [next(it) for _ in range(6)]
    nxt = [next(it) for _ in range(3)] if cast_next else []
    if in_place:
        next(it)
    o_ref = next(it)
    cst = [next(it) for _ in range(3)] if cast_next else []
    f = pl.program_id(1)

    @pl.when(f == 0)
    def _():
        o_ref[...] = jnp.zeros_like(o_ref)

    h = h_ref[...]
    gate = _dot(h, wg_ref[...])
    up = _dot(h, wu_ref[...])
    a = (_silu(gate) * up).astype(BF16)
    for c in range(D_MODEL // FFN_DC):
        cs = slice(c * FFN_DC, (c + 1) * FFN_DC)
        o_ref[:, cs] += (0.5 * mod_ref[base + 2:base + 3, cs]) * _dot(a, wd_ref[:, cs])

    for c in range(D_MODEL // FFN_XC):
        @pl.when(f == c)
        def _(c=c):
            o_ref[:, c * FFN_XC:(c + 1) * FFN_XC] += x_ref[...]

    for src, dst in zip(nxt, cst):
        dst[...] = src[...].astype(BF16)


def _cast_rows(total, steps):
    rows = BF16_ROWS
    while total % rows or total // rows > steps:
        rows += BF16_ROWS
    return rows


def _ffn(h, x, x_tile0, tile0, tiles, mod_l, w_b, k, nxt, out):
    tm, tf = FFN_TM, FFN_TF
    nf = D_FF // tf
    assert nf >= D_MODEL // FFN_XC
    wg, wu, wd = w_b
    in_specs = [
        pl.BlockSpec((tm, D_MODEL), lambda i, f: (tile0 + i, 0)),
        pl.BlockSpec((tm, FFN_XC), lambda i, f: (x_tile0 + i, jnp.minimum(f, D_MODEL // FFN_XC - 1))),
        pl.BlockSpec((None, N_MOD, D_MODEL), lambda i, f: (_cond_row(tile0 + i, tm), 0, 0)),
        pl.BlockSpec((D_MODEL, tf), lambda i, f: (0, f)),
        pl.BlockSpec((D_MODEL, tf), lambda i, f: (0, f)),
        pl.BlockSpec((tf, D_MODEL), lambda i, f: (f, 0)),
    ]
    args = [h, x, mod_l, wg, wu, wd]
    if out == "own":
        out_shape = [jax.ShapeDtypeStruct((tiles * tm, D_MODEL), F32)]
        out_specs = [pl.BlockSpec((tm, D_MODEL), lambda i, f: (i, 0))]
    else:
        out_shape = [jax.ShapeDtypeStruct((T_ALL, D_MODEL), F32)]
        out_specs = [pl.BlockSpec((tm, D_MODEL), lambda i, f: (tile0 + i, 0))]
    if nxt is not None:
        ng, nu, nd, l2, k2 = nxt
        gu_rows, d_rows = _cast_rows(D_MODEL, tiles * nf), _cast_rows(D_FF, tiles * nf)
        n_gu, n_d = D_MODEL // gu_rows, D_FF // d_rows
        gu_in = pl.BlockSpec((None, None, gu_rows, D_FF),
                             lambda i, f: (l2, k2, jnp.minimum(i * nf + f, n_gu - 1), 0))
        gu_out = pl.BlockSpec((gu_rows, D_FF), lambda i, f: (jnp.minimum(i * nf + f, n_gu - 1), 0))
        in_specs += [gu_in, gu_in, pl.BlockSpec((None, None, d_rows, D_MODEL),
                                                lambda i, f: (l2, k2, jnp.minimum(i * nf + f, n_d - 1), 0))]
        args += [ng, nu, nd]
        out_shape += [jax.ShapeDtypeStruct((D_MODEL, D_FF), BF16)] * 2 + [jax.ShapeDtypeStruct((D_FF, D_MODEL), BF16)]
        out_specs += [gu_out, gu_out,
                      pl.BlockSpec((d_rows, D_MODEL), lambda i, f: (jnp.minimum(i * nf + f, n_d - 1), 0))]
    aliases = {}
    if isinstance(out, tuple):
        aliases = {len(args): 0}
        in_specs += [pl.BlockSpec(memory_space=pl.ANY)]
        args += [out[1]]
    return pl.pallas_call(
        functools.partial(_ffn_kernel, base=6 * k, cast_next=nxt is not None, in_place=bool(aliases)),
        out_shape=out_shape,
        grid=(tiles, nf),
        in_specs=in_specs,
        out_specs=out_specs,
        input_output_aliases=aliases,
        compiler_params=pltpu.CompilerParams(dimension_semantics=("arbitrary", "arbitrary"),
                                             vmem_limit_bytes=FFN_VMEM_LIMIT),
        name="ffn",
    )(*args)


IN_TM = SEQ


def _inproj_even_kernel(*refs, latent):
    x_ref, mod_ref, g_ref, w_ref, gq_ref, gk_ref = refs[:6]
    refs = refs[6:]
    if latent:
        cos_ref, sa_ref, sb_ref, q_ref, k_ref, v_ref, f_ref = refs
    else:
        q_ref, k_ref, v_ref, f_ref, nk_ref, nv_ref = refs[-6:]
    dh2 = 2 * DH_A
    h = _norm_mod(x_ref[...], g_ref[...], mod_ref, 3).astype(BF16)
    gains = (gq_ref[...] * (DH_A ** -0.5 * LOG2E), gk_ref[...])
    for part, (g, o_ref) in enumerate(zip(gains, (q_ref, k_ref))):
        for hh in range(H_A):
            res = _dot(h, w_ref[:, part * W_A + hh * dh2:part * W_A + (hh + 1) * dh2])
            for m in range(2):
                xc = res[:, m * DH_A:(m + 1) * DH_A]
                y = xc * _rms(xc, DH_A) * g
                if latent:
                    y = _rope(y, cos_ref[...], sa_ref[...], sb_ref[...], DH_A // 4)
                o_ref[:, hh * dh2 + m * DH_A:hh * dh2 + (m + 1) * DH_A] = y.astype(BF16)
                if part == 1 and not latent:
                    nk_ref[hh, :, m * DH_A:(m + 1) * DH_A] = y
    for hh in range(H_A):
        res = _dot(h, w_ref[:, 2 * W_A + hh * dh2:2 * W_A + (hh + 1) * dh2])
        v_ref[:, hh * dh2:(hh + 1) * dh2] = res.astype(BF16)
        if not latent:
            nv_ref[hh] = res
    f_ref[...] = _dot(h, w_ref[:, 3 * W_A:]).astype(BF16)


def _inproj_even(ps, x, mod_l, g_norm_l, w_in, gqk, e, tabs, prev_kv):
    tm = IN_TM
    t0 = ps.tile0 * (TM // tm)
    in_specs = [
        pl.BlockSpec((tm, D_MODEL), lambda i: (t0 + i, 0)),
        pl.BlockSpec((None, N_MOD, D_MODEL), lambda i: (_cond_row(t0 + i, tm), 0, 0)),
        pl.BlockSpec((None, 1, D_MODEL), lambda i: (1, 0, 0)),
        pl.BlockSpec((None, D_MODEL, 3 * W_A + W_B), lambda i: (e, 0, 0), pipeline_mode=pl.Buffered(1)),
        pl.BlockSpec((None, None, 1, DH_A), lambda i: (e, 0, 0, 0)),
        pl.BlockSpec((None, None, 1, DH_A), lambda i: (e, 1, 0, 0)),
    ]
    args = [x, mod_l, g_norm_l, w_in, gqk, gqk]
    out_shape = [jax.ShapeDtypeStruct((ps.rows, W_A), BF16)] * 3 + [jax.ShapeDtypeStruct((ps.rows, W_B), BF16)]
    out_specs = [pl.BlockSpec((tm, W_A), lambda i: (i, 0))] * 3 + [pl.BlockSpec((tm, W_B), lambda i: (i, 0))]
    aliases = {}
    if ps.latent:
        tab_spec = pl.BlockSpec((tm, LANE), lambda i: (i % (DEC_SEQ // tm), 0))
        in_specs += [tab_spec] * 3
        args += list(tabs)
    else:
        kv_shape = jax.ShapeDtypeStruct((BATCH, N_EVEN, H_A, SEQ, 2 * DH_A), F32)
        kv_spec = pl.BlockSpec((None, None, H_A, SEQ, 2 * DH_A), lambda i: (i, e, 0, 0, 0))
        out_shape += [kv_shape, kv_shape]
        out_specs += [kv_spec, kv_spec]
        if prev_kv is not None:
            aliases = {len(args): 4, len(args) + 1: 5}
            in_specs += [pl.BlockSpec(memory_space=pl.ANY)] * 2
            args += list(prev_kv)
    return pl.pallas_call(
        functools.partial(_inproj_even_kernel, latent=ps.latent),
        out_shape=out_shape,
        grid=(ps.rows // tm,),
        in_specs=in_specs,
        out_specs=out_specs,
        input_output_aliases=aliases,
        compiler_params=_cparams(("parallel",)),
        name="inproj_even",
    )(*args)


def _diff_attn_kernel(lam_ref, gsub_ref, q_ref, k_ref, v_ref, *rest, lam_init, heads, has_ctx, sub):
    if has_ctx:
        ck_ref, cv_ref, o_ref = rest
    else:
        (o_ref,) = rest
    lv = lam_ref[...]
    lam = (jnp.exp(jnp.sum(lv[0:1] * lv[1:2], axis=-1, keepdims=True))
           - jnp.exp(jnp.sum(lv[2:3] * lv[3:4], axis=-1, keepdims=True)) + lam_init)
    dh2 = 2 * DH_A
    gsub = gsub_ref[...] * (1.0 - lam_init)
    for hh in range(heads):
        hs = slice(hh * dh2, (hh + 1) * dh2)
        k = k_ref[:, hs]
        v = v_ref[:, hs]
        if has_ctx:
            ck = ck_ref[hh].astype(BF16)
            cv = cv_ref[hh].astype(BF16)
        for r in range(q_ref.shape[0] // sub):
            rs = slice(r * sub, (r + 1) * sub)
            q = q_ref[rs, hs]
            a, a2 = None, None
            for m in range(2):
                sl = slice(m * DH_A, (m + 1) * DH_A)
                s = _dot_nt(q[:, sl], k[:, sl])
                s2 = _dot_nt(q[:, sl], ck[:, sl]) if has_ctx else None
                e, e2, inv = _softmax_exp(s, s2, True)
                w = inv if m == 0 else -(lam * inv)
                a = e * w if a is None else a + e * w
                if has_ctx:
                    a2 = e2 * w if a2 is None else a2 + e2 * w
            o = _dot(a.astype(BF16), v)
            if has_ctx:
                o = o + _dot(a2.astype(BF16), cv)
            o_ref[rs, hs] = (o * _rms(o, dh2) * gsub).astype(BF16)


def _diff_attn_ctx(q, k, v, lam_vecs, gsub, e, lam_init):
    spec = pl.BlockSpec((SEQ, W_A), lambda b: (b, 0))
    return pl.pallas_call(
        functools.partial(_diff_attn_kernel, lam_init=lam_init, heads=H_A, has_ctx=False, sub=SEQ),
        out_shape=jax.ShapeDtypeStruct((T_CTX, W_A), BF16),
        grid=(BATCH,),
        in_specs=[
            pl.BlockSpec((None, 4, DH_A), lambda b: (e, 0, 0)),
            pl.BlockSpec((None, 1, 2 * DH_A), lambda b: (e, 0, 0)),
            spec, spec, spec,
        ],
        out_specs=spec,
        compiler_params=_cparams(("parallel",)),
        name="diff_attn_ctx",
    )(lam_vecs, gsub, q, k, v)


def _diff_attn_smp(q, k, v, cache_k, cache_v, lam_vecs, gsub, e, lam_init):
    tq, sub = 1024, 256
    w = 2 * DH_A
    nq = DEC_SEQ // tq
    cache_spec = pl.BlockSpec((None, None, 1, PAST_LEN, w), lambda b, h, i: (b, e, h, 0, 0))
    return pl.pallas_call(
        functools.partial(_diff_attn_kernel, lam_init=lam_init, heads=1, has_ctx=True, sub=sub),
        out_shape=jax.ShapeDtypeStruct((T_SMP, W_A), BF16),
        grid=(DEC_BATCH, H_A, nq),
        in_specs=[
            pl.BlockSpec((None, 4, DH_A), lambda b, h, i: (e, 0, 0)),
            pl.BlockSpec((None, 1, w), lambda b, h, i: (e, 0, 0)),
            pl.BlockSpec((tq, w), lambda b, h, i: (b * nq + i, h)),
            pl.BlockSpec((DEC_SEQ, w), lambda b, h, i: (b, h)),
            pl.BlockSpec((DEC_SEQ, w), lambda b, h, i: (b, h)),
            cache_spec, cache_spec,
        ],
        out_specs=pl.BlockSpec((tq, w), lambda b, h, i: (b * nq + i, h)),
        compiler_params=_cparams(("parallel", "parallel", "arbitrary")),
        name="diff_attn_smp",
    )(lam_vecs, gsub, q, k, v, cache_k, cache_v)


def _fourier_kernel(x_ref, bc_ref, bs_ref, cs_ref, ss_ref, o_ref, y1_ref, y2_ref, *, scale):
    @pl.when(pl.program_id(1) == 0)
    def _():
        x = x_ref[...]
        y1_ref[...] = _dot(x, bc_ref[...]).astype(BF16)
        y2_ref[...] = _dot(x, bs_ref[...]).astype(BF16)

    out = _dot(cs_ref[...], y1_ref[...]) - _dot(ss_ref[...], y2_ref[...])
    o_ref[...] = (out * scale).astype(BF16)


def _fourier(f, s, dft_ch, dft_seq):
    tr = 256
    nb = f.shape[0] // s
    bc, bs = dft_ch
    cs, ss = dft_seq
    return pl.pallas_call(
        functools.partial(_fourier_kernel, scale=1.0 / math.sqrt(s * DG_B)),
        out_shape=jax.ShapeDtypeStruct(f.shape, BF16),
        grid=(nb, s // tr),
        in_specs=[
            pl.BlockSpec((s, W_B), lambda b, r: (b, 0)),
            pl.BlockSpec((W_B, W_B), lambda b, r: (0, 0)),
            pl.BlockSpec((W_B, W_B), lambda b, r: (0, 0)),
            pl.BlockSpec((tr, s), lambda b, r: (r, 0)),
            pl.BlockSpec((tr, s), lambda b, r: (r, 0)),
        ],
        out_specs=pl.BlockSpec((tr, W_B), lambda b, r: (b * (s // tr) + r, 0)),
        scratch_shapes=[pltpu.VMEM((s, W_B), BF16), pltpu.VMEM((s, W_B), BF16)],
        compiler_params=_cparams(("parallel", "arbitrary")),
        name="fourier",
    )(f, bc, bs, cs, ss)


def _dft_angles(rows, stride, n):
    j = jnp.arange(rows, dtype=jnp.int32)[:, None] * stride
    k = jnp.arange(n, dtype=jnp.int32)[None, :]
    return ((j * k) % n).astype(F32) * (2.0 * math.pi / n)


def _dft_mats(n):
    lo = min(n, 64)
    a0 = _dft_angles(lo, 1, n)
    c0, s0 = jnp.cos(a0), jnp.sin(a0)
    if lo == n:
        return c0, s0
    a1 = _dft_angles(n // lo, lo, n)
    c1, s1 = jnp.cos(a1)[:, None, :], jnp.sin(a1)[:, None, :]
    c = c1 * c0[None] - s1 * s0[None]
    s = s1 * c0[None] + c1 * s0[None]
    return c.reshape(n, n), s.reshape(n, n)


def _dft_channel_mats():
    c, s = _dft_mats(DG_B)
    eye = jnp.eye(G_B, dtype=F32)
    return jnp.kron(eye, c).astype(BF16), jnp.kron(eye, s).astype(BF16)


def _outproj_kernel(*refs, n_act):
    x_ref, mod_ref, g_ref = refs[:3]
    act = refs[3:3 + 3 * n_act]
    o_ref, hn_ref = refs[3 + 3 * n_act:]
    is_ctx = pl.program_id(0) < T_CTX // TM
    y = None
    for a in range(n_act):
        ctx_ref, smp_ref, w_ref = act[3 * a:3 * a + 3]
        lhs = jnp.where(is_ctx, ctx_ref[...], smp_ref[...])
        t = _dot(lhs, w_ref[...])
        y = t if y is None else y + t
    x_new = x_ref[...] + mod_ref[5:6, :] * y
    o_ref[...] = x_new
    hn_ref[...] = _norm_mod(x_new, g_ref[...], mod_ref, 6).astype(BF16)


def _outproj(x, mod_l, g_norm_l, acts):
    n_ctx = T_CTX // TM
    in_specs = [
        pl.BlockSpec((TM, D_MODEL), lambda i: (i, 0)),
        pl.BlockSpec((None, N_MOD, D_MODEL), lambda i: (_cond_row(i), 0, 0)),
        pl.BlockSpec((None, 1, D_MODEL), lambda i: (2, 0, 0)),
    ]
    args = [x, mod_l, g_norm_l]
    for ctx, smp, w, idx, rb in acts:
        kdim = ctx.shape[1]
        in_specs += [
            pl.BlockSpec((TM, kdim), lambda i: (jnp.minimum(i, n_ctx - 1), 0)),
            pl.BlockSpec((TM, kdim), lambda i: (jnp.maximum(i - n_ctx, 0), 0)),
            pl.BlockSpec((None, kdim, D_MODEL), lambda i, idx=idx, rb=rb: (idx, rb, 0)),
        ]
        args += [ctx, smp, w]
    row_spec = pl.BlockSpec((TM, D_MODEL), lambda i: (i, 0))
    return pl.pallas_call(
        functools.partial(_outproj_kernel, n_act=len(acts)),
        out_shape=(jax.ShapeDtypeStruct((T_ALL, D_MODEL), F32), jax.ShapeDtypeStruct((T_ALL, D_MODEL), BF16)),
        grid=(T_ALL // TM,),
        in_specs=in_specs,
        out_specs=(row_spec, row_spec),
        compiler_params=_cparams(("parallel",)),
        name="outproj",
    )(*args)


MLA_TM = 256


def _mla_proj_kernel(x_ref, mod_ref, g_ref, win_ref, wuq_ref, wukv_ref, gql_ref, gkvl_ref, gq_ref, gk_ref,
                     cos_ref, sa_ref, sb_ref, q_ref, k_ref, v_ref, ckv_ref, kpe_ref):
    h = _norm_mod(x_ref[...], g_ref[...], mod_ref, 3).astype(BF16)
    res = _dot(h, win_ref[...])
    cq = res[:, :Q_LORA]
    cqn = (cq * _rms(cq, Q_LORA) * gql_ref[...]).astype(BF16)
    ckv = res[:, Q_LORA:Q_LORA + KV_LORA]
    ckvn = ckv * _rms(ckv, KV_LORA) * gkvl_ref[...]
    ckv_ref[...] = ckvn
    kpe = res[:, Q_LORA + KV_LORA:]
    kpe_ref[...] = kpe
    cos, sa, sb = cos_ref[...], sa_ref[...], sb_ref[...]
    gq, gk = gq_ref[...] * (DQK_C ** -0.5 * LOG2E), gk_ref[...]
    ss_pe = jnp.sum(kpe * kpe, axis=-1, keepdims=True)
    kr = _rope(kpe * gk[:, LANE:], cos, sa, sb, ROPE_C // 4)
    ckvb = ckvn.astype(BF16)
    for hh in range(H_C):
        hs = slice(hh * HEAD_PAD, (hh + 1) * HEAD_PAD)
        lo = slice(hh * HEAD_PAD, hh * HEAD_PAD + LANE)
        hi = slice(hh * HEAD_PAD + LANE, (hh + 1) * HEAD_PAD)
        xh = _dot(cqn, wuq_ref[:, hs])
        y = xh * _rms(xh, DQK_C) * gq
        q_ref[:, lo] = y[:, :LANE].astype(BF16)
        q_ref[:, hi] = _rope(y[:, LANE:], cos, sa, sb, ROPE_C // 4).astype(BF16)
        kv = _dot(ckvb, wukv_ref[:, hs])
        k_nope = kv[:, :LANE]
        ss = jnp.sum(k_nope * k_nope, axis=-1, keepdims=True) + ss_pe
        r = lax.rsqrt(ss * (1.0 / DQK_C) + EPS)
        k_ref[:, lo] = (k_nope * r * gk[:, :LANE]).astype(BF16)
        k_ref[:, hi] = (kr * r).astype(BF16)
        v_ref[:, hh * DV_C:(hh + 1) * DV_C] = kv[:, LANE:].astype(BF16)


def _mla_proj(x, mod_l, g_norm_l, w_in_pad, w_uq_pad, w_ukv, gq_lora, gkv_lora, gq_pad, gk_pad, o, tabs):
    tm = MLA_TM
    n_in = Q_LORA + KV_LORA + LANE
    resident = dict(pipeline_mode=pl.Buffered(1))
    tab_spec = pl.BlockSpec((tm, LANE), lambda i: (i, 0))
    row = lambda n: pl.BlockSpec((tm, n), lambda i: (i, 0))
    return pl.pallas_call(
        _mla_proj_kernel,
        out_shape=(jax.ShapeDtypeStruct((T_ALL, H_C * HEAD_PAD), BF16),
                   jax.ShapeDtypeStruct((T_ALL, H_C * HEAD_PAD), BF16),
                   jax.ShapeDtypeStruct((T_ALL, H_C * DV_C), BF16),
                   jax.ShapeDtypeStruct((T_ALL, KV_LORA), F32),
                   jax.ShapeDtypeStruct((T_ALL, LANE), F32)),
        grid=(T_ALL // tm,),
        in_specs=[
            row(D_MODEL),
            pl.BlockSpec((None, N_MOD, D_MODEL), lambda i: (_cond_row(i, tm), 0, 0)),
            pl.BlockSpec((None, 1, D_MODEL), lambda i: (1, 0, 0)),
            pl.BlockSpec((None, D_MODEL, n_in), lambda i: (o, 0, 0), **resident),
            pl.BlockSpec((None, Q_LORA, H_C * HEAD_PAD), lambda i: (o, 0, 0), **resident),
            pl.BlockSpec((None, KV_LORA, H_C * HEAD_PAD), lambda i: (o, 0, 0), **resident),
            pl.BlockSpec((None, 1, Q_LORA), lambda i: (o, 0, 0)),
            pl.BlockSpec((None, 1, KV_LORA), lambda i: (o, 0, 0)),
            pl.BlockSpec((None, 1, HEAD_PAD), lambda i: (o, 0, 0)),
            pl.BlockSpec((None, 1, HEAD_PAD), lambda i: (o, 0, 0)),
            tab_spec, tab_spec, tab_spec,
        ],
        out_specs=(row(H_C * HEAD_PAD), row(H_C * HEAD_PAD), row(H_C * DV_C), row(KV_LORA), row(LANE)),
        compiler_params=_cparams(("parallel",)),
        name="mla_proj",
    )(x, mod_l, g_norm_l, w_in_pad, w_uq_pad, w_ukv, gq_lora, gkv_lora, gq_pad, gk_pad, *tabs)


MLA_HB = 4


def _mla_kv_kernel(ckv_ref, kpe_ref, w_ref, g_ref, *rest, use_rope):
    if use_rope:
        cos_ref, sa_ref, sb_ref, k_ref, v_ref = rest
    else:
        k_ref, v_ref = rest
    ckv = ckv_ref[...].astype(BF16)
    g = g_ref[...]
    kpe = kpe_ref[...]
    ss_pe = jnp.sum(kpe * kpe, axis=-1, keepdims=True)
    kr = kpe * g[:, LANE:]
    if use_rope:
        kr = _rope(kr, cos_ref[...], sa_ref[...], sb_ref[...], ROPE_C // 4)
    for hh in range(MLA_HB):
        res = _dot(ckv, w_ref[:, hh * 2 * LANE:(hh + 1) * 2 * LANE])
        k_nope = res[:, :LANE]
        ss = jnp.sum(k_nope * k_nope, axis=-1, keepdims=True) + ss_pe
        r = lax.rsqrt(ss * (1.0 / DQK_C) + EPS)
        k_ref[:, hh * HEAD_PAD:hh * HEAD_PAD + LANE] = (k_nope * r * g[:, :LANE]).astype(BF16)
        k_ref[:, hh * HEAD_PAD + LANE:(hh + 1) * HEAD_PAD] = (kr * r).astype(BF16)
        v_ref[:, hh * DV_C:(hh + 1) * DV_C] = res[:, LANE:].astype(BF16)


def _mla_kv(ckv, kpe_pad, tile0, rows, w_ukv, gk_pad, o, tabs):
    use_rope = tabs is not None
    in_specs = [
        pl.BlockSpec((TM, KV_LORA), lambda i, j: (tile0 + i, 0)),
        pl.BlockSpec((TM, LANE), lambda i, j: (tile0 + i, 0)),
        pl.BlockSpec((None, KV_LORA, MLA_HB * 2 * LANE), lambda i, j: (o, 0, j)),
        pl.BlockSpec((None, 1, HEAD_PAD), lambda i, j: (o, 0, 0)),
    ]
    args = [ckv, kpe_pad, w_ukv, gk_pad]
    if use_rope:
        in_specs += [pl.BlockSpec((TM, LANE), lambda i, j: (i % (DEC_SEQ // TM), 0))] * 3
        args += list(tabs)
    return pl.pallas_call(
        functools.partial(_mla_kv_kernel, use_rope=use_rope),
        out_shape=(jax.ShapeDtypeStruct((rows, H_C * HEAD_PAD), BF16),
                   jax.ShapeDtypeStruct((rows, H_C * DV_C), BF16)),
        grid=(rows // TM, H_C // MLA_HB),
        in_specs=in_specs,
        out_specs=(pl.BlockSpec((TM, MLA_HB * HEAD_PAD), lambda i, j: (i, j)),
                   pl.BlockSpec((TM, MLA_HB * DV_C), lambda i, j: (i, j))),
        compiler_params=_cparams(("parallel", "parallel")),
        name="mla_kv",
    )(*args)


def _mla_attn_kernel(q_ref, k_ref, v_ref, *rest, heads, has_ctx, sub):
    if has_ctx:
        ck_ref, cv_ref, o_ref, v1_ref, cv1_ref = rest
    else:
        o_ref, v1_ref = rest
    def with_ones(dst_ref, hh, v):
        dst_ref[hh, :, :DV_C] = v
        dst_ref[hh, :, DV_C:] = jnp.ones_like(v)
        return dst_ref[hh]

    for hh in range(heads):
        ks = slice(hh * HEAD_PAD, (hh + 1) * HEAD_PAD)
        vs = slice(hh * DV_C, (hh + 1) * DV_C)
        v1 = with_ones(v1_ref, hh, v_ref[:, vs])
        if has_ctx:
            cv1 = with_ones(cv1_ref, hh, cv_ref[:, vs])
        for r in range(q_ref.shape[0] // sub):
            rs = slice(r * sub, (r + 1) * sub)
            q = q_ref[rs, ks]
            s = _dot_nt(q, k_ref[:, ks])
            s2 = _dot_nt(q, ck_ref[:, ks]) if has_ctx else None
            e, e2, _ = _softmax_exp(s, s2, False)
            o = _dot(e.astype(BF16), v1)
            if has_ctx:
                o = o + _dot(e2.astype(BF16), cv1)
            o_ref[rs, vs] = (o[:, :DV_C] * (1.0 / o[:, DV_C:])).astype(BF16)


def _mla_attn_ctx(q, k, v):
    heads = 8
    return pl.pallas_call(
        functools.partial(_mla_attn_kernel, heads=heads, has_ctx=False, sub=SEQ),
        out_shape=jax.ShapeDtypeStruct((T_CTX, H_C * DV_C), BF16),
        grid=(BATCH, H_C // heads),
        in_specs=[
            pl.BlockSpec((SEQ, heads * HEAD_PAD), lambda b, j: (b, j)),
            pl.BlockSpec((SEQ, heads * HEAD_PAD), lambda b, j: (b, j)),
            pl.BlockSpec((SEQ, heads * DV_C), lambda b, j: (b, j)),
        ],
        out_specs=pl.BlockSpec((SEQ, heads * DV_C), lambda b, j: (b, j)),
        scratch_shapes=[pltpu.VMEM((heads, SEQ, 2 * DV_C), BF16)],
        compiler_params=_cparams(("parallel", "parallel")),
        name="mla_attn_ctx",
    )(q, k, v)


def _mla_attn_smp(q, k, v, ck, cv):
    tq, sub = 1024, 256
    nq = DEC_SEQ // tq
    q0, s0 = T_CTX // tq, T_CTX // DEC_SEQ
    return pl.pallas_call(
        functools.partial(_mla_attn_kernel, heads=1, has_ctx=True, sub=sub),
        out_shape=jax.ShapeDtypeStruct((T_SMP, H_C * DV_C), BF16),
        grid=(DEC_BATCH, H_C, nq),
        in_specs=[
            pl.BlockSpec((tq, HEAD_PAD), lambda b, h, i: (q0 + b * nq + i, h)),
            pl.BlockSpec((DEC_SEQ, HEAD_PAD), lambda b, h, i: (s0 + b, h)),
            pl.BlockSpec((DEC_SEQ, DV_C), lambda b, h, i: (s0 + b, h)),
            pl.BlockSpec((PAST_LEN, HEAD_PAD), lambda b, h, i: (b, h)),
            pl.BlockSpec((PAST_LEN, DV_C), lambda b, h, i: (b, h)),
        ],
        out_specs=pl.BlockSpec((tq, DV_C), lambda b, h, i: (b * nq + i, h)),
        scratch_shapes=[pltpu.VMEM((1, DEC_SEQ, 2 * DV_C), BF16), pltpu.VMEM((1, PAST_LEN, 2 * DV_C), BF16)],
        compiler_params=_cparams(("parallel", "parallel", "arbitrary")),
        name="mla_attn_smp",
    )(q, k, v, ck, cv)


def _axial_angles(rows, rot_dim):
    row = jnp.repeat(jnp.arange(rows, dtype=F32), GRID_W)
    col = jnp.tile(jnp.arange(GRID_W, dtype=F32), rows)
    half = rot_dim // 2
    inv_freq = ROPE_BASE ** (-jnp.arange(0, half, 2, dtype=F32) / half)
    ang_r = row[:, None] * inv_freq[None, :]
    ang_c = col[:, None] * inv_freq[None, :]
    return jnp.concatenate([ang_r, ang_r, ang_c, ang_c], axis=-1)


def _rope_tables(rows, rot_dim):
    ang = _axial_angles(rows, rot_dim)
    q = rot_dim // 4
    blk = (jnp.arange(rot_dim) // q) % 2
    cos, sin = jnp.cos(ang), jnp.sin(ang)
    sin_a = jnp.where(blk == 0, -sin, 0.0)
    sin_b = jnp.where(blk == 1, sin, 0.0)

    def full(t, fill):
        return jnp.pad(t, ((0, 0), (0, LANE - rot_dim)), constant_values=fill)

    return full(cos, 1.0), full(sin_a, 0.0), full(sin_b, 0.0)


def _joint_tables(tabs):
    def joint(t, fill):
        return jnp.concatenate([jnp.full((T_CTX, LANE), fill, F32), jnp.tile(t, (DEC_BATCH, 1))], axis=0)
    return joint(tabs[0], 1.0), joint(tabs[1], 0.0), joint(tabs[2], 0.0)


def kernel(x_prompt, x_sample, c, cache_diff_k, cache_diff_v, cache_mla_ckv, cache_mla_kpe, c_ctx, w_mod, b_mod, g_norm, w_ffn_gate, w_ffn_up, w_ffn_down, w_in_ab, w_out_ab, g_qk_diff, diff_lambda, g_diff_sub, w_in_mla, g_q_lora, w_uq, g_kv_lora, w_ukv, g_qk_mla, w_o_mla):
    rows = DEC_SEQ // GRID_W
    tabs_a = _rope_tables(rows, DH_A)
    tabs_c_joint = _joint_tables(_rope_tables(rows, ROPE_C))
    dft_ch = _dft_channel_mats()
    dft_ctx = tuple(m.astype(BF16) for m in _dft_mats(SEQ))
    dft_smp = tuple(m.astype(BF16) for m in _dft_mats(DEC_SEQ))
    p_ctx, p_smp = _Pass(False), _Pass(True)

    w_ffn = (w_ffn_gate[0, 0].astype(BF16), w_ffn_up[0, 0].astype(BF16), w_ffn_down[0, 0].astype(BF16))
    w_in_ab_b, w_out_ab_b = w_in_ab.astype(BF16), w_out_ab.astype(BF16)
    w_in_mla_b = jnp.pad(w_in_mla.astype(BF16), ((0, 0), (0, 0), (0, LANE - ROPE_C)))
    w_uq_b = jnp.pad(w_uq.astype(BF16).reshape(N_ODD, Q_LORA, H_C, DQK_C),
                     ((0, 0), (0, 0), (0, 0), (0, HEAD_PAD - DQK_C))).reshape(N_ODD, Q_LORA, H_C * HEAD_PAD)
    w_ukv_b, w_o_b = w_ukv.astype(BF16), w_o_mla.astype(BF16)
    g_qk_mla_pad = jnp.pad(g_qk_mla, ((0, 0), (0, 0), (0, HEAD_PAD - DQK_C)))[:, :, None, :]
    gq_mla, gk_mla = g_qk_mla_pad[:, 0], g_qk_mla_pad[:, 1]
    g_norm4 = g_norm[:, :, None, :]
    gqk_diff = g_qk_diff[:, :, None, :]
    g_sub3 = g_diff_sub[:, None, :]
    g_q_lora3, g_kv_lora3 = g_q_lora[:, None, :], g_kv_lora[:, None, :]

    cond8 = jnp.concatenate([c_ctx[None, :], c, jnp.zeros((8 - N_COND, D_MODEL), F32)], axis=0)
    mod = _modulation(cond8, w_mod, b_mod)[:, :N_COND].reshape(DEPTH, N_COND, N_MOD, D_MODEL)

    x_ctx, x_smp = x_prompt.reshape(T_CTX, D_MODEL), x_sample.reshape(T_SMP, D_MODEL)
    n_ctx, n_smp, n_all = T_CTX // FFN_TM, T_SMP // FFN_TM, T_ALL // FFN_TM
    w_stacks = (w_ffn_gate, w_ffn_up, w_ffn_down)
    new_kv = None
    mla_ckv, mla_kpe = [], []

    for l in range(DEPTH):
        mod_l = mod[l]
        if l == 0:
            h = _prenorm((x_ctx, x_smp), mod_l, g_norm4[l])
            x, *w_next = _ffn(h, x_ctx, 0, 0, n_ctx, mod_l, w_ffn, 0, (*w_stacks, l, 1), "joint")
            (x,) = _ffn(h, x_smp, 0, n_ctx, n_smp, mod_l, w_ffn, 0, None, ("joint", x))
            w_ffn = w_next
        else:
            h = _prenorm((x,), mod_l, g_norm4[l])
            x, *w_ffn = _ffn(h, x, 0, 0, n_all, mod_l, w_ffn, 0, (*w_stacks, l, 1), "joint")
        if l % 2 == 0:
            e = l // 2
            lam_init = 0.8 - 0.6 * math.exp(-0.3 * l)
            q_c, k_c, v_c, f_c, nk, nv = _inproj_even(p_ctx, x, mod_l, g_norm4[l], w_in_ab_b, gqk_diff, e,
                                                     None, new_kv)
            new_kv = (nk, nv)
            q_s, k_s, v_s, f_s = _inproj_even(p_smp, x, mod_l, g_norm4[l], w_in_ab_b, gqk_diff, e, tabs_a, None)
            o_ctx = _diff_attn_ctx(q_c, k_c, v_c, diff_lambda, g_sub3, e, lam_init)
            o_smp = _diff_attn_smp(q_s, k_s, v_s, cache_diff_k, cache_diff_v, diff_lambda, g_sub3, e, lam_init)
            fo_ctx = _fourier(f_c, SEQ, dft_ch, dft_ctx)
            fo_smp = _fourier(f_s, DEC_SEQ, dft_ch, dft_smp)
            x, h = _outproj(x, mod_l, g_norm4[l],
                            [(o_ctx, o_smp, w_out_ab_b, e, 0), (fo_ctx, fo_smp, w_out_ab_b, e, W_A // W_B)])
        else:
            o = l // 2
            q, k, v, ckvn, kpe_pad = _mla_proj(x, mod_l, g_norm4[l], w_in_mla_b, w_uq_b, w_ukv_b, g_q_lora3,
                                               g_kv_lora3, gq_mla, gk_mla, o, tabs_c_joint)
            cache_ckv = cache_mla_ckv[:, o].reshape(DEC_BATCH * PAST_LEN, KV_LORA)
            cache_kpe = jnp.pad(cache_mla_kpe[:, o].reshape(DEC_BATCH * PAST_LEN, ROPE_C),
                                ((0, 0), (0, LANE - ROPE_C)))
            ck, cv = _mla_kv(cache_ckv, cache_kpe, 0, DEC_BATCH * PAST_LEN, w_ukv_b, gk_mla, o, None)
            a_ctx = _mla_attn_ctx(q, k, v)
            a_smp = _mla_attn_smp(q, k, v, ck, cv)
            x, h = _outproj(x, mod_l, g_norm4[l], [(a_ctx, a_smp, w_o_b, o, 0)])
            mla_ckv.append(ckvn[:T_CTX].reshape(BATCH, SEQ, KV_LORA))
            mla_kpe.append(kpe_pad[:T_CTX, :ROPE_C].reshape(BATCH, SEQ, ROPE_C))
        if l + 1 < DEPTH:
            x, *w_ffn = _ffn(h, x, 0, 0, n_all, mod_l, w_ffn, 1, (*w_stacks, l + 1, 0), "joint")
        else:
            (y_ctx,) = _ffn(h, x, 0, 0, n_ctx, mod_l, w_ffn, 1, None, "own")
            (y_smp,) = _ffn(h, x, n_ctx, n_ctx, n_smp, mod_l, w_ffn, 1, None, "own")

    y_prompt = y_ctx.reshape(BATCH, SEQ, D_MODEL)
    y_sample = y_smp.reshape(DEC_BATCH, DEC_SEQ, D_MODEL)
    return (y_prompt, y_sample, new_kv[0], new_kv[1], jnp.stack(mla_ckv, axis=1), jnp.stack(mla_kpe, axis=1))
```

```python
import functools
import math

import jax
import jax.numpy as jnp
from jax import lax
from jax.experimental import pallas as pl
from jax.experimental.pallas import tpu as pltpu

F32 = jnp.float32
BF16 = jnp.bfloat16

D_MODEL = 2048
BATCH = 32
SEQ = 256
DEPTH = 4
DEC_BATCH = 2
DEC_SEQ = 2048
PAST_LEN = 256
GRID_W = 64
N_EVEN = 2
N_ODD = 2
H_A = 6
DH_A = 128
W_A = H_A * 2 * DH_A
G_B = 4
DG_B = 128
W_B = G_B * DG_B
H_C = 16
Q_LORA = 1536
KV_LORA = 512
NOPE_C = 128
ROPE_C = 64
DQK_C = NOPE_C + ROPE_C
DV_C = 128
D_FF = 5632
N_MOD = 9
ROPE_BASE = 10000.0
EPS = 1e-6
LOG2E = 1.4426950408889634

T_CTX = BATCH * SEQ
T_SMP = DEC_BATCH * DEC_SEQ
T_ALL = T_CTX + T_SMP
N_COND = 1 + DEC_BATCH
LANE = 128
HEAD_PAD = 256
TM = 512

VMEM_LIMIT = 52 * 1024 * 1024
FFN_TM = 1024
FFN_VMEM_LIMIT = 60 * 1024 * 1024


def _cparams(sem):
    return pltpu.CompilerParams(dimension_semantics=sem, vmem_limit_bytes=VMEM_LIMIT)


def _cond_row(i, tm=TM):
    n_ctx = T_CTX // tm
    per = DEC_SEQ // tm
    return jnp.where(i < n_ctx, 0, 1 + (i - n_ctx) // per)


class _Pass:
    def __init__(self, latent):
        self.latent = latent
        self.rows = T_SMP if latent else T_CTX
        self.tile0 = (T_CTX // TM) if latent else 0
        self.tiles = self.rows // TM
        self.seq = DEC_SEQ if latent else SEQ
        self.batch = DEC_BATCH if latent else BATCH

    def cond(self, i):
        return 1 + i // (DEC_SEQ // TM) if self.latent else 0


def _dot(a, b):
    return jnp.dot(a, b, preferred_element_type=F32)


def _dot_nt(a, b):
    return lax.dot_general(a, b, (((1,), (1,)), ((), ())), preferred_element_type=F32)


def _silu(x):
    return x / (1.0 + jnp.exp(-x))


def _rms(x, denom):
    return lax.rsqrt(jnp.sum(x * x, axis=-1, keepdims=True) * (1.0 / denom) + EPS)


def _norm_mod(x, g, mod_ref, base):
    sh = mod_ref[base:base + 1, :]
    sc = mod_ref[base + 1:base + 2, :]
    return (x * _rms(x, D_MODEL) * g) * (1.0 + sc) + sh


def _rope(y, cos, sin_a, sin_b, q):
    return y * cos + pltpu.roll(y, LANE - q, 1) * sin_a + pltpu.roll(y, q, 1) * sin_b


def _softmax_exp(s, s2, need_sum):
    m = jnp.max(s, axis=-1, keepdims=True)
    if s2 is not None:
        m = jnp.maximum(m, jnp.max(s2, axis=-1, keepdims=True))
    e = jnp.exp2(s - m)
    e2 = jnp.exp2(s2 - m) if s2 is not None else None
    if not need_sum:
        return e, e2, None
    l = jnp.sum(e, axis=-1, keepdims=True)
    if s2 is not None:
        l = l + jnp.sum(e2, axis=-1, keepdims=True)
    return e, e2, 1.0 / l


def _mod_kernel(c_ref, w_ref, b_ref, o_ref):
    a = _silu(c_ref[...]).astype(BF16)
    o_ref[...] = _dot(a, w_ref[...].astype(BF16)) + b_ref[...]


def _modulation(cond8, w_mod, b_mod):
    tn = 1024
    n = N_MOD * D_MODEL
    return pl.pallas_call(
        _mod_kernel,
        out_shape=jax.ShapeDtypeStruct((DEPTH, 8, n), F32),
        grid=(DEPTH, n // tn),
        in_specs=[
            pl.BlockSpec((8, D_MODEL), lambda l, j: (0, 0)),
            pl.BlockSpec((None, D_MODEL, tn), lambda l, j: (l, 0, j)),
            pl.BlockSpec((None, 1, tn), lambda l, j: (l, 0, j)),
        ],
        out_specs=pl.BlockSpec((None, 8, tn), lambda l, j: (l, 0, j)),
        compiler_params=_cparams(("parallel", "parallel")),
        name="modulation",
    )(cond8, w_mod, b_mod.reshape(DEPTH, 1, n))


FFN_TF = 512
FFN_XR = 128
FFN_DC = 512
BF16_ROWS = 16


def _prenorm_kernel(*refs, split):
    if split:
        xc_ref, xs_ref, mod_ref, g_ref, h_ref = refs
        x = jnp.where(pl.program_id(0) < T_CTX // TM, xc_ref[...], xs_ref[...])
    else:
        x_ref, mod_ref, g_ref, h_ref = refs
        x = x_ref[...]
    h_ref[...] = _norm_mod(x, g_ref[...], mod_ref, 0).astype(BF16)


def _prenorm(xs, mod_l, g):
    n_ctx = T_CTX // TM
    if len(xs) == 2:
        x_specs = [pl.BlockSpec((TM, D_MODEL), lambda i: (jnp.minimum(i, n_ctx - 1), 0)),
                   pl.BlockSpec((TM, D_MODEL), lambda i: (jnp.maximum(i - n_ctx, 0), 0))]
    else:
        x_specs = [pl.BlockSpec((TM, D_MODEL), lambda i: (i, 0))]
    return pl.pallas_call(
        functools.partial(_prenorm_kernel, split=len(xs) == 2),
        out_shape=jax.ShapeDtypeStruct((T_ALL, D_MODEL), BF16),
        grid=(T_ALL // TM,),
        in_specs=x_specs + [
            pl.BlockSpec((None, N_MOD, D_MODEL), lambda i: (_cond_row(i), 0, 0)),
            pl.BlockSpec((None, 1, D_MODEL), lambda i: (0, 0, 0)),
        ],
        out_specs=pl.BlockSpec((TM, D_MODEL), lambda i: (i, 0)),
        compiler_params=_cparams(("parallel",)),
        name="prenorm",
    )(*xs, mod_l, g)


def _ffn_kernel(*refs, base, cast_next):
    it = iter(refs)
    h_ref, x_ref, mod_ref, wg_ref, wu_ref, wd_ref = [next(it) for _ in range(6)]
    nxt = [next(it) for _ in range(3)] if cast_next else []
    o_ref = next(it)
    cst = [next(it) for _ in range(3)] if cast_next else []
    f = pl.program_id(1)

    @pl.when(f == 0)
    def _():
        o_ref[...] = jnp.zeros_like(o_ref)

    h = h_ref[...]
    n_xr = o_ref.shape[0] // FFN_XR
    rows = pl.ds(pl.multiple_of(jnp.minimum(f, n_xr - 1) * FFN_XR, FFN_XR), FFN_XR)
    o_ref[rows, :] += jnp.where(f < n_xr, x_ref[...], 0.0)

    gate = _dot(h, wg_ref[...])
    up = _dot(h, wu_ref[...])
    a = (_silu(gate) * up).astype(BF16)
    for c in range(D_MODEL // FFN_DC):
        cs = slice(c * FFN_DC, (c + 1) * FFN_DC)
        o_ref[:, cs] += (0.5 * mod_ref[base + 2:base + 3, cs]) * _dot(a, wd_ref[:, cs])

    for src, dst in zip(nxt, cst):
        dst[...] = src[...].astype(BF16)


def _cast_rows(total, steps):
    rows = BF16_ROWS
    while total % rows or total // rows > steps:
        rows += BF16_ROWS
    return rows


def _ffn(h, x, x_tile0, tile0, tiles, mod_l, w_b, k, nxt):
    tm, tf = FFN_TM, FFN_TF
    nf = D_FF // tf
    n_xr = tm // FFN_XR
    assert nf >= n_xr
    wg, wu, wd = w_b
    in_specs = [
        pl.BlockSpec((tm, D_MODEL), lambda i, f: (tile0 + i, 0)),
        pl.BlockSpec((FFN_XR, D_MODEL), lambda i, f: ((x_tile0 + i) * n_xr + jnp.minimum(f, n_xr - 1), 0)),
        pl.BlockSpec((None, N_MOD, D_MODEL), lambda i, f: (_cond_row(tile0 + i, tm), 0, 0)),
        pl.BlockSpec((D_MODEL, tf), lambda i, f: (0, f)),
        pl.BlockSpec((D_MODEL, tf), lambda i, f: (0, f)),
        pl.BlockSpec((tf, D_MODEL), lambda i, f: (f, 0)),
    ]
    args = [h, x, mod_l, wg, wu, wd]
    out_shape = [jax.ShapeDtypeStruct((tiles * tm, D_MODEL), F32)]
    out_specs = [pl.BlockSpec((tm, D_MODEL), lambda i, f: (i, 0))]
    if nxt is not None:
        ng, nu, nd, l2, k2 = nxt
        gu_rows, d_rows = _cast_rows(D_MODEL, tiles * nf), _cast_rows(D_FF, tiles * nf)
        n_gu, n_d = D_MODEL // gu_rows, D_FF // d_rows
        gu_in = pl.BlockSpec((None, None, gu_rows, D_FF),
                             lambda i, f: (l2, k2, jnp.minimum(i * nf + f, n_gu - 1), 0))
        gu_out = pl.BlockSpec((gu_rows, D_FF), lambda i, f: (jnp.minimum(i * nf + f, n_gu - 1), 0))
        in_specs += [gu_in, gu_in, pl.BlockSpec((None, None, d_rows, D_MODEL),
                                                lambda i, f: (l2, k2, jnp.minimum(i * nf + f, n_d - 1), 0))]
        args += [ng, nu, nd]
        out_shape += [jax.ShapeDtypeStruct((D_MODEL, D_FF), BF16)] * 2 + [jax.ShapeDtypeStruct((D_FF, D_MODEL), BF16)]
        out_specs += [gu_out, gu_out,
                      pl.BlockSpec((d_rows, D_MODEL), lambda i, f: (jnp.minimum(i * nf + f, n_d - 1), 0))]
    return pl.pallas_call(
        functools.partial(_ffn_kernel, base=6 * k, cast_next=nxt is not None),
        out_shape=out_shape,
        grid=(tiles, nf),
        in_specs=in_specs,
        out_specs=out_specs,
        compiler_params=pltpu.CompilerParams(dimension_semantics=("arbitrary", "arbitrary"),
                                             vmem_limit_bytes=FFN_VMEM_LIMIT),
        name="ffn",
    )(*args)


IN_TM = SEQ


def _inproj_even_kernel(*refs, latent, e):
    x_ref, mod_ref, g_ref, w_ref, gq_ref, gk_ref = refs[:6]
    refs = refs[6:]
    if latent:
        cos_ref, sa_ref, sb_ref, q_ref, k_ref, v_ref, f_ref = refs
    else:
        q_ref, k_ref, v_ref, f_ref, nk_ref, nv_ref = refs[-6:]
        if e > 0:
            nk_ref[:e] = refs[0][...]
            nv_ref[:e] = refs[1][...]
        nk_ref, nv_ref = nk_ref.at[e], nv_ref.at[e]
    dh2 = 2 * DH_A
    h = _norm_mod(x_ref[...], g_ref[...], mod_ref, 3).astype(BF16)
    gains = (gq_ref[...] * (DH_A ** -0.5 * LOG2E), gk_ref[...])
    for part, (g, o_ref) in enumerate(zip(gains, (q_ref, k_ref))):
        for hh in range(H_A):
            res = _dot(h, w_ref[:, part * W_A + hh * dh2:part * W_A + (hh + 1) * dh2])
            for m in range(2):
                xc = res[:, m * DH_A:(m + 1) * DH_A]
                y = xc * _rms(xc, DH_A) * g
                if latent:
                    y = _rope(y, cos_ref[...], sa_ref[...], sb_ref[...], DH_A // 4)
                o_ref[:, hh * dh2 + m * DH_A:hh * dh2 + (m + 1) * DH_A] = y.astype(BF16)
                if part == 1 and not latent:
                    nk_ref[hh, :, m * DH_A:(m + 1) * DH_A] = y
    for hh in range(H_A):
        res = _dot(h, w_ref[:, 2 * W_A + hh * dh2:2 * W_A + (hh + 1) * dh2])
        v_ref[:, hh * dh2:(hh + 1) * dh2] = res.astype(BF16)
        if not latent:
            nv_ref[hh] = res
    f_ref[...] = _dot(h, w_ref[:, 3 * W_A:]).astype(BF16)


def _inproj_even(ps, x, mod_l, g_norm_l, w_in, gqk, e, tabs, prev_kv):
    tm = IN_TM
    t0 = ps.tile0 * (TM // tm)
    xt0 = t0 if x.shape[0] == T_ALL else 0
    in_specs = [
        pl.BlockSpec((tm, D_MODEL), lambda i: (xt0 + i, 0)),
        pl.BlockSpec((None, N_MOD, D_MODEL), lambda i: (_cond_row(t0 + i, tm), 0, 0)),
        pl.BlockSpec((None, 1, D_MODEL), lambda i: (1, 0, 0)),
        pl.BlockSpec((None, D_MODEL, 3 * W_A + W_B), lambda i: (e, 0, 0), pipeline_mode=pl.Buffered(1)),
        pl.BlockSpec((None, None, 1, DH_A), lambda i: (e, 0, 0, 0)),
        pl.BlockSpec((None, None, 1, DH_A), lambda i: (e, 1, 0, 0)),
    ]
    args = [x, mod_l, g_norm_l, w_in, gqk, gqk]
    out_shape = [jax.ShapeDtypeStruct((ps.rows, W_A), BF16)] * 3 + [jax.ShapeDtypeStruct((ps.rows, W_B), BF16)]
    out_specs = [pl.BlockSpec((tm, W_A), lambda i: (i, 0))] * 3 + [pl.BlockSpec((tm, W_B), lambda i: (i, 0))]
    if ps.latent:
        tab_spec = pl.BlockSpec((tm, LANE), lambda i: (i % (DEC_SEQ // tm), 0))
        in_specs += [tab_spec] * 3
        args += list(tabs)
    else:
        kv_block = lambda n: pl.BlockSpec((None, n, H_A, SEQ, 2 * DH_A), lambda i: (i, 0, 0, 0, 0))
        out_shape += [jax.ShapeDtypeStruct((BATCH, e + 1, H_A, SEQ, 2 * DH_A), F32)] * 2
        out_specs += [kv_block(e + 1)] * 2
        if e > 0:
            in_specs += [kv_block(e)] * 2
            args += list(prev_kv)
    return pl.pallas_call(
        functools.partial(_inproj_even_kernel, latent=ps.latent, e=e),
        out_shape=out_shape,
        grid=(ps.rows // tm,),
        in_specs=in_specs,
        out_specs=out_specs,
        compiler_params=_cparams(("parallel",)),
        name="inproj_even",
    )(*args)


def _diff_attn_kernel(lam_ref, gsub_ref, q_ref, k_ref, v_ref, *rest, lam_init, heads, has_ctx, sub):
    if has_ctx:
        ck_ref, cv_ref, o_ref = rest
    else:
        (o_ref,) = rest
    lv = lam_ref[...]
    lam = (jnp.exp(jnp.sum(lv[0:1] * lv[1:2], axis=-1, keepdims=True))
           - jnp.exp(jnp.sum(lv[2:3] * lv[3:4], axis=-1, keepdims=True)) + lam_init)
    dh2 = 2 * DH_A
    gsub = gsub_ref[...] * (1.0 - lam_init)
    for hh in range(heads):
        hs = slice(hh * dh2, (hh + 1) * dh2)
        k = k_ref[:, hs]
        v = v_ref[:, hs]
        if has_ctx:
            ck = ck_ref[hh].astype(BF16)
            cv = cv_ref[hh].astype(BF16)
        for r in range(q_ref.shape[0] // sub):
            rs = slice(r * sub, (r + 1) * sub)
            q = q_ref[rs, hs]
            a, a2 = None, None
            for m in range(2):
                sl = slice(m * DH_A, (m + 1) * DH_A)
                s = _dot_nt(q[:, sl], k[:, sl])
                s2 = _dot_nt(q[:, sl], ck[:, sl]) if has_ctx else None
                e, e2, inv = _softmax_exp(s, s2, True)
                w = inv if m == 0 else -(lam * inv)
                a = e * w if a is None else a + e * w
                if has_ctx:
                    a2 = e2 * w if a2 is None else a2 + e2 * w
            o = _dot(a.astype(BF16), v)
            if has_ctx:
                o = o + _dot(a2.astype(BF16), cv)
            o_ref[rs, hs] = (o * _rms(o, dh2) * gsub).astype(BF16)


def _diff_attn_ctx(q, k, v, lam_vecs, gsub, e, lam_init):
    spec = pl.BlockSpec((SEQ, W_A), lambda b: (b, 0))
    return pl.pallas_call(
        functools.partial(_diff_attn_kernel, lam_init=lam_init, heads=H_A, has_ctx=False, sub=SEQ),
        out_shape=jax.ShapeDtypeStruct((T_CTX, W_A), BF16),
        grid=(BATCH,),
        in_specs=[
            pl.BlockSpec((None, 4, DH_A), lambda b: (e, 0, 0)),
            pl.BlockSpec((None, 1, 2 * DH_A), lambda b: (e, 0, 0)),
            spec, spec, spec,
        ],
        out_specs=spec,
        compiler_params=_cparams(("parallel",)),
        name="diff_attn_ctx",
    )(lam_vecs, gsub, q, k, v)


def _diff_attn_smp(q, k, v, cache_k, cache_v, lam_vecs, gsub, e, lam_init):
    tq, sub = 1024, 256
    w = 2 * DH_A
    nq = DEC_SEQ // tq
    cache_spec = pl.BlockSpec((None, None, 1, PAST_LEN, w), lambda b, h, i: (b, e, h, 0, 0))
    return pl.pallas_call(
        functools.partial(_diff_attn_kernel, lam_init=lam_init, heads=1, has_ctx=True, sub=sub),
        out_shape=jax.ShapeDtypeStruct((T_SMP, W_A), BF16),
        grid=(DEC_BATCH, H_A, nq),
        in_specs=[
            pl.BlockSpec((None, 4, DH_A), lambda b, h, i: (e, 0, 0)),
            pl.BlockSpec((None, 1, w), lambda b, h, i: (e, 0, 0)),
            pl.BlockSpec((tq, w), lambda b, h, i: (b * nq + i, h)),
            pl.BlockSpec((DEC_SEQ, w), lambda b, h, i: (b, h)),
            pl.BlockSpec((DEC_SEQ, w), lambda b, h, i: (b, h)),
            cache_spec, cache_spec,
        ],
        out_specs=pl.BlockSpec((tq, w), lambda b, h, i: (b * nq + i, h)),
        compiler_params=_cparams(("parallel", "parallel", "arbitrary")),
        name="diff_attn_smp",
    )(lam_vecs, gsub, q, k, v, cache_k, cache_v)


def _fourier_kernel(x_ref, bc_ref, bs_ref, cs_ref, ss_ref, o_ref, y1_ref, y2_ref, *, scale):
    @pl.when(pl.program_id(1) == 0)
    def _():
        x = x_ref[...]
        y1_ref[...] = _dot(x, bc_ref[...]).astype(BF16)
        y2_ref[...] = _dot(x, bs_ref[...]).astype(BF16)

    out = _dot(cs_ref[...], y1_ref[...]) - _dot(ss_ref[...], y2_ref[...])
    o_ref[...] = (out * scale).astype(BF16)


def _fourier(f, s, dft_ch, dft_seq):
    tr = 256
    nb = f.shape[0] // s
    bc, bs = dft_ch
    cs, ss = dft_seq
    return pl.pallas_call(
        functools.partial(_fourier_kernel, scale=1.0 / math.sqrt(s * DG_B)),
        out_shape=jax.ShapeDtypeStruct(f.shape, BF16),
        grid=(nb, s // tr),
        in_specs=[
            pl.BlockSpec((s, W_B), lambda b, r: (b, 0)),
            pl.BlockSpec((W_B, W_B), lambda b, r: (0, 0)),
            pl.BlockSpec((W_B, W_B), lambda b, r: (0, 0)),
            pl.BlockSpec((tr, s), lambda b, r: (r, 0)),
            pl.BlockSpec((tr, s), lambda b, r: (r, 0)),
        ],
        out_specs=pl.BlockSpec((tr, W_B), lambda b, r: (b * (s // tr) + r, 0)),
        scratch_shapes=[pltpu.VMEM((s, W_B), BF16), pltpu.VMEM((s, W_B), BF16)],
        compiler_params=_cparams(("parallel", "arbitrary")),
        name="fourier",
    )(f, bc, bs, cs, ss)


def _dft_angles(rows, stride, n):
    j = jnp.arange(rows, dtype=jnp.int32)[:, None] * stride
    k = jnp.arange(n, dtype=jnp.int32)[None, :]
    return ((j * k) % n).astype(F32) * (2.0 * math.pi / n)


def _dft_mats(n):
    lo = min(n, 64)
    a0 = _dft_angles(lo, 1, n)
    c0, s0 = jnp.cos(a0), jnp.sin(a0)
    if lo == n:
        return c0, s0
    a1 = _dft_angles(n // lo, lo, n)
    c1, s1 = jnp.cos(a1)[:, None, :], jnp.sin(a1)[:, None, :]
    c = c1 * c0[None] - s1 * s0[None]
    s = s1 * c0[None] + c1 * s0[None]
    return c.reshape(n, n), s.reshape(n, n)


def _dft_channel_mats():
    c, s = _dft_mats(DG_B)
    eye = jnp.eye(G_B, dtype=F32)
    return jnp.kron(eye, c).astype(BF16), jnp.kron(eye, s).astype(BF16)


def _outproj_kernel(*refs, n_act, n_x):
    is_ctx = pl.program_id(0) < T_CTX // TM
    x = refs[0][...] if n_x == 1 else jnp.where(is_ctx, refs[0][...], refs[1][...])
    mod_ref, g_ref = refs[n_x:n_x + 2]
    act = refs[n_x + 2:n_x + 2 + 3 * n_act]
    o_ref, hn_ref = refs[n_x + 2 + 3 * n_act:]
    y = None
    for a in range(n_act):
        ctx_ref, smp_ref, w_ref = act[3 * a:3 * a + 3]
        lhs = jnp.where(is_ctx, ctx_ref[...], smp_ref[...])
        t = _dot(lhs, w_ref[...])
        y = t if y is None else y + t
    x_new = x + mod_ref[5:6, :] * y
    o_ref[...] = x_new
    hn_ref[...] = _norm_mod(x_new, g_ref[...], mod_ref, 6).astype(BF16)


def _outproj(xs, mod_l, g_norm_l, acts):
    n_ctx = T_CTX // TM
    row_spec = pl.BlockSpec((TM, D_MODEL), lambda i: (i, 0))
    ctx_rows = lambda n: pl.BlockSpec((TM, n), lambda i: (jnp.minimum(i, n_ctx - 1), 0))
    smp_rows = lambda n: pl.BlockSpec((TM, n), lambda i: (jnp.maximum(i - n_ctx, 0), 0))
    in_specs = ([row_spec] if len(xs) == 1 else [ctx_rows(D_MODEL), smp_rows(D_MODEL)]) + [
        pl.BlockSpec((None, N_MOD, D_MODEL), lambda i: (_cond_row(i), 0, 0)),
        pl.BlockSpec((None, 1, D_MODEL), lambda i: (2, 0, 0)),
    ]
    args = [*xs, mod_l, g_norm_l]
    for ctx, smp, w, idx, rb in acts:
        kdim = ctx.shape[1]
        in_specs += [ctx_rows(kdim), smp_rows(kdim),
                     pl.BlockSpec((None, kdim, D_MODEL), lambda i, idx=idx, rb=rb: (idx, rb, 0))]
        args += [ctx, smp, w]
    return pl.pallas_call(
        functools.partial(_outproj_kernel, n_act=len(acts), n_x=len(xs)),
        out_shape=(jax.ShapeDtypeStruct((T_ALL, D_MODEL), F32), jax.ShapeDtypeStruct((T_ALL, D_MODEL), BF16)),
        grid=(T_ALL // TM,),
        in_specs=in_specs,
        out_specs=(row_spec, row_spec),
        compiler_params=_cparams(("parallel",)),
        name="outproj",
    )(*args)


MLA_TM = 256


def _mla_proj_kernel(x_ref, mod_ref, g_ref, win_ref, wuq_ref, wukv_ref, gql_ref, gkvl_ref, gq_ref, gk_ref,
                     cos_ref, sa_ref, sb_ref, q_ref, k_ref, v_ref, ckv_ref, kpe_ref):
    h = _norm_mod(x_ref[...], g_ref[...], mod_ref, 3).astype(BF16)
    res = _dot(h, win_ref[...])
    cq = res[:, :Q_LORA]
    cqn = (cq * _rms(cq, Q_LORA) * gql_ref[...]).astype(BF16)
    ckv = res[:, Q_LORA:Q_LORA + KV_LORA]
    ckvn = ckv * _rms(ckv, KV_LORA) * gkvl_ref[...]
    ckv_ref[...] = ckvn
    kpe = res[:, Q_LORA + KV_LORA:]
    kpe_ref[...] = kpe
    cos, sa, sb = cos_ref[...], sa_ref[...], sb_ref[...]
    gq, gk = gq_ref[...] * (DQK_C ** -0.5 * LOG2E), gk_ref[...]
    ss_pe = jnp.sum(kpe * kpe, axis=-1, keepdims=True)
    kr = _rope(kpe * gk[:, LANE:], cos, sa, sb, ROPE_C // 4)
    ckvb = ckvn.astype(BF16)
    for hh in range(H_C):
        hs = slice(hh * HEAD_PAD, (hh + 1) * HEAD_PAD)
        lo = slice(hh * HEAD_PAD, hh * HEAD_PAD + LANE)
        hi = slice(hh * HEAD_PAD + LANE, (hh + 1) * HEAD_PAD)
        xh = _dot(cqn, wuq_ref[:, hs])
        y = xh * _rms(xh, DQK_C) * gq
        q_ref[:, lo] = y[:, :LANE].astype(BF16)
        q_ref[:, hi] = _rope(y[:, LANE:], cos, sa, sb, ROPE_C // 4).astype(BF16)
        kv = _dot(ckvb, wukv_ref[:, hs])
        k_nope = kv[:, :LANE]
        ss = jnp.sum(k_nope * k_nope, axis=-1, keepdims=True) + ss_pe
        r = lax.rsqrt(ss * (1.0 / DQK_C) + EPS)
        k_ref[:, lo] = (k_nope * r * gk[:, :LANE]).astype(BF16)
        k_ref[:, hi] = (kr * r).astype(BF16)
        v_ref[:, hh * DV_C:(hh + 1) * DV_C] = kv[:, LANE:].astype(BF16)


def _mla_proj(x, mod_l, g_norm_l, w_in_pad, w_uq_pad, w_ukv, gq_lora, gkv_lora, gq_pad, gk_pad, o, tabs):
    tm = MLA_TM
    n_in = Q_LORA + KV_LORA + LANE
    resident = dict(pipeline_mode=pl.Buffered(1))
    tab_spec = pl.BlockSpec((tm, LANE), lambda i: (i, 0))
    row = lambda n: pl.BlockSpec((tm, n), lambda i: (i, 0))
    return pl.pallas_call(
        _mla_proj_kernel,
        out_shape=(jax.ShapeDtypeStruct((T_ALL, H_C * HEAD_PAD), BF16),
                   jax.ShapeDtypeStruct((T_ALL, H_C * HEAD_PAD), BF16),
                   jax.ShapeDtypeStruct((T_ALL, H_C * DV_C), BF16),
                   jax.ShapeDtypeStruct((T_ALL, KV_LORA), F32),
                   jax.ShapeDtypeStruct((T_ALL, LANE), F32)),
        grid=(T_ALL // tm,),
        in_specs=[
            row(D_MODEL),
            pl.BlockSpec((None, N_MOD, D_MODEL), lambda i: (_cond_row(i, tm), 0, 0)),
            pl.BlockSpec((None, 1, D_MODEL), lambda i: (1, 0, 0)),
            pl.BlockSpec((None, D_MODEL, n_in), lambda i: (o, 0, 0), **resident),
            pl.BlockSpec((None, Q_LORA, H_C * HEAD_PAD), lambda i: (o, 0, 0), **resident),
            pl.BlockSpec((None, KV_LORA, H_C * HEAD_PAD), lambda i: (o, 0, 0), **resident),
            pl.BlockSpec((None, 1, Q_LORA), lambda i: (o, 0, 0)),
            pl.BlockSpec((None, 1, KV_LORA), lambda i: (o, 0, 0)),
            pl.BlockSpec((None, 1, HEAD_PAD), lambda i: (o, 0, 0)),
            pl.BlockSpec((None, 1, HEAD_PAD), lambda i: (o, 0, 0)),
            tab_spec, tab_spec, tab_spec,
        ],
        out_specs=(row(H_C * HEAD_PAD), row(H_C * HEAD_PAD), row(H_C * DV_C), row(KV_LORA), row(LANE)),
        compiler_params=_cparams(("parallel",)),
        name="mla_proj",
    )(x, mod_l, g_norm_l, w_in_pad, w_uq_pad, w_ukv, gq_lora, gkv_lora, gq_pad, gk_pad, *tabs)


MLA_HB = 4


def _mla_kv_kernel(ckv_ref, kpe_ref, w_ref, g_ref, *rest, use_rope):
    if use_rope:
        cos_ref, sa_ref, sb_ref, k_ref, v_ref = rest
    else:
        k_ref, v_ref = rest
    ckv = ckv_ref[...].astype(BF16)
    g = g_ref[...]
    kpe = kpe_ref[...]
    ss_pe = jnp.sum(kpe * kpe, axis=-1, keepdims=True)
    kr = kpe * g[:, LANE:]
    if use_rope:
        kr = _rope(kr, cos_ref[...], sa_ref[...], sb_ref[...], ROPE_C // 4)
    for hh in range(MLA_HB):
        res = _dot(ckv, w_ref[:, hh * 2 * LANE:(hh + 1) * 2 * LANE])
        k_nope = res[:, :LANE]
        ss = jnp.sum(k_nope * k_nope, axis=-1, keepdims=True) + ss_pe
        r = lax.rsqrt(ss * (1.0 / DQK_C) + EPS)
        k_ref[:, hh * HEAD_PAD:hh * HEAD_PAD + LANE] = (k_nope * r * g[:, :LANE]).astype(BF16)
        k_ref[:, hh * HEAD_PAD + LANE:(hh + 1) * HEAD_PAD] = (kr * r).astype(BF16)
        v_ref[:, hh * DV_C:(hh + 1) * DV_C] = res[:, LANE:].astype(BF16)


def _mla_kv(ckv, kpe_pad, tile0, rows, w_ukv, gk_pad, o, tabs):
    use_rope = tabs is not None
    in_specs = [
        pl.BlockSpec((TM, KV_LORA), lambda i, j: (tile0 + i, 0)),
        pl.BlockSpec((TM, LANE), lambda i, j: (tile0 + i, 0)),
        pl.BlockSpec((None, KV_LORA, MLA_HB * 2 * LANE), lambda i, j: (o, 0, j)),
        pl.BlockSpec((None, 1, HEAD_PAD), lambda i, j: (o, 0, 0)),
    ]
    args = [ckv, kpe_pad, w_ukv, gk_pad]
    if use_rope:
        in_specs += [pl.BlockSpec((TM, LANE), lambda i, j: (i % (DEC_SEQ // TM), 0))] * 3
        args += list(tabs)
    return pl.pallas_call(
        functools.partial(_mla_kv_kernel, use_rope=use_rope),
        out_shape=(jax.ShapeDtypeStruct((rows, H_C * HEAD_PAD), BF16),
                   jax.ShapeDtypeStruct((rows, H_C * DV_C), BF16)),
        grid=(rows // TM, H_C // MLA_HB),
        in_specs=in_specs,
        out_specs=(pl.BlockSpec((TM, MLA_HB * HEAD_PAD), lambda i, j: (i, j)),
                   pl.BlockSpec((TM, MLA_HB * DV_C), lambda i, j: (i, j))),
        compiler_params=_cparams(("parallel", "parallel")),
        name="mla_kv",
    )(*args)


def _mla_attn_kernel(q_ref, k_ref, v_ref, *rest, heads, has_ctx, sub):
    if has_ctx:
        ck_ref, cv_ref, o_ref, v1_ref, cv1_ref = rest
    else:
        o_ref, v1_ref = rest
    def with_ones(dst_ref, hh, v):
        dst_ref[hh, :, :DV_C] = v
        dst_ref[hh, :, DV_C:] = jnp.ones_like(v)
        return dst_ref[hh]

    for hh in range(heads):
        ks = slice(hh * HEAD_PAD, (hh + 1) * HEAD_PAD)
        vs = slice(hh * DV_C, (hh + 1) * DV_C)
        v1 = with_ones(v1_ref, hh, v_ref[:, vs])
        if has_ctx:
            cv1 = with_ones(cv1_ref, hh, cv_ref[:, vs])
        for r in range(q_ref.shape[0] // sub):
            rs = slice(r * sub, (r + 1) * sub)
            q = q_ref[rs, ks]
            s = _dot_nt(q, k_ref[:, ks])
            s2 = _dot_nt(q, ck_ref[:, ks]) if has_ctx else None
            e, e2, _ = _softmax_exp(s, s2, False)
            o = _dot(e.astype(BF16), v1)
            if has_ctx:
                o = o + _dot(e2.astype(BF16), cv1)
            o_ref[rs, vs] = (o[:, :DV_C] * (1.0 / o[:, DV_C:])).astype(BF16)


def _mla_attn_ctx(q, k, v):
    heads = 8
    return pl.pallas_call(
        functools.partial(_mla_attn_kernel, heads=heads, has_ctx=False, sub=SEQ),
        out_shape=jax.ShapeDtypeStruct((T_CTX, H_C * DV_C), BF16),
        grid=(BATCH, H_C // heads),
        in_specs=[
            pl.BlockSpec((SEQ, heads * HEAD_PAD), lambda b, j: (b, j)),
            pl.BlockSpec((SEQ, heads * HEAD_PAD), lambda b, j: (b, j)),
            pl.BlockSpec((SEQ, heads * DV_C), lambda b, j: (b, j)),
        ],
        out_specs=pl.BlockSpec((SEQ, heads * DV_C), lambda b, j: (b, j)),
        scratch_shapes=[pltpu.VMEM((heads, SEQ, 2 * DV_C), BF16)],
        compiler_params=_cparams(("parallel", "parallel")),
        name="mla_attn_ctx",
    )(q, k, v)


def _mla_attn_smp(q, k, v, ck, cv):
    tq, sub = 1024, 256
    nq = DEC_SEQ // tq
    q0, s0 = T_CTX // tq, T_CTX // DEC_SEQ
    return pl.pallas_call(
        functools.partial(_mla_attn_kernel, heads=1, has_ctx=True, sub=sub),
        out_shape=jax.ShapeDtypeStruct((T_SMP, H_C * DV_C), BF16),
        grid=(DEC_BATCH, H_C, nq),
        in_specs=[
            pl.BlockSpec((tq, HEAD_PAD), lambda b, h, i: (q0 + b * nq + i, h)),
            pl.BlockSpec((DEC_SEQ, HEAD_PAD), lambda b, h, i: (s0 + b, h)),
            pl.BlockSpec((DEC_SEQ, DV_C), lambda b, h, i: (s0 + b, h)),
            pl.BlockSpec((PAST_LEN, HEAD_PAD), lambda b, h, i: (b, h)),
            pl.BlockSpec((PAST_LEN, DV_C), lambda b, h, i: (b, h)),
        ],
        out_specs=pl.BlockSpec((tq, DV_C), lambda b, h, i: (b * nq + i, h)),
        scratch_shapes=[pltpu.VMEM((1, DEC_SEQ, 2 * DV_C), BF16), pltpu.VMEM((1, PAST_LEN, 2 * DV_C), BF16)],
        compiler_params=_cparams(("parallel", "parallel", "arbitrary")),
        name="mla_attn_smp",
    )(q, k, v, ck, cv)


def _axial_angles(rows, rot_dim):
    row = jnp.repeat(jnp.arange(rows, dtype=F32), GRID_W)
    col = jnp.tile(jnp.arange(GRID_W, dtype=F32), rows)
    half = rot_dim // 2
    inv_freq = ROPE_BASE ** (-jnp.arange(0, half, 2, dtype=F32) / half)
    ang_r = row[:, None] * inv_freq[None, :]
    ang_c = col[:, None] * inv_freq[None, :]
    return jnp.concatenate([ang_r, ang_r, ang_c, ang_c], axis=-1)


def _rope_tables(rows, rot_dim):
    ang = _axial_angles(rows, rot_dim)
    q = rot_dim // 4
    blk = (jnp.arange(rot_dim) // q) % 2
    cos, sin = jnp.cos(ang), jnp.sin(ang)
    sin_a = jnp.where(blk == 0, -sin, 0.0)
    sin_b = jnp.where(blk == 1, sin, 0.0)

    def full(t, fill):
        return jnp.pad(t, ((0, 0), (0, LANE - rot_dim)), constant_values=fill)

    return full(cos, 1.0), full(sin_a, 0.0), full(sin_b, 0.0)


def _joint_tables(tabs):
    def joint(t, fill):
        return jnp.concatenate([jnp.full((T_CTX, LANE), fill, F32), jnp.tile(t, (DEC_BATCH, 1))], axis=0)
    return joint(tabs[0], 1.0), joint(tabs[1], 0.0), joint(tabs[2], 0.0)


def kernel(x_prompt, x_sample, c, cache_diff_k, cache_diff_v, cache_mla_ckv, cache_mla_kpe, c_ctx, w_mod, b_mod, g_norm, w_ffn_gate, w_ffn_up, w_ffn_down, w_in_ab, w_out_ab, g_qk_diff, diff_lambda, g_diff_sub, w_in_mla, g_q_lora, w_uq, g_kv_lora, w_ukv, g_qk_mla, w_o_mla):
    rows = DEC_SEQ // GRID_W
    tabs_a = _rope_tables(rows, DH_A)
    tabs_c_joint = _joint_tables(_rope_tables(rows, ROPE_C))
    dft_ch = _dft_channel_mats()
    dft_ctx = tuple(m.astype(BF16) for m in _dft_mats(SEQ))
    dft_smp = tuple(m.astype(BF16) for m in _dft_mats(DEC_SEQ))
    p_ctx, p_smp = _Pass(False), _Pass(True)

    w_ffn = (w_ffn_gate[0, 0].astype(BF16), w_ffn_up[0, 0].astype(BF16), w_ffn_down[0, 0].astype(BF16))
    w_in_ab_b, w_out_ab_b = w_in_ab.astype(BF16), w_out_ab.astype(BF16)
    w_in_mla_b = jnp.pad(w_in_mla.astype(BF16), ((0, 0), (0, 0), (0, LANE - ROPE_C)))
    w_uq_b = jnp.pad(w_uq.astype(BF16).reshape(N_ODD, Q_LORA, H_C, DQK_C),
                     ((0, 0), (0, 0), (0, 0), (0, HEAD_PAD - DQK_C))).reshape(N_ODD, Q_LORA, H_C * HEAD_PAD)
    w_ukv_b, w_o_b = w_ukv.astype(BF16), w_o_mla.astype(BF16)
    g_qk_mla_pad = jnp.pad(g_qk_mla, ((0, 0), (0, 0), (0, HEAD_PAD - DQK_C)))[:, :, None, :]
    gq_mla, gk_mla = g_qk_mla_pad[:, 0], g_qk_mla_pad[:, 1]
    g_norm4 = g_norm[:, :, None, :]
    gqk_diff = g_qk_diff[:, :, None, :]
    g_sub3 = g_diff_sub[:, None, :]
    g_q_lora3, g_kv_lora3 = g_q_lora[:, None, :], g_kv_lora[:, None, :]

    cond8 = jnp.concatenate([c_ctx[None, :], c, jnp.zeros((8 - N_COND, D_MODEL), F32)], axis=0)
    mod = _modulation(cond8, w_mod, b_mod)[:, :N_COND].reshape(DEPTH, N_COND, N_MOD, D_MODEL)

    x_ctx, x_smp = x_prompt.reshape(T_CTX, D_MODEL), x_sample.reshape(T_SMP, D_MODEL)
    n_ctx, n_smp, n_all = T_CTX // FFN_TM, T_SMP // FFN_TM, T_ALL // FFN_TM
    w_stacks = (w_ffn_gate, w_ffn_up, w_ffn_down)
    new_kv = None
    mla_ckv, mla_kpe = [], []

    for l in range(DEPTH):
        mod_l = mod[l]
        if l == 0:
            h = _prenorm((x_ctx, x_smp), mod_l, g_norm4[l])
            xc, *w_next = _ffn(h, x_ctx, 0, 0, n_ctx, mod_l, w_ffn, 0, (*w_stacks, l, 1))
            (xs,) = _ffn(h, x_smp, 0, n_ctx, n_smp, mod_l, w_ffn, 0, None)
            w_ffn, xs_l = w_next, (xc, xs)
        else:
            h = _prenorm((x,), mod_l, g_norm4[l])
            x, *w_ffn = _ffn(h, x, 0, 0, n_all, mod_l, w_ffn, 0, (*w_stacks, l, 1))
            xs_l = (x, x)
        if l % 2 == 0:
            e = l // 2
            lam_init = 0.8 - 0.6 * math.exp(-0.3 * l)
            q_c, k_c, v_c, f_c, *new_kv = _inproj_even(p_ctx, xs_l[0], mod_l, g_norm4[l], w_in_ab_b, gqk_diff, e,
                                                      None, new_kv)
            q_s, k_s, v_s, f_s = _inproj_even(p_smp, xs_l[1], mod_l, g_norm4[l], w_in_ab_b, gqk_diff, e, tabs_a, None)
            o_ctx = _diff_attn_ctx(q_c, k_c, v_c, diff_lambda, g_sub3, e, lam_init)
            o_smp = _diff_attn_smp(q_s, k_s, v_s, cache_diff_k, cache_diff_v, diff_lambda, g_sub3, e, lam_init)
            fo_ctx = _fourier(f_c, SEQ, dft_ch, dft_ctx)
            fo_smp = _fourier(f_s, DEC_SEQ, dft_ch, dft_smp)
            x, h = _outproj(xs_l if l == 0 else (x,), mod_l, g_norm4[l],
                            [(o_ctx, o_smp, w_out_ab_b, e, 0), (fo_ctx, fo_smp, w_out_ab_b, e, W_A // W_B)])
        else:
            o = l // 2
            q, k, v, ckvn, kpe_pad = _mla_proj(x, mod_l, g_norm4[l], w_in_mla_b, w_uq_b, w_ukv_b, g_q_lora3,
                                               g_kv_lora3, gq_mla, gk_mla, o, tabs_c_joint)
            cache_ckv = cache_mla_ckv[:, o].reshape(DEC_BATCH * PAST_LEN, KV_LORA)
            cache_kpe = jnp.pad(cache_mla_kpe[:, o].reshape(DEC_BATCH * PAST_LEN, ROPE_C),
                                ((0, 0), (0, LANE - ROPE_C)))
            ck, cv = _mla_kv(cache_ckv, cache_kpe, 0, DEC_BATCH * PAST_LEN, w_ukv_b, gk_mla, o, None)
            a_ctx = _mla_attn_ctx(q, k, v)
            a_smp = _mla_attn_smp(q, k, v, ck, cv)
            x, h = _outproj((x,), mod_l, g_norm4[l], [(a_ctx, a_smp, w_o_b, o, 0)])
            mla_ckv.append(ckvn[:T_CTX].reshape(BATCH, SEQ, KV_LORA))
            mla_kpe.append(kpe_pad[:T_CTX, :ROPE_C].reshape(BATCH, SEQ, ROPE_C))
        if l + 1 < DEPTH:
            x, *w_ffn = _ffn(h, x, 0, 0, n_all, mod_l, w_ffn, 1, (*w_stacks, l + 1, 0))
        else:
            (y_ctx,) = _ffn(h, x, 0, 0, n_ctx, mod_l, w_ffn, 1, None)
            (y_smp,) = _ffn(h, x, n_ctx, n_ctx, n_smp, mod_l, w_ffn, 1, None)

    y_prompt = y_ctx.reshape(BATCH, SEQ, D_MODEL)
    y_sample = y_smp.reshape(DEC_BATCH, DEC_SEQ, D_MODEL)
    return (y_prompt, y_sample, new_kv[0], new_kv[1], jnp.stack(mla_ckv, axis=1), jnp.stack(mla_kpe, axis=1))
```

```python
import functools
import math

import jax
import jax.numpy as jnp
from jax import lax
from jax.experimental import pallas as pl
from jax.experimental.pallas import tpu as pltpu

F32 = jnp.float32
BF16 = jnp.bfloat16

D_MODEL = 2048
BATCH = 32
SEQ = 256
DEPTH = 4
DEC_BATCH = 2
DEC_SEQ = 2048
PAST_LEN = 256
GRID_W = 64
N_EVEN = 2
N_ODD = 2
H_A = 6
DH_A = 128
W_A = H_A * 2 * DH_A
G_B = 4
DG_B = 128
W_B = G_B * DG_B
H_C = 16
Q_LORA = 1536
KV_LORA = 512
NOPE_C = 128
ROPE_C = 64
DQK_C = NOPE_C + ROPE_C
DV_C = 128
D_FF = 5632
N_MOD = 9
ROPE_BASE = 10000.0
EPS = 1e-6
LOG2E = 1.4426950408889634

T_CTX = BATCH * SEQ
T_SMP = DEC_BATCH * DEC_SEQ
T_ALL = T_CTX + T_SMP
N_COND = 1 + DEC_BATCH
LANE = 128
HEAD_PAD = 256
TM = 512

VMEM_LIMIT = 52 * 1024 * 1024
FFN_TM = 1024
FFN_VMEM_LIMIT = 60 * 1024 * 1024


def _cparams(sem):
    return pltpu.CompilerParams(dimension_semantics=sem, vmem_limit_bytes=VMEM_LIMIT)


def _cond_row(i, tm=TM):
    n_ctx = T_CTX // tm
    per = DEC_SEQ // tm
    return jnp.where(i < n_ctx, 0, 1 + (i - n_ctx) // per)


class _Pass:
    def __init__(self, latent):
        self.latent = latent
        self.rows = T_SMP if latent else T_CTX
        self.tile0 = (T_CTX // TM) if latent else 0
        self.tiles = self.rows // TM
        self.seq = DEC_SEQ if latent else SEQ
        self.batch = DEC_BATCH if latent else BATCH

    def cond(self, i):
        return 1 + i // (DEC_SEQ // TM) if self.latent else 0


def _dot(a, b):
    return jnp.dot(a, b, preferred_element_type=F32)


def _dot_nt(a, b):
    return lax.dot_general(a, b, (((1,), (1,)), ((), ())), preferred_element_type=F32)


def _silu(x):
    return x / (1.0 + jnp.exp(-x))


def _rms(x, denom):
    return lax.rsqrt(jnp.sum(x * x, axis=-1, keepdims=True) * (1.0 / denom) + EPS)


def _norm_mod(x, g, mod_ref, base):
    sh = mod_ref[base:base + 1, :]
    sc = mod_ref[base + 1:base + 2, :]
    return (x * _rms(x, D_MODEL) * g) * (1.0 + sc) + sh


def _rope(y, cos, sin_a, sin_b, q):
    return y * cos + pltpu.roll(y, LANE - q, 1) * sin_a + pltpu.roll(y, q, 1) * sin_b


def _softmax_exp(s, s2, need_sum):
    m = jnp.max(s, axis=-1, keepdims=True)
    if s2 is not None:
        m = jnp.maximum(m, jnp.max(s2, axis=-1, keepdims=True))
    e = jnp.exp2(s - m)
    e2 = jnp.exp2(s2 - m) if s2 is not None else None
    if not need_sum:
        return e, e2, None
    l = jnp.sum(e, axis=-1, keepdims=True)
    if s2 is not None:
        l = l + jnp.sum(e2, axis=-1, keepdims=True)
    return e, e2, 1.0 / l


def _mod_kernel(c_ref, w_ref, b_ref, o_ref):
    a = _silu(c_ref[...]).astype(BF16)
    o_ref[...] = _dot(a, w_ref[...].astype(BF16)) + b_ref[...]


def _modulation(cond8, w_mod, b_mod):
    tn = 1024
    n = N_MOD * D_MODEL
    return pl.pallas_call(
        _mod_kernel,
        out_shape=jax.ShapeDtypeStruct((DEPTH, 8, n), F32),
        grid=(DEPTH, n // tn),
        in_specs=[
            pl.BlockSpec((8, D_MODEL), lambda l, j: (0, 0)),
            pl.BlockSpec((None, D_MODEL, tn), lambda l, j: (l, 0, j)),
            pl.BlockSpec((None, 1, tn), lambda l, j: (l, 0, j)),
        ],
        out_specs=pl.BlockSpec((None, 8, tn), lambda l, j: (l, 0, j)),
        compiler_params=_cparams(("parallel", "parallel")),
        name="modulation",
    )(cond8, w_mod, b_mod.reshape(DEPTH, 1, n))


FFN_TF = 512
FFN_XC = 256
FFN_DC = 512
BF16_ROWS = 16


def _prenorm_kernel(*refs, split):
    if split:
        xc_ref, xs_ref, mod_ref, g_ref, h_ref = refs
        x = jnp.where(pl.program_id(0) < T_CTX // TM, xc_ref[...], xs_ref[...])
    else:
        x_ref, mod_ref, g_ref, h_ref = refs
        x = x_ref[...]
    h_ref[...] = _norm_mod(x, g_ref[...], mod_ref, 0).astype(BF16)


def _prenorm(xs, mod_l, g):
    n_ctx = T_CTX // TM
    if len(xs) == 2:
        x_specs = [pl.BlockSpec((TM, D_MODEL), lambda i: (jnp.minimum(i, n_ctx - 1), 0)),
                   pl.BlockSpec((TM, D_MODEL), lambda i: (jnp.maximum(i - n_ctx, 0), 0))]
    else:
        x_specs = [pl.BlockSpec((TM, D_MODEL), lambda i: (i, 0))]
    return pl.pallas_call(
        functools.partial(_prenorm_kernel, split=len(xs) == 2),
        out_shape=jax.ShapeDtypeStruct((T_ALL, D_MODEL), BF16),
        grid=(T_ALL // TM,),
        in_specs=x_specs + [
            pl.BlockSpec((None, N_MOD, D_MODEL), lambda i: (_cond_row(i), 0, 0)),
            pl.BlockSpec((None, 1, D_MODEL), lambda i: (0, 0, 0)),
        ],
        out_specs=pl.BlockSpec((TM, D_MODEL), lambda i: (i, 0)),
        compiler_params=_cparams(("parallel",)),
        name="prenorm",
    )(*xs, mod_l, g)


def _cast_weight(src_ref, dst_ref, kind):
    w = src_ref[...]
    if kind == "plain":
        dst_ref[...] = w.astype(BF16)
    elif kind == "tail":
        n = w.shape[1]
        dst_ref[:, :n] = w.astype(BF16)
        dst_ref[:, n:] = jnp.zeros((w.shape[0], dst_ref.shape[1] - n), BF16)
    else:
        for hh in range(H_C):
            dst_ref[:, hh * HEAD_PAD:hh * HEAD_PAD + DQK_C] = w[:, hh * DQK_C:(hh + 1) * DQK_C].astype(BF16)
            dst_ref[:, hh * HEAD_PAD + DQK_C:(hh + 1) * HEAD_PAD] = jnp.zeros((w.shape[0], HEAD_PAD - DQK_C), BF16)


def _ffn_kernel(*refs, base, cast_kinds):
    it = iter(refs)
    h_ref, x_ref, mod_ref, wg_ref, wu_ref, wd_ref = [next(it) for _ in range(6)]
    srcs = [next(it) for _ in cast_kinds]
    o_ref = next(it)
    dsts = [next(it) for _ in cast_kinds]
    f = pl.program_id(1)

    @pl.when(f == 0)
    def _():
        o_ref[...] = jnp.zeros_like(o_ref)

    h = h_ref[...]
    gate = _dot(h, wg_ref[...])
    up = _dot(h, wu_ref[...])
    a = (_silu(gate) * up).astype(BF16)
    for c in range(D_MODEL // FFN_DC):
        cs = slice(c * FFN_DC, (c + 1) * FFN_DC)
        o_ref[:, cs] += (0.5 * mod_ref[base + 2:base + 3, cs]) * _dot(a, wd_ref[:, cs])

    for c in range(D_MODEL // FFN_XC):
        @pl.when(f == c)
        def _(c=c):
            o_ref[:, c * FFN_XC:(c + 1) * FFN_XC] += x_ref[...]

    for src, dst, kind in zip(srcs, dsts, cast_kinds):
        _cast_weight(src, dst, kind)


def _cast_rows(total, steps):
    rows = BF16_ROWS
    while total % rows or total // rows > steps:
        rows += BF16_ROWS
    return rows


def _ffn(h, x, x_tile0, tile0, tiles, mod_l, w_b, k, casts):
    tm, tf = FFN_TM, FFN_TF
    nf = D_FF // tf
    assert nf >= D_MODEL // FFN_XC
    wg, wu, wd = w_b
    in_specs = [
        pl.BlockSpec((tm, D_MODEL), lambda i, f: (tile0 + i, 0)),
        pl.BlockSpec((tm, FFN_XC), lambda i, f: (x_tile0 + i, jnp.minimum(f, D_MODEL // FFN_XC - 1))),
        pl.BlockSpec((None, N_MOD, D_MODEL), lambda i, f: (_cond_row(tile0 + i, tm), 0, 0)),
        pl.BlockSpec((D_MODEL, tf), lambda i, f: (0, f)),
        pl.BlockSpec((D_MODEL, tf), lambda i, f: (0, f)),
        pl.BlockSpec((tf, D_MODEL), lambda i, f: (f, 0)),
    ]
    args = [h, x, mod_l, wg, wu, wd]
    out_shape = [jax.ShapeDtypeStruct((tiles * tm, D_MODEL), F32)]
    out_specs = [pl.BlockSpec((tm, D_MODEL), lambda i, f: (i, 0))]
    for w, lead, kind in casts:
        n_rows, n_cols = w.shape[-2:]
        out_cols = {"plain": n_cols, "tail": -(-n_cols // LANE) * LANE, "heads": H_C * HEAD_PAD}[kind]
        rows = _cast_rows(n_rows, tiles * nf)
        last = n_rows // rows - 1
        in_specs.append(pl.BlockSpec((None,) * len(lead) + (rows, n_cols),
                                     lambda i, f, lead=lead, last=last: (*lead, jnp.minimum(i * nf + f, last), 0)))
        args.append(w)
        out_shape.append(jax.ShapeDtypeStruct((n_rows, out_cols), BF16))
        out_specs.append(pl.BlockSpec((rows, out_cols), lambda i, f, last=last: (jnp.minimum(i * nf + f, last), 0)))
    return pl.pallas_call(
        functools.partial(_ffn_kernel, base=6 * k, cast_kinds=tuple(kind for _, _, kind in casts)),
        out_shape=out_shape,
        grid=(tiles, nf),
        in_specs=in_specs,
        out_specs=out_specs,
        compiler_params=pltpu.CompilerParams(dimension_semantics=("arbitrary", "arbitrary"),
                                             vmem_limit_bytes=FFN_VMEM_LIMIT),
        name="ffn",
    )(*args)


IN_TM = SEQ


def _inproj_even_kernel(*refs, latent, e):
    x_ref, mod_ref, g_ref, w_ref, gq_ref, gk_ref = refs[:6]
    refs = refs[6:]
    if latent:
        cos_ref, sa_ref, sb_ref, q_ref, k_ref, v_ref, f_ref = refs
    else:
        q_ref, k_ref, v_ref, f_ref, nk_ref, nv_ref = refs[-6:]
        if e > 0:
            nk_ref[:e] = refs[0][...]
            nv_ref[:e] = refs[1][...]
        nk_ref, nv_ref = nk_ref.at[e], nv_ref.at[e]
    dh2 = 2 * DH_A
    h = _norm_mod(x_ref[...], g_ref[...], mod_ref, 3).astype(BF16)
    gains = (gq_ref[...] * (DH_A ** -0.5 * LOG2E), gk_ref[...])
    for part, (g, o_ref) in enumerate(zip(gains, (q_ref, k_ref))):
        for hh in range(H_A):
            res = _dot(h, w_ref[:, part * W_A + hh * dh2:part * W_A + (hh + 1) * dh2])
            for m in range(2):
                xc = res[:, m * DH_A:(m + 1) * DH_A]
                y = xc * _rms(xc, DH_A) * g
                if latent:
                    y = _rope(y, cos_ref[...], sa_ref[...], sb_ref[...], DH_A // 4)
                o_ref[:, hh * dh2 + m * DH_A:hh * dh2 + (m + 1) * DH_A] = y.astype(BF16)
                if part == 1 and not latent:
                    nk_ref[hh, :, m * DH_A:(m + 1) * DH_A] = y
    for hh in range(H_A):
        res = _dot(h, w_ref[:, 2 * W_A + hh * dh2:2 * W_A + (hh + 1) * dh2])
        v_ref[:, hh * dh2:(hh + 1) * dh2] = res.astype(BF16)
        if not latent:
            nv_ref[hh] = res
    f_ref[...] = _dot(h, w_ref[:, 3 * W_A:]).astype(BF16)


def _inproj_even(ps, x, mod_l, g_norm_l, w_in, gqk, e, tabs, prev_kv):
    tm = IN_TM
    t0 = ps.tile0 * (TM // tm)
    xt0 = t0 if x.shape[0] == T_ALL else 0
    in_specs = [
        pl.BlockSpec((tm, D_MODEL), lambda i: (xt0 + i, 0)),
        pl.BlockSpec((None, N_MOD, D_MODEL), lambda i: (_cond_row(t0 + i, tm), 0, 0)),
        pl.BlockSpec((None, 1, D_MODEL), lambda i: (1, 0, 0)),
        pl.BlockSpec((D_MODEL, 3 * W_A + W_B), lambda i: (0, 0), pipeline_mode=pl.Buffered(1)),
        pl.BlockSpec((None, None, 1, DH_A), lambda i: (e, 0, 0, 0)),
        pl.BlockSpec((None, None, 1, DH_A), lambda i: (e, 1, 0, 0)),
    ]
    args = [x, mod_l, g_norm_l, w_in, gqk, gqk]
    out_shape = [jax.ShapeDtypeStruct((ps.rows, W_A), BF16)] * 3 + [jax.ShapeDtypeStruct((ps.rows, W_B), BF16)]
    out_specs = [pl.BlockSpec((tm, W_A), lambda i: (i, 0))] * 3 + [pl.BlockSpec((tm, W_B), lambda i: (i, 0))]
    if ps.latent:
        tab_spec = pl.BlockSpec((tm, LANE), lambda i: (i % (DEC_SEQ // tm), 0))
        in_specs += [tab_spec] * 3
        args += list(tabs)
    else:
        kv_block = lambda n: pl.BlockSpec((None, n, H_A, SEQ, 2 * DH_A), lambda i: (i, 0, 0, 0, 0))
        out_shape += [jax.ShapeDtypeStruct((BATCH, e + 1, H_A, SEQ, 2 * DH_A), F32)] * 2
        out_specs += [kv_block(e + 1)] * 2
        if e > 0:
            in_specs += [kv_block(e)] * 2
            args += list(prev_kv)
    return pl.pallas_call(
        functools.partial(_inproj_even_kernel, latent=ps.latent, e=e),
        out_shape=out_shape,
        grid=(ps.rows // tm,),
        in_specs=in_specs,
        out_specs=out_specs,
        compiler_params=_cparams(("parallel",)),
        name="inproj_even",
    )(*args)


def _diff_attn_kernel(lam_ref, gsub_ref, q_ref, k_ref, v_ref, *rest, lam_init, heads, has_ctx, sub):
    if has_ctx:
        ck_ref, cv_ref, o_ref = rest
    else:
        (o_ref,) = rest
    lv = lam_ref[...]
    lam = (jnp.exp(jnp.sum(lv[0:1] * lv[1:2], axis=-1, keepdims=True))
           - jnp.exp(jnp.sum(lv[2:3] * lv[3:4], axis=-1, keepdims=True)) + lam_init)
    dh2 = 2 * DH_A
    gsub = gsub_ref[...] * (1.0 - lam_init)
    for hh in range(heads):
        hs = slice(hh * dh2, (hh + 1) * dh2)
        k = k_ref[:, hs]
        v = v_ref[:, hs]
        if has_ctx:
            ck = ck_ref[hh].astype(BF16)
            cv = cv_ref[hh].astype(BF16)
        for r in range(q_ref.shape[0] // sub):
            rs = slice(r * sub, (r + 1) * sub)
            q = q_ref[rs, hs]
            a, a2 = None, None
            for m in range(2):
                sl = slice(m * DH_A, (m + 1) * DH_A)
                s = _dot_nt(q[:, sl], k[:, sl])
                s2 = _dot_nt(q[:, sl], ck[:, sl]) if has_ctx else None
                e, e2, inv = _softmax_exp(s, s2, True)
                w = inv if m == 0 else -(lam * inv)
                a = e * w if a is None else a + e * w
                if has_ctx:
                    a2 = e2 * w if a2 is None else a2 + e2 * w
            o = _dot(a.astype(BF16), v)
            if has_ctx:
                o = o + _dot(a2.astype(BF16), cv)
            o_ref[rs, hs] = (o * _rms(o, dh2) * gsub).astype(BF16)


def _diff_attn_ctx(q, k, v, lam_vecs, gsub, e, lam_init):
    spec = pl.BlockSpec((SEQ, W_A), lambda b: (b, 0))
    return pl.pallas_call(
        functools.partial(_diff_attn_kernel, lam_init=lam_init, heads=H_A, has_ctx=False, sub=SEQ),
        out_shape=jax.ShapeDtypeStruct((T_CTX, W_A), BF16),
        grid=(BATCH,),
        in_specs=[
            pl.BlockSpec((None, 4, DH_A), lambda b: (e, 0, 0)),
            pl.BlockSpec((None, 1, 2 * DH_A), lambda b: (e, 0, 0)),
            spec, spec, spec,
        ],
        out_specs=spec,
        compiler_params=_cparams(("parallel",)),
        name="diff_attn_ctx",
    )(lam_vecs, gsub, q, k, v)


def _diff_attn_smp(q, k, v, cache_k, cache_v, lam_vecs, gsub, e, lam_init):
    tq, sub = 1024, 256
    w = 2 * DH_A
    nq = DEC_SEQ // tq
    cache_spec = pl.BlockSpec((None, None, 1, PAST_LEN, w), lambda b, h, i: (b, e, h, 0, 0))
    return pl.pallas_call(
        functools.partial(_diff_attn_kernel, lam_init=lam_init, heads=1, has_ctx=True, sub=sub),
        out_shape=jax.ShapeDtypeStruct((T_SMP, W_A), BF16),
        grid=(DEC_BATCH, H_A, nq),
        in_specs=[
            pl.BlockSpec((None, 4, DH_A), lambda b, h, i: (e, 0, 0)),
            pl.BlockSpec((None, 1, w), lambda b, h, i: (e, 0, 0)),
            pl.BlockSpec((tq, w), lambda b, h, i: (b * nq + i, h)),
            pl.BlockSpec((DEC_SEQ, w), lambda b, h, i: (b, h)),
            pl.BlockSpec((DEC_SEQ, w), lambda b, h, i: (b, h)),
            cache_spec, cache_spec,
        ],
        out_specs=pl.BlockSpec((tq, w), lambda b, h, i: (b * nq + i, h)),
        compiler_params=_cparams(("parallel", "parallel", "arbitrary")),
        name="diff_attn_smp",
    )(lam_vecs, gsub, q, k, v, cache_k, cache_v)


def _fourier_kernel(x_ref, bc_ref, bs_ref, cs_ref, ss_ref, o_ref, y1_ref, y2_ref, *, scale):
    @pl.when(pl.program_id(1) == 0)
    def _():
        x = x_ref[...]
        y1_ref[...] = _dot(x, bc_ref[...]).astype(BF16)
        y2_ref[...] = _dot(x, bs_ref[...]).astype(BF16)

    out = _dot(cs_ref[...], y1_ref[...]) - _dot(ss_ref[...], y2_ref[...])
    o_ref[...] = (out * scale).astype(BF16)


def _fourier(f, s, dft_ch, dft_seq):
    tr = 256
    nb = f.shape[0] // s
    bc, bs = dft_ch
    cs, ss = dft_seq
    return pl.pallas_call(
        functools.partial(_fourier_kernel, scale=1.0 / math.sqrt(s * DG_B)),
        out_shape=jax.ShapeDtypeStruct(f.shape, BF16),
        grid=(nb, s // tr),
        in_specs=[
            pl.BlockSpec((s, W_B), lambda b, r: (b, 0)),
            pl.BlockSpec((W_B, W_B), lambda b, r: (0, 0)),
            pl.BlockSpec((W_B, W_B), lambda b, r: (0, 0)),
            pl.BlockSpec((tr, s), lambda b, r: (r, 0)),
            pl.BlockSpec((tr, s), lambda b, r: (r, 0)),
        ],
        out_specs=pl.BlockSpec((tr, W_B), lambda b, r: (b * (s // tr) + r, 0)),
        scratch_shapes=[pltpu.VMEM((s, W_B), BF16), pltpu.VMEM((s, W_B), BF16)],
        compiler_params=_cparams(("parallel", "arbitrary")),
        name="fourier",
    )(f, bc, bs, cs, ss)


def _dft_angles(rows, stride, n):
    j = jnp.arange(rows, dtype=jnp.int32)[:, None] * stride
    k = jnp.arange(n, dtype=jnp.int32)[None, :]
    return ((j * k) % n).astype(F32) * (2.0 * math.pi / n)


def _dft_mats(n):
    lo = min(n, 64)
    a0 = _dft_angles(lo, 1, n)
    c0, s0 = jnp.cos(a0), jnp.sin(a0)
    if lo == n:
        return c0, s0
    a1 = _dft_angles(n // lo, lo, n)
    c1, s1 = jnp.cos(a1)[:, None, :], jnp.sin(a1)[:, None, :]
    c = c1 * c0[None] - s1 * s0[None]
    s = s1 * c0[None] + c1 * s0[None]
    return c.reshape(n, n), s.reshape(n, n)


def _dft_channel_mats():
    c, s = _dft_mats(DG_B)
    eye = jnp.eye(G_B, dtype=F32)
    return jnp.kron(eye, c).astype(BF16), jnp.kron(eye, s).astype(BF16)


def _outproj_kernel(*refs, n_act, n_x):
    is_ctx = pl.program_id(0) < T_CTX // TM
    x = refs[0][...] if n_x == 1 else jnp.where(is_ctx, refs[0][...], refs[1][...])
    mod_ref, g_ref = refs[n_x:n_x + 2]
    act = refs[n_x + 2:n_x + 2 + 3 * n_act]
    o_ref, hn_ref = refs[n_x + 2 + 3 * n_act:]
    y = None
    for a in range(n_act):
        ctx_ref, smp_ref, w_ref = act[3 * a:3 * a + 3]
        lhs = jnp.where(is_ctx, ctx_ref[...], smp_ref[...])
        t = _dot(lhs, w_ref[...])
        y = t if y is None else y + t
    x_new = x + mod_ref[5:6, :] * y
    o_ref[...] = x_new
    hn_ref[...] = _norm_mod(x_new, g_ref[...], mod_ref, 6).astype(BF16)


def _outproj(xs, mod_l, g_norm_l, acts):
    n_ctx = T_CTX // TM
    row_spec = pl.BlockSpec((TM, D_MODEL), lambda i: (i, 0))
    ctx_rows = lambda n: pl.BlockSpec((TM, n), lambda i: (jnp.minimum(i, n_ctx - 1), 0))
    smp_rows = lambda n: pl.BlockSpec((TM, n), lambda i: (jnp.maximum(i - n_ctx, 0), 0))
    in_specs = ([row_spec] if len(xs) == 1 else [ctx_rows(D_MODEL), smp_rows(D_MODEL)]) + [
        pl.BlockSpec((None, N_MOD, D_MODEL), lambda i: (_cond_row(i), 0, 0)),
        pl.BlockSpec((None, 1, D_MODEL), lambda i: (2, 0, 0)),
    ]
    args = [*xs, mod_l, g_norm_l]
    for ctx, smp, w, rb in acts:
        kdim = ctx.shape[1]
        in_specs += [ctx_rows(kdim), smp_rows(kdim), pl.BlockSpec((kdim, D_MODEL), lambda i, rb=rb: (rb, 0))]
        args += [ctx, smp, w]
    return pl.pallas_call(
        functools.partial(_outproj_kernel, n_act=len(acts), n_x=len(xs)),
        out_shape=(jax.ShapeDtypeStruct((T_ALL, D_MODEL), F32), jax.ShapeDtypeStruct((T_ALL, D_MODEL), BF16)),
        grid=(T_ALL // TM,),
        in_specs=in_specs,
        out_specs=(row_spec, row_spec),
        compiler_params=_cparams(("parallel",)),
        name="outproj",
    )(*args)


MLA_TM = 256


def _mla_proj_kernel(x_ref, mod_ref, g_ref, win_ref, wuq_ref, wukv_ref, gql_ref, gkvl_ref, gq_ref, gk_ref,
                     cos_ref, sa_ref, sb_ref, q_ref, k_ref, v_ref, ckv_ref, kpe_ref):
    h = _norm_mod(x_ref[...], g_ref[...], mod_ref, 3).astype(BF16)
    res = _dot(h, win_ref[...])
    cq = res[:, :Q_LORA]
    cqn = (cq * _rms(cq, Q_LORA) * gql_ref[...]).astype(BF16)
    ckv = res[:, Q_LORA:Q_LORA + KV_LORA]
    ckvn = ckv * _rms(ckv, KV_LORA) * gkvl_ref[...]
    ckv_ref[...] = ckvn
    kpe = res[:, Q_LORA + KV_LORA:]
    kpe_ref[...] = kpe
    cos, sa, sb = cos_ref[...], sa_ref[...], sb_ref[...]
    gq, gk = gq_ref[...] * (DQK_C ** -0.5 * LOG2E), gk_ref[...]
    ss_pe = jnp.sum(kpe * kpe, axis=-1, keepdims=True)
    kr = _rope(kpe * gk[:, LANE:], cos, sa, sb, ROPE_C // 4)
    ckvb = ckvn.astype(BF16)
    for hh in range(H_C):
        hs = slice(hh * HEAD_PAD, (hh + 1) * HEAD_PAD)
        lo = slice(hh * HEAD_PAD, hh * HEAD_PAD + LANE)
        hi = slice(hh * HEAD_PAD + LANE, (hh + 1) * HEAD_PAD)
        xh = _dot(cqn, wuq_ref[:, hs])
        y = xh * _rms(xh, DQK_C) * gq
        q_ref[:, lo] = y[:, :LANE].astype(BF16)
        q_ref[:, hi] = _rope(y[:, LANE:], cos, sa, sb, ROPE_C // 4).astype(BF16)
        kv = _dot(ckvb, wukv_ref[:, hs])
        k_nope = kv[:, :LANE]
        ss = jnp.sum(k_nope * k_nope, axis=-1, keepdims=True) + ss_pe
        r = lax.rsqrt(ss * (1.0 / DQK_C) + EPS)
        k_ref[:, lo] = (k_nope * r * gk[:, :LANE]).astype(BF16)
        k_ref[:, hi] = (kr * r).astype(BF16)
        v_ref[:, hh * DV_C:(hh + 1) * DV_C] = kv[:, LANE:].astype(BF16)


def _mla_proj(x, mod_l, g_norm_l, w_in_pad, w_uq_pad, w_ukv, gq_lora, gkv_lora, gq_pad, gk_pad, o, tabs):
    tm = MLA_TM
    n_in = Q_LORA + KV_LORA + LANE
    resident = dict(pipeline_mode=pl.Buffered(1))
    tab_spec = pl.BlockSpec((tm, LANE), lambda i: (i, 0))
    row = lambda n: pl.BlockSpec((tm, n), lambda i: (i, 0))
    return pl.pallas_call(
        _mla_proj_kernel,
        out_shape=(jax.ShapeDtypeStruct((T_ALL, H_C * HEAD_PAD), BF16),
                   jax.ShapeDtypeStruct((T_ALL, H_C * HEAD_PAD), BF16),
                   jax.ShapeDtypeStruct((T_ALL, H_C * DV_C), BF16),
                   jax.ShapeDtypeStruct((T_ALL, KV_LORA), F32),
                   jax.ShapeDtypeStruct((T_ALL, LANE), F32)),
        grid=(T_ALL // tm,),
        in_specs=[
            row(D_MODEL),
            pl.BlockSpec((None, N_MOD, D_MODEL), lambda i: (_cond_row(i, tm), 0, 0)),
            pl.BlockSpec((None, 1, D_MODEL), lambda i: (1, 0, 0)),
            pl.BlockSpec((D_MODEL, n_in), lambda i: (0, 0), **resident),
            pl.BlockSpec((Q_LORA, H_C * HEAD_PAD), lambda i: (0, 0), **resident),
            pl.BlockSpec((KV_LORA, H_C * HEAD_PAD), lambda i: (0, 0), **resident),
            pl.BlockSpec((None, 1, Q_LORA), lambda i: (o, 0, 0)),
            pl.BlockSpec((None, 1, KV_LORA), lambda i: (o, 0, 0)),
            pl.BlockSpec((None, 1, HEAD_PAD), lambda i: (o, 0, 0)),
            pl.BlockSpec((None, 1, HEAD_PAD), lambda i: (o, 0, 0)),
            tab_spec, tab_spec, tab_spec,
        ],
        out_specs=(row(H_C * HEAD_PAD), row(H_C * HEAD_PAD), row(H_C * DV_C), row(KV_LORA), row(LANE)),
        compiler_params=_cparams(("parallel",)),
        name="mla_proj",
    )(x, mod_l, g_norm_l, w_in_pad, w_uq_pad, w_ukv, gq_lora, gkv_lora, gq_pad, gk_pad, *tabs)


MLA_HB = 4


def _mla_kv_kernel(ckv_ref, kpe_ref, w_ref, g_ref, *rest, use_rope):
    if use_rope:
        cos_ref, sa_ref, sb_ref, k_ref, v_ref = rest
    else:
        k_ref, v_ref = rest
    ckv = ckv_ref[...].astype(BF16)
    g = g_ref[...]
    kpe = kpe_ref[...]
    ss_pe = jnp.sum(kpe * kpe, axis=-1, keepdims=True)
    kr = kpe * g[:, LANE:]
    if use_rope:
        kr = _rope(kr, cos_ref[...], sa_ref[...], sb_ref[...], ROPE_C // 4)
    for hh in range(MLA_HB):
        res = _dot(ckv, w_ref[:, hh * 2 * LANE:(hh + 1) * 2 * LANE])
        k_nope = res[:, :LANE]
        ss = jnp.sum(k_nope * k_nope, axis=-1, keepdims=True) + ss_pe
        r = lax.rsqrt(ss * (1.0 / DQK_C) + EPS)
        k_ref[:, hh * HEAD_PAD:hh * HEAD_PAD + LANE] = (k_nope * r * g[:, :LANE]).astype(BF16)
        k_ref[:, hh * HEAD_PAD + LANE:(hh + 1) * HEAD_PAD] = (kr * r).astype(BF16)
        v_ref[:, hh * DV_C:(hh + 1) * DV_C] = res[:, LANE:].astype(BF16)


def _mla_kv(ckv, kpe_pad, tile0, rows, w_ukv, gk_pad, o, tabs):
    use_rope = tabs is not None
    in_specs = [
        pl.BlockSpec((TM, KV_LORA), lambda i, j: (tile0 + i, 0)),
        pl.BlockSpec((TM, LANE), lambda i, j: (tile0 + i, 0)),
        pl.BlockSpec((KV_LORA, MLA_HB * 2 * LANE), lambda i, j: (0, j)),
        pl.BlockSpec((None, 1, HEAD_PAD), lambda i, j: (o, 0, 0)),
    ]
    args = [ckv, kpe_pad, w_ukv, gk_pad]
    if use_rope:
        in_specs += [pl.BlockSpec((TM, LANE), lambda i, j: (i % (DEC_SEQ // TM), 0))] * 3
        args += list(tabs)
    return pl.pallas_call(
        functools.partial(_mla_kv_kernel, use_rope=use_rope),
        out_shape=(jax.ShapeDtypeStruct((rows, H_C * HEAD_PAD), BF16),
                   jax.ShapeDtypeStruct((rows, H_C * DV_C), BF16)),
        grid=(rows // TM, H_C // MLA_HB),
        in_specs=in_specs,
        out_specs=(pl.BlockSpec((TM, MLA_HB * HEAD_PAD), lambda i, j: (i, j)),
                   pl.BlockSpec((TM, MLA_HB * DV_C), lambda i, j: (i, j))),
        compiler_params=_cparams(("parallel", "parallel")),
        name="mla_kv",
    )(*args)


def _mla_attn_kernel(q_ref, k_ref, v_ref, *rest, heads, has_ctx, sub):
    if has_ctx:
        ck_ref, cv_ref, o_ref, v1_ref, cv1_ref = rest
    else:
        o_ref, v1_ref = rest
    def with_ones(dst_ref, hh, v):
        dst_ref[hh, :, :DV_C] = v
        dst_ref[hh, :, DV_C:] = jnp.ones_like(v)
        return dst_ref[hh]

    for hh in range(heads):
        ks = slice(hh * HEAD_PAD, (hh + 1) * HEAD_PAD)
        vs = slice(hh * DV_C, (hh + 1) * DV_C)
        v1 = with_ones(v1_ref, hh, v_ref[:, vs])
        if has_ctx:
            cv1 = with_ones(cv1_ref, hh, cv_ref[:, vs])
        for r in range(q_ref.shape[0] // sub):
            rs = slice(r * sub, (r + 1) * sub)
            q = q_ref[rs, ks]
            s = _dot_nt(q, k_ref[:, ks])
            s2 = _dot_nt(q, ck_ref[:, ks]) if has_ctx else None
            e, e2, _ = _softmax_exp(s, s2, False)
            o = _dot(e.astype(BF16), v1)
            if has_ctx:
                o = o + _dot(e2.astype(BF16), cv1)
            o_ref[rs, vs] = (o[:, :DV_C] * (1.0 / o[:, DV_C:])).astype(BF16)


def _mla_attn_ctx(q, k, v):
    heads = 8
    return pl.pallas_call(
        functools.partial(_mla_attn_kernel, heads=heads, has_ctx=False, sub=SEQ),
        out_shape=jax.ShapeDtypeStruct((T_CTX, H_C * DV_C), BF16),
        grid=(BATCH, H_C // heads),
        in_specs=[
            pl.BlockSpec((SEQ, heads * HEAD_PAD), lambda b, j: (b, j)),
            pl.BlockSpec((SEQ, heads * HEAD_PAD), lambda b, j: (b, j)),
            pl.BlockSpec((SEQ, heads * DV_C), lambda b, j: (b, j)),
        ],
        out_specs=pl.BlockSpec((SEQ, heads * DV_C), lambda b, j: (b, j)),
        scratch_shapes=[pltpu.VMEM((heads, SEQ, 2 * DV_C), BF16)],
        compiler_params=_cparams(("parallel", "parallel")),
        name="mla_attn_ctx",
    )(q, k, v)


def _mla_attn_smp(q, k, v, ck, cv):
    tq, sub = 1024, 256
    nq = DEC_SEQ // tq
    q0, s0 = T_CTX // tq, T_CTX // DEC_SEQ
    return pl.pallas_call(
        functools.partial(_mla_attn_kernel, heads=1, has_ctx=True, sub=sub),
        out_shape=jax.ShapeDtypeStruct((T_SMP, H_C * DV_C), BF16),
        grid=(DEC_BATCH, H_C, nq),
        in_specs=[
            pl.BlockSpec((tq, HEAD_PAD), lambda b, h, i: (q0 + b * nq + i, h)),
            pl.BlockSpec((DEC_SEQ, HEAD_PAD), lambda b, h, i: (s0 + b, h)),
            pl.BlockSpec((DEC_SEQ, DV_C), lambda b, h, i: (s0 + b, h)),
            pl.BlockSpec((PAST_LEN, HEAD_PAD), lambda b, h, i: (b, h)),
            pl.BlockSpec((PAST_LEN, DV_C), lambda b, h, i: (b, h)),
        ],
        out_specs=pl.BlockSpec((tq, DV_C), lambda b, h, i: (b * nq + i, h)),
        scratch_shapes=[pltpu.VMEM((1, DEC_SEQ, 2 * DV_C), BF16), pltpu.VMEM((1, PAST_LEN, 2 * DV_C), BF16)],
        compiler_params=_cparams(("parallel", "parallel", "arbitrary")),
        name="mla_attn_smp",
    )(q, k, v, ck, cv)


def _axial_angles(rows, rot_dim):
    row = jnp.repeat(jnp.arange(rows, dtype=F32), GRID_W)
    col = jnp.tile(jnp.arange(GRID_W, dtype=F32), rows)
    half = rot_dim // 2
    inv_freq = ROPE_BASE ** (-jnp.arange(0, half, 2, dtype=F32) / half)
    ang_r = row[:, None] * inv_freq[None, :]
    ang_c = col[:, None] * inv_freq[None, :]
    return jnp.concatenate([ang_r, ang_r, ang_c, ang_c], axis=-1)


def _rope_tables(rows, rot_dim):
    ang = _axial_angles(rows, rot_dim)
    q = rot_dim // 4
    blk = (jnp.arange(rot_dim) // q) % 2
    cos, sin = jnp.cos(ang), jnp.sin(ang)
    sin_a = jnp.where(blk == 0, -sin, 0.0)
    sin_b = jnp.where(blk == 1, sin, 0.0)

    def full(t, fill):
        return jnp.pad(t, ((0, 0), (0, LANE - rot_dim)), constant_values=fill)

    return full(cos, 1.0), full(sin_a, 0.0), full(sin_b, 0.0)


def _joint_tables(tabs):
    def joint(t, fill):
        return jnp.concatenate([jnp.full((T_CTX, LANE), fill, F32), jnp.tile(t, (DEC_BATCH, 1))], axis=0)
    return joint(tabs[0], 1.0), joint(tabs[1], 0.0), joint(tabs[2], 0.0)


def kernel(x_prompt, x_sample, c, cache_diff_k, cache_diff_v, cache_mla_ckv, cache_mla_kpe, c_ctx, w_mod, b_mod, g_norm, w_ffn_gate, w_ffn_up, w_ffn_down, w_in_ab, w_out_ab, g_qk_diff, diff_lambda, g_diff_sub, w_in_mla, g_q_lora, w_uq, g_kv_lora, w_ukv, g_qk_mla, w_o_mla):
    rows = DEC_SEQ // GRID_W
    tabs_a = _rope_tables(rows, DH_A)
    tabs_c_joint = _joint_tables(_rope_tables(rows, ROPE_C))
    dft_ch = _dft_channel_mats()
    dft_ctx = tuple(m.astype(BF16) for m in _dft_mats(SEQ))
    dft_smp = tuple(m.astype(BF16) for m in _dft_mats(DEC_SEQ))
    p_ctx, p_smp = _Pass(False), _Pass(True)

    w_ffn = (w_ffn_gate[0, 0].astype(BF16), w_ffn_up[0, 0].astype(BF16), w_ffn_down[0, 0].astype(BF16))

    def ffn_casts(l, k):
        return [(w_ffn_gate, (l, k), "plain"), (w_ffn_up, (l, k), "plain"), (w_ffn_down, (l, k), "plain")]

    def mixer_casts(l):
        j = l // 2
        if l % 2 == 0:
            return [(w_in_ab, (j,), "plain"), (w_out_ab, (j,), "plain")]
        return [(w_in_mla, (j,), "tail"), (w_uq, (j,), "heads"), (w_ukv, (j,), "plain"), (w_o_mla, (j,), "plain")]

    g_qk_mla_pad = jnp.pad(g_qk_mla, ((0, 0), (0, 0), (0, HEAD_PAD - DQK_C)))[:, :, None, :]
    gq_mla, gk_mla = g_qk_mla_pad[:, 0], g_qk_mla_pad[:, 1]
    g_norm4 = g_norm[:, :, None, :]
    gqk_diff = g_qk_diff[:, :, None, :]
    g_sub3 = g_diff_sub[:, None, :]
    g_q_lora3, g_kv_lora3 = g_q_lora[:, None, :], g_kv_lora[:, None, :]

    cond8 = jnp.concatenate([c_ctx[None, :], c, jnp.zeros((8 - N_COND, D_MODEL), F32)], axis=0)
    mod = _modulation(cond8, w_mod, b_mod)[:, :N_COND].reshape(DEPTH, N_COND, N_MOD, D_MODEL)

    x_ctx, x_smp = x_prompt.reshape(T_CTX, D_MODEL), x_sample.reshape(T_SMP, D_MODEL)
    n_ctx, n_smp, n_all = T_CTX // FFN_TM, T_SMP // FFN_TM, T_ALL // FFN_TM
    new_kv = None
    mla_ckv, mla_kpe = [], []

    for l in range(DEPTH):
        mod_l = mod[l]
        if l == 0:
            h = _prenorm((x_ctx, x_smp), mod_l, g_norm4[l])
            xc, *cast = _ffn(h, x_ctx, 0, 0, n_ctx, mod_l, w_ffn, 0, ffn_casts(l, 1) + mixer_casts(l)[:1])
            xs, w_out0 = _ffn(h, x_smp, 0, n_ctx, n_smp, mod_l, w_ffn, 0, mixer_casts(l)[1:])
            w_ffn, w_mix, xs_l = cast[:3], [cast[3], w_out0], (xc, xs)
        else:
            h = _prenorm((x,), mod_l, g_norm4[l])
            x, *w_ffn = _ffn(h, x, 0, 0, n_all, mod_l, w_ffn, 0, ffn_casts(l, 1))
            xs_l = (x, x)
        if l % 2 == 0:
            e = l // 2
            lam_init = 0.8 - 0.6 * math.exp(-0.3 * l)
            w_in_b, w_out_b = w_mix
            q_c, k_c, v_c, f_c, *new_kv = _inproj_even(p_ctx, xs_l[0], mod_l, g_norm4[l], w_in_b, gqk_diff, e,
                                                      None, new_kv)
            q_s, k_s, v_s, f_s = _inproj_even(p_smp, xs_l[1], mod_l, g_norm4[l], w_in_b, gqk_diff, e, tabs_a, None)
            o_ctx = _diff_attn_ctx(q_c, k_c, v_c, diff_lambda, g_sub3, e, lam_init)
            o_smp = _diff_attn_smp(q_s, k_s, v_s, cache_diff_k, cache_diff_v, diff_lambda, g_sub3, e, lam_init)
            fo_ctx = _fourier(f_c, SEQ, dft_ch, dft_ctx)
            fo_smp = _fourier(f_s, DEC_SEQ, dft_ch, dft_smp)
            x, h = _outproj(xs_l if l == 0 else (x,), mod_l, g_norm4[l],
                            [(o_ctx, o_smp, w_out_b, 0), (fo_ctx, fo_smp, w_out_b, W_A // W_B)])
        else:
            o = l // 2
            w_in_mla_b, w_uq_b, w_ukv_b, w_o_b = w_mix
            q, k, v, ckvn, kpe_pad = _mla_proj(x, mod_l, g_norm4[l], w_in_mla_b, w_uq_b, w_ukv_b, g_q_lora3,
                                               g_kv_lora3, gq_mla, gk_mla, o, tabs_c_joint)
            cache_ckv = cache_mla_ckv[:, o].reshape(DEC_BATCH * PAST_LEN, KV_LORA)
            cache_kpe = jnp.pad(cache_mla_kpe[:, o].reshape(DEC_BATCH * PAST_LEN, ROPE_C),
                                ((0, 0), (0, LANE - ROPE_C)))
            ck, cv = _mla_kv(cache_ckv, cache_kpe, 0, DEC_BATCH * PAST_LEN, w_ukv_b, gk_mla, o, None)
            a_ctx = _mla_attn_ctx(q, k, v)
            a_smp = _mla_attn_smp(q, k, v, ck, cv)
            x, h = _outproj((x,), mod_l, g_norm4[l], [(a_ctx, a_smp, w_o_b, 0)])
            mla_ckv.append(ckvn[:T_CTX].reshape(BATCH, SEQ, KV_LORA))
            mla_kpe.append(kpe_pad[:T_CTX, :ROPE_C].reshape(BATCH, SEQ, ROPE_C))
        if l + 1 < DEPTH:
            x, *cast = _ffn(h, x, 0, 0, n_all, mod_l, w_ffn, 1, ffn_casts(l + 1, 0) + mixer_casts(l + 1))
            w_ffn, w_mix = cast[:3], cast[3:]
        else:
            (y_ctx,) = _ffn(h, x, 0, 0, n_ctx, mod_l, w_ffn, 1, [])
            (y_smp,) = _ffn(h, x, n_ctx, n_ctx, n_smp, mod_l, w_ffn, 1, [])

    y_prompt = y_ctx.reshape(BATCH, SEQ, D_MODEL)
    y_sample = y_smp.reshape(DEC_BATCH, DEC_SEQ, D_MODEL)
    return (y_prompt, y_sample, new_kv[0], new_kv[1], jnp.stack(mla_ckv, axis=1), jnp.stack(mla_kpe, axis=1))
```

```python
import functools
import math

import jax
import jax.numpy as jnp
from jax import lax
from jax.experimental import pallas as pl
from jax.experimental.pallas import tpu as pltpu

F32 = jnp.float32
BF16 = jnp.bfloat16

D_MODEL = 2048
BATCH = 32
SEQ = 256
DEPTH = 4
DEC_BATCH = 2
DEC_SEQ = 2048
PAST_LEN = 256
GRID_W = 64
N_EVEN = 2
N_ODD = 2
H_A = 6
DH_A = 128
W_A = H_A * 2 * DH_A
G_B = 4
DG_B = 128
W_B = G_B * DG_B
H_C = 16
Q_LORA = 1536
KV_LORA = 512
NOPE_C = 128
ROPE_C = 64
DQK_C = NOPE_C + ROPE_C
DV_C = 128
D_FF = 5632
N_MOD = 9
ROPE_BASE = 10000.0
EPS = 1e-6
LOG2E = 1.4426950408889634

T_CTX = BATCH * SEQ
T_SMP = DEC_BATCH * DEC_SEQ
T_ALL = T_CTX + T_SMP
N_COND = 1 + DEC_BATCH
LANE = 128
HEAD_PAD = 256
TM = 512

VMEM_LIMIT = 52 * 1024 * 1024
FFN_TM = 1024
FFN_VMEM_LIMIT = 60 * 1024 * 1024


def _cparams(sem):
    return pltpu.CompilerParams(dimension_semantics=sem, vmem_limit_bytes=VMEM_LIMIT)


def _cond_row(i, tm=TM):
    n_ctx = T_CTX // tm
    per = DEC_SEQ // tm
    return jnp.where(i < n_ctx, 0, 1 + (i - n_ctx) // per)


class _Pass:
    def __init__(self, latent):
        self.latent = latent
        self.rows = T_SMP if latent else T_CTX
        self.tile0 = (T_CTX // TM) if latent else 0
        self.tiles = self.rows // TM
        self.seq = DEC_SEQ if latent else SEQ
        self.batch = DEC_BATCH if latent else BATCH

    def cond(self, i):
        return 1 + i // (DEC_SEQ // TM) if self.latent else 0


def _dot(a, b):
    return jnp.dot(a, b, preferred_element_type=F32)


def _dot_nt(a, b):
    return lax.dot_general(a, b, (((1,), (1,)), ((), ())), preferred_element_type=F32)


def _silu(x):
    return x / (1.0 + jnp.exp(-x))


def _rms(x, denom):
    return lax.rsqrt(jnp.sum(x * x, axis=-1, keepdims=True) * (1.0 / denom) + EPS)


def _norm_mod(x, g, mod_ref, base):
    sh = mod_ref[base:base + 1, :]
    sc = mod_ref[base + 1:base + 2, :]
    return (x * _rms(x, D_MODEL) * g) * (1.0 + sc) + sh


def _rope(y, cos, sin_a, sin_b, q):
    return y * cos + pltpu.roll(y, LANE - q, 1) * sin_a + pltpu.roll(y, q, 1) * sin_b


def _softmax_exp(s, s2, need_sum):
    m = jnp.max(s, axis=-1, keepdims=True)
    if s2 is not None:
        m = jnp.maximum(m, jnp.max(s2, axis=-1, keepdims=True))
    e = jnp.exp2(s - m)
    e2 = jnp.exp2(s2 - m) if s2 is not None else None
    if not need_sum:
        return e, e2, None
    l = jnp.sum(e, axis=-1, keepdims=True)
    if s2 is not None:
        l = l + jnp.sum(e2, axis=-1, keepdims=True)
    return e, e2, 1.0 / l


def _mod_kernel(c_ref, w_ref, b_ref, o_ref):
    a = _silu(c_ref[...]).astype(BF16)
    o_ref[...] = _dot(a, w_ref[...].astype(BF16)) + b_ref[...]


def _modulation(cond8, w_mod, b_mod):
    tn = 1024
    n = N_MOD * D_MODEL
    return pl.pallas_call(
        _mod_kernel,
        out_shape=jax.ShapeDtypeStruct((DEPTH, 8, n), F32),
        grid=(DEPTH, n // tn),
        in_specs=[
            pl.BlockSpec((8, D_MODEL), lambda l, j: (0, 0)),
            pl.BlockSpec((None, D_MODEL, tn), lambda l, j: (l, 0, j)),
            pl.BlockSpec((None, 1, tn), lambda l, j: (l, 0, j)),
        ],
        out_specs=pl.BlockSpec((None, 8, tn), lambda l, j: (l, 0, j)),
        compiler_params=_cparams(("parallel", "parallel")),
        name="modulation",
    )(cond8, w_mod, b_mod.reshape(DEPTH, 1, n))


FFN_TF = 512
FFN_XC = 256
FFN_DC = 512
BF16_ROWS = 16


def _prenorm_kernel(*refs, split):
    if split:
        xc_ref, xs_ref, mod_ref, g_ref, h_ref = refs
        x = jnp.where(pl.program_id(0) < T_CTX // TM, xc_ref[...], xs_ref[...])
    else:
        x_ref, mod_ref, g_ref, h_ref = refs
        x = x_ref[...]
    h_ref[...] = _norm_mod(x, g_ref[...], mod_ref, 0).astype(BF16)


def _prenorm(xs, mod_l, g):
    n_ctx = T_CTX // TM
    if len(xs) == 2:
        x_specs = [pl.BlockSpec((TM, D_MODEL), lambda i: (jnp.minimum(i, n_ctx - 1), 0)),
                   pl.BlockSpec((TM, D_MODEL), lambda i: (jnp.maximum(i - n_ctx, 0), 0))]
    else:
        x_specs = [pl.BlockSpec((TM, D_MODEL), lambda i: (i, 0))]
    return pl.pallas_call(
        functools.partial(_prenorm_kernel, split=len(xs) == 2),
        out_shape=jax.ShapeDtypeStruct((T_ALL, D_MODEL), BF16),
        grid=(T_ALL // TM,),
        in_specs=x_specs + [
            pl.BlockSpec((None, N_MOD, D_MODEL), lambda i: (_cond_row(i), 0, 0)),
            pl.BlockSpec((None, 1, D_MODEL), lambda i: (0, 0, 0)),
        ],
        out_specs=pl.BlockSpec((TM, D_MODEL), lambda i: (i, 0)),
        compiler_params=_cparams(("parallel",)),
        name="prenorm",
    )(*xs, mod_l, g)


def _cast_weight(src_ref, dst_ref, kind):
    w = src_ref[...]
    if kind == "plain":
        dst_ref[...] = w.astype(BF16)
    elif kind == "tail":
        n = w.shape[1]
        dst_ref[:, :n] = w.astype(BF16)
        dst_ref[:, n:] = jnp.zeros((w.shape[0], dst_ref.shape[1] - n), BF16)
    else:
        for hh in range(H_C):
            dst_ref[:, hh * HEAD_PAD:hh * HEAD_PAD + DQK_C] = w[:, hh * DQK_C:(hh + 1) * DQK_C].astype(BF16)
            dst_ref[:, hh * HEAD_PAD + DQK_C:(hh + 1) * HEAD_PAD] = jnp.zeros((w.shape[0], HEAD_PAD - DQK_C), BF16)


def _ffn_kernel(*refs, base, cast_kinds):
    it = iter(refs)
    h_ref, x_ref, mod_ref, wg_ref, wu_ref, wd_ref = [next(it) for _ in range(6)]
    srcs = [next(it) for _ in cast_kinds]
    o_ref = next(it)
    dsts = [next(it) for _ in cast_kinds]
    f = pl.program_id(1)

    @pl.when(f == 0)
    def _():
        o_ref[...] = jnp.zeros_like(o_ref)

    h = h_ref[...]
    gate = _dot(h, wg_ref[...])
    up = _dot(h, wu_ref[...])
    a = (_silu(gate) * up).astype(BF16)
    for c in range(D_MODEL // FFN_DC):
        cs = slice(c * FFN_DC, (c + 1) * FFN_DC)
        o_ref[:, cs] += (0.5 * mod_ref[base + 2:base + 3, cs]) * _dot(a, wd_ref[:, cs])

    for c in range(D_MODEL // FFN_XC):
        @pl.when(f == c)
        def _(c=c):
            o_ref[:, c * FFN_XC:(c + 1) * FFN_XC] += x_ref[...]

    for src, dst, kind in zip(srcs, dsts, cast_kinds):
        _cast_weight(src, dst, kind)


def _cast_rows(total, steps):
    rows = BF16_ROWS
    while total % rows or total // rows > steps:
        rows += BF16_ROWS
    return rows


def _ffn(h, x, x_tile0, tile0, tiles, mod_l, w_b, k, casts):
    tm, tf = FFN_TM, FFN_TF
    nf = D_FF // tf
    assert nf >= D_MODEL // FFN_XC
    wg, wu, wd = w_b
    in_specs = [
        pl.BlockSpec((tm, D_MODEL), lambda i, f: (tile0 + i, 0)),
        pl.BlockSpec((tm, FFN_XC), lambda i, f: (x_tile0 + i, jnp.minimum(f, D_MODEL // FFN_XC - 1))),
        pl.BlockSpec((None, N_MOD, D_MODEL), lambda i, f: (_cond_row(tile0 + i, tm), 0, 0)),
        pl.BlockSpec((D_MODEL, tf), lambda i, f: (0, f)),
        pl.BlockSpec((D_MODEL, tf), lambda i, f: (0, f)),
        pl.BlockSpec((tf, D_MODEL), lambda i, f: (f, 0)),
    ]
    args = [h, x, mod_l, wg, wu, wd]
    out_shape = [jax.ShapeDtypeStruct((tiles * tm, D_MODEL), F32)]
    out_specs = [pl.BlockSpec((tm, D_MODEL), lambda i, f: (i, 0))]
    for w, lead, kind in casts:
        n_rows, n_cols = w.shape[-2:]
        out_cols = {"plain": n_cols, "tail": -(-n_cols // LANE) * LANE, "heads": H_C * HEAD_PAD}[kind]
        rows = _cast_rows(n_rows, tiles * nf)
        last = n_rows // rows - 1
        in_specs.append(pl.BlockSpec((None,) * len(lead) + (rows, n_cols),
                                     lambda i, f, lead=lead, last=last: (*lead, jnp.minimum(i * nf + f, last), 0)))
        args.append(w)
        out_shape.append(jax.ShapeDtypeStruct((n_rows, out_cols), BF16))
        out_specs.append(pl.BlockSpec((rows, out_cols), lambda i, f, last=last: (jnp.minimum(i * nf + f, last), 0)))
    return pl.pallas_call(
        functools.partial(_ffn_kernel, base=6 * k, cast_kinds=tuple(kind for _, _, kind in casts)),
        out_shape=out_shape,
        grid=(tiles, nf),
        in_specs=in_specs,
        out_specs=out_specs,
        compiler_params=pltpu.CompilerParams(dimension_semantics=("arbitrary", "arbitrary"),
                                             vmem_limit_bytes=FFN_VMEM_LIMIT),
        name="ffn",
    )(*args)


IN_TM = SEQ


def _inproj_even_kernel(*refs, latent, e):
    x_ref, mod_ref, g_ref, w_ref, gq_ref, gk_ref = refs[:6]
    refs = refs[6:]
    if latent:
        cos_ref, sa_ref, sb_ref, q_ref, k_ref, v_ref, f_ref = refs
    else:
        q_ref, k_ref, v_ref, f_ref, nk_ref, nv_ref = refs[-6:]
        if e > 0:
            nk_ref[:e] = refs[0][...]
            nv_ref[:e] = refs[1][...]
        nk_ref, nv_ref = nk_ref.at[e], nv_ref.at[e]
    dh2 = 2 * DH_A
    h = _norm_mod(x_ref[...], g_ref[...], mod_ref, 3).astype(BF16)
    gains = (gq_ref[...] * (DH_A ** -0.5 * LOG2E), gk_ref[...])
    for part, (g, o_ref) in enumerate(zip(gains, (q_ref, k_ref))):
        for hh in range(H_A):
            res = _dot(h, w_ref[:, part * W_A + hh * dh2:part * W_A + (hh + 1) * dh2])
            for m in range(2):
                xc = res[:, m * DH_A:(m + 1) * DH_A]
                y = xc * _rms(xc, DH_A) * g
                if latent:
                    y = _rope(y, cos_ref[...], sa_ref[...], sb_ref[...], DH_A // 4)
                o_ref[:, hh * dh2 + m * DH_A:hh * dh2 + (m + 1) * DH_A] = y.astype(BF16)
                if part == 1 and not latent:
                    nk_ref[hh, :, m * DH_A:(m + 1) * DH_A] = y
    for hh in range(H_A):
        res = _dot(h, w_ref[:, 2 * W_A + hh * dh2:2 * W_A + (hh + 1) * dh2])
        v_ref[:, hh * dh2:(hh + 1) * dh2] = res.astype(BF16)
        if not latent:
            nv_ref[hh] = res
    f_ref[...] = _dot(h, w_ref[:, 3 * W_A:]).astype(BF16)


def _inproj_even(ps, x, mod_l, g_norm_l, w_in, gqk, e, tabs, prev_kv):
    tm = IN_TM
    t0 = ps.tile0 * (TM // tm)
    xt0 = t0 if x.shape[0] == T_ALL else 0
    in_specs = [
        pl.BlockSpec((tm, D_MODEL), lambda i: (xt0 + i, 0)),
        pl.BlockSpec((None, N_MOD, D_MODEL), lambda i: (_cond_row(t0 + i, tm), 0, 0)),
        pl.BlockSpec((None, 1, D_MODEL), lambda i: (1, 0, 0)),
        pl.BlockSpec((D_MODEL, 3 * W_A + W_B), lambda i: (0, 0), pipeline_mode=pl.Buffered(1)),
        pl.BlockSpec((None, None, 1, DH_A), lambda i: (e, 0, 0, 0)),
        pl.BlockSpec((None, None, 1, DH_A), lambda i: (e, 1, 0, 0)),
    ]
    args = [x, mod_l, g_norm_l, w_in, gqk, gqk]
    out_shape = [jax.ShapeDtypeStruct((ps.rows, W_A), BF16)] * 3 + [jax.ShapeDtypeStruct((ps.rows, W_B), BF16)]
    out_specs = [pl.BlockSpec((tm, W_A), lambda i: (i, 0))] * 3 + [pl.BlockSpec((tm, W_B), lambda i: (i, 0))]
    if ps.latent:
        tab_spec = pl.BlockSpec((tm, LANE), lambda i: (i % (DEC_SEQ // tm), 0))
        in_specs += [tab_spec] * 3
        args += list(tabs)
    else:
        kv_block = lambda n: pl.BlockSpec((None, n, H_A, SEQ, 2 * DH_A), lambda i: (i, 0, 0, 0, 0))
        out_shape += [jax.ShapeDtypeStruct((BATCH, e + 1, H_A, SEQ, 2 * DH_A), F32)] * 2
        out_specs += [kv_block(e + 1)] * 2
        if e > 0:
            in_specs += [kv_block(e)] * 2
            args += list(prev_kv)
    return pl.pallas_call(
        functools.partial(_inproj_even_kernel, latent=ps.latent, e=e),
        out_shape=out_shape,
        grid=(ps.rows // tm,),
        in_specs=in_specs,
        out_specs=out_specs,
        compiler_params=_cparams(("parallel",)),
        name="inproj_even",
    )(*args)


def _diff_attn_kernel(lam_ref, gsub_ref, q_ref, k_ref, v_ref, *rest, lam_init, heads, has_ctx, sub):
    if has_ctx:
        ck_ref, cv_ref, o_ref = rest
    else:
        (o_ref,) = rest
    lv = lam_ref[...]
    lam = (jnp.exp(jnp.sum(lv[0:1] * lv[1:2], axis=-1, keepdims=True))
           - jnp.exp(jnp.sum(lv[2:3] * lv[3:4], axis=-1, keepdims=True)) + lam_init)
    dh2 = 2 * DH_A
    gsub = gsub_ref[...] * (1.0 - lam_init)
    for hh in range(heads):
        hs = slice(hh * dh2, (hh + 1) * dh2)
        k = k_ref[:, hs]
        v = v_ref[:, hs]
        if has_ctx:
            ck = ck_ref[hh].astype(BF16)
            cv = cv_ref[hh].astype(BF16)
        for r in range(q_ref.shape[0] // sub):
            rs = slice(r * sub, (r + 1) * sub)
            q = q_ref[rs, hs]
            o = None
            for m in range(2):
                sl = slice(m * DH_A, (m + 1) * DH_A)
                s = _dot_nt(q[:, sl], k[:, sl])
                s2 = _dot_nt(q[:, sl], ck[:, sl]) if has_ctx else None
                e, e2, inv = _softmax_exp(s, s2, True)
                om = _dot(e.astype(BF16), v)
                if has_ctx:
                    om = om + _dot(e2.astype(BF16), cv)
                o = om * inv if m == 0 else o - om * (lam * inv)
            o_ref[rs, hs] = (o * _rms(o, dh2) * gsub).astype(BF16)


def _diff_attn_ctx(q, k, v, lam_vecs, gsub, e, lam_init):
    spec = pl.BlockSpec((SEQ, W_A), lambda b: (b, 0))
    return pl.pallas_call(
        functools.partial(_diff_attn_kernel, lam_init=lam_init, heads=H_A, has_ctx=False, sub=SEQ),
        out_shape=jax.ShapeDtypeStruct((T_CTX, W_A), BF16),
        grid=(BATCH,),
        in_specs=[
            pl.BlockSpec((None, 4, DH_A), lambda b: (e, 0, 0)),
            pl.BlockSpec((None, 1, 2 * DH_A), lambda b: (e, 0, 0)),
            spec, spec, spec,
        ],
        out_specs=spec,
        compiler_params=_cparams(("parallel",)),
        name="diff_attn_ctx",
    )(lam_vecs, gsub, q, k, v)


def _diff_attn_smp(q, k, v, cache_k, cache_v, lam_vecs, gsub, e, lam_init):
    tq, sub = 1024, 256
    w = 2 * DH_A
    nq = DEC_SEQ // tq
    cache_spec = pl.BlockSpec((None, None, 1, PAST_LEN, w), lambda b, h, i: (b, e, h, 0, 0))
    return pl.pallas_call(
        functools.partial(_diff_attn_kernel, lam_init=lam_init, heads=1, has_ctx=True, sub=sub),
        out_shape=jax.ShapeDtypeStruct((T_SMP, W_A), BF16),
        grid=(DEC_BATCH, H_A, nq),
        in_specs=[
            pl.BlockSpec((None, 4, DH_A), lambda b, h, i: (e, 0, 0)),
            pl.BlockSpec((None, 1, w), lambda b, h, i: (e, 0, 0)),
            pl.BlockSpec((tq, w), lambda b, h, i: (b * nq + i, h)),
            pl.BlockSpec((DEC_SEQ, w), lambda b, h, i: (b, h)),
            pl.BlockSpec((DEC_SEQ, w), lambda b, h, i: (b, h)),
            cache_spec, cache_spec,
        ],
        out_specs=pl.BlockSpec((tq, w), lambda b, h, i: (b * nq + i, h)),
        compiler_params=_cparams(("parallel", "parallel", "arbitrary")),
        name="diff_attn_smp",
    )(lam_vecs, gsub, q, k, v, cache_k, cache_v)


DFT_LO = 64


def _fourier_kernel(x_ref, bc_ref, bs_ref, c0_ref, s0_ref, c1_ref, s1_ref, o_ref, y1_ref, y2_ref, *, scale, tr):
    r = pl.program_id(1)

    @pl.when(r == 0)
    def _():
        x = x_ref[...]
        y1_ref[...] = _dot(x, bc_ref[...]).astype(BF16)
        y2_ref[...] = _dot(x, bs_ref[...]).astype(BF16)

    c0, s0 = c0_ref[...], s0_ref[...]
    cs, ss = [], []
    for jj in range(tr // DFT_LO):
        j1 = r * (tr // DFT_LO) + jj
        c1, s1 = c1_ref[pl.ds(j1, 1), :], s1_ref[pl.ds(j1, 1), :]
        cs.append((c1 * c0 - s1 * s0).astype(BF16))
        ss.append((s1 * c0 + c1 * s0).astype(BF16))
    out = _dot(jnp.concatenate(cs, axis=0), y1_ref[...]) - _dot(jnp.concatenate(ss, axis=0), y2_ref[...])
    o_ref[...] = (out * scale).astype(BF16)


def _fourier(f, s, dft_ch, dft_seq):
    tr = 256
    nb = f.shape[0] // s
    bc, bs = dft_ch
    full = lambda a: pl.BlockSpec(a.shape, lambda b, r: (0, 0))
    return pl.pallas_call(
        functools.partial(_fourier_kernel, scale=1.0 / math.sqrt(s * DG_B), tr=tr),
        out_shape=jax.ShapeDtypeStruct(f.shape, BF16),
        grid=(nb, s // tr),
        in_specs=[pl.BlockSpec((s, W_B), lambda b, r: (b, 0)), full(bc), full(bs)] + [full(t) for t in dft_seq],
        out_specs=pl.BlockSpec((tr, W_B), lambda b, r: (b * (s // tr) + r, 0)),
        scratch_shapes=[pltpu.VMEM((s, W_B), BF16), pltpu.VMEM((s, W_B), BF16)],
        compiler_params=_cparams(("parallel", "arbitrary")),
        name="fourier",
    )(f, bc, bs, *dft_seq)


def _dft_seq_tables(n):
    a0, a1 = _dft_angles(DFT_LO, 1, n), _dft_angles(n // DFT_LO, DFT_LO, n)
    return jnp.cos(a0), jnp.sin(a0), jnp.cos(a1), jnp.sin(a1)


def _dft_angles(rows, stride, n):
    j = jnp.arange(rows, dtype=jnp.int32)[:, None] * stride
    k = jnp.arange(n, dtype=jnp.int32)[None, :]
    return ((j * k) % n).astype(F32) * (2.0 * math.pi / n)


def _dft_mats(n):
    lo = min(n, 64)
    a0 = _dft_angles(lo, 1, n)
    c0, s0 = jnp.cos(a0), jnp.sin(a0)
    if lo == n:
        return c0, s0
    a1 = _dft_angles(n // lo, lo, n)
    c1, s1 = jnp.cos(a1)[:, None, :], jnp.sin(a1)[:, None, :]
    c = c1 * c0[None] - s1 * s0[None]
    s = s1 * c0[None] + c1 * s0[None]
    return c.reshape(n, n), s.reshape(n, n)


def _dft_channel_mats():
    c, s = _dft_mats(DG_B)
    eye = jnp.eye(G_B, dtype=F32)
    return jnp.kron(eye, c).astype(BF16), jnp.kron(eye, s).astype(BF16)


def _outproj_kernel(*refs, n_act, n_x):
    is_ctx = pl.program_id(0) < T_CTX // TM
    x = refs[0][...] if n_x == 1 else jnp.where(is_ctx, refs[0][...], refs[1][...])
    mod_ref, g_ref = refs[n_x:n_x + 2]
    act = refs[n_x + 2:n_x + 2 + 3 * n_act]
    o_ref, hn_ref = refs[n_x + 2 + 3 * n_act:]
    y = None
    for a in range(n_act):
        ctx_ref, smp_ref, w_ref = act[3 * a:3 * a + 3]
        lhs = jnp.where(is_ctx, ctx_ref[...], smp_ref[...])
        t = _dot(lhs, w_ref[...])
        y = t if y is None else y + t
    x_new = x + mod_ref[5:6, :] * y
    o_ref[...] = x_new
    hn_ref[...] = _norm_mod(x_new, g_ref[...], mod_ref, 6).astype(BF16)


def _outproj(xs, mod_l, g_norm_l, acts):
    n_ctx = T_CTX // TM
    row_spec = pl.BlockSpec((TM, D_MODEL), lambda i: (i, 0))
    ctx_rows = lambda n: pl.BlockSpec((TM, n), lambda i: (jnp.minimum(i, n_ctx - 1), 0))
    smp_rows = lambda n: pl.BlockSpec((TM, n), lambda i: (jnp.maximum(i - n_ctx, 0), 0))
    in_specs = ([row_spec] if len(xs) == 1 else [ctx_rows(D_MODEL), smp_rows(D_MODEL)]) + [
        pl.BlockSpec((None, N_MOD, D_MODEL), lambda i: (_cond_row(i), 0, 0)),
        pl.BlockSpec((None, 1, D_MODEL), lambda i: (2, 0, 0)),
    ]
    args = [*xs, mod_l, g_norm_l]
    for ctx, smp, w, rb in acts:
        kdim = ctx.shape[1]
        in_specs += [ctx_rows(kdim), smp_rows(kdim), pl.BlockSpec((kdim, D_MODEL), lambda i, rb=rb: (rb, 0))]
        args += [ctx, smp, w]
    return pl.pallas_call(
        functools.partial(_outproj_kernel, n_act=len(acts), n_x=len(xs)),
        out_shape=(jax.ShapeDtypeStruct((T_ALL, D_MODEL), F32), jax.ShapeDtypeStruct((T_ALL, D_MODEL), BF16)),
        grid=(T_ALL // TM,),
        in_specs=in_specs,
        out_specs=(row_spec, row_spec),
        compiler_params=_cparams(("parallel",)),
        name="outproj",
    )(*args)


MLA_TM = 256


def _mla_proj_kernel(x_ref, mod_ref, g_ref, win_ref, wuq_ref, wukv_ref, gql_ref, gkvl_ref, gq_ref, gk_ref,
                     cos_ref, sa_ref, sb_ref, q_ref, k_ref, v_ref, ckv_ref, kpe_ref):
    h = _norm_mod(x_ref[...], g_ref[...], mod_ref, 3).astype(BF16)
    res = _dot(h, win_ref[...])
    cq = res[:, :Q_LORA]
    cqn = (cq * _rms(cq, Q_LORA) * gql_ref[...]).astype(BF16)
    ckv = res[:, Q_LORA:Q_LORA + KV_LORA]
    ckvn = ckv * _rms(ckv, KV_LORA) * gkvl_ref[...]
    ckv_ref[...] = ckvn
    kpe = res[:, Q_LORA + KV_LORA:]
    kpe_ref[...] = kpe
    cos, sa, sb = cos_ref[...], sa_ref[...], sb_ref[...]
    gq, gk = gq_ref[...] * (DQK_C ** -0.5 * LOG2E), gk_ref[...]
    ss_pe = jnp.sum(kpe * kpe, axis=-1, keepdims=True)
    kr = _rope(kpe * gk[:, LANE:], cos, sa, sb, ROPE_C // 4)
    ckvb = ckvn.astype(BF16)
    for hh in range(H_C):
        hs = slice(hh * HEAD_PAD, (hh + 1) * HEAD_PAD)
        lo = slice(hh * HEAD_PAD, hh * HEAD_PAD + LANE)
        hi = slice(hh * HEAD_PAD + LANE, (hh + 1) * HEAD_PAD)
        xh = _dot(cqn, wuq_ref[:, hs])
        y = xh * _rms(xh, DQK_C) * gq
        q_ref[:, lo] = y[:, :LANE].astype(BF16)
        q_ref[:, hi] = _rope(y[:, LANE:], cos, sa, sb, ROPE_C // 4).astype(BF16)
        kv = _dot(ckvb, wukv_ref[:, hs])
        k_nope = kv[:, :LANE]
        ss = jnp.sum(k_nope * k_nope, axis=-1, keepdims=True) + ss_pe
        r = lax.rsqrt(ss * (1.0 / DQK_C) + EPS)
        k_ref[:, lo] = (k_nope * r * gk[:, :LANE]).astype(BF16)
        k_ref[:, hi] = (kr * r).astype(BF16)
        v_ref[:, hh * DV_C:(hh + 1) * DV_C] = kv[:, LANE:].astype(BF16)


def _mla_proj(x, mod_l, g_norm_l, w_in_pad, w_uq_pad, w_ukv, gq_lora, gkv_lora, gq_pad, gk_pad, o, tabs):
    tm = MLA_TM
    n_in = Q_LORA + KV_LORA + LANE
    resident = dict(pipeline_mode=pl.Buffered(1))
    tab_spec = pl.BlockSpec((tm, LANE), lambda i: (i, 0))
    row = lambda n: pl.BlockSpec((tm, n), lambda i: (i, 0))
    return pl.pallas_call(
        _mla_proj_kernel,
        out_shape=(jax.ShapeDtypeStruct((T_ALL, H_C * HEAD_PAD), BF16),
                   jax.ShapeDtypeStruct((T_ALL, H_C * HEAD_PAD), BF16),
                   jax.ShapeDtypeStruct((T_ALL, H_C * DV_C), BF16),
                   jax.ShapeDtypeStruct((T_ALL, KV_LORA), F32),
                   jax.ShapeDtypeStruct((T_ALL, LANE), F32)),
        grid=(T_ALL // tm,),
        in_specs=[
            row(D_MODEL),
            pl.BlockSpec((None, N_MOD, D_MODEL), lambda i: (_cond_row(i, tm), 0, 0)),
            pl.BlockSpec((None, 1, D_MODEL), lambda i: (1, 0, 0)),
            pl.BlockSpec((D_MODEL, n_in), lambda i: (0, 0), **resident),
            pl.BlockSpec((Q_LORA, H_C * HEAD_PAD), lambda i: (0, 0), **resident),
            pl.BlockSpec((KV_LORA, H_C * HEAD_PAD), lambda i: (0, 0), **resident),
            pl.BlockSpec((None, 1, Q_LORA), lambda i: (o, 0, 0)),
            pl.BlockSpec((None, 1, KV_LORA), lambda i: (o, 0, 0)),
            pl.BlockSpec((None, 1, HEAD_PAD), lambda i: (o, 0, 0)),
            pl.BlockSpec((None, 1, HEAD_PAD), lambda i: (o, 0, 0)),
            tab_spec, tab_spec, tab_spec,
        ],
        out_specs=(row(H_C * HEAD_PAD), row(H_C * HEAD_PAD), row(H_C * DV_C), row(KV_LORA), row(LANE)),
        compiler_params=_cparams(("parallel",)),
        name="mla_proj",
    )(x, mod_l, g_norm_l, w_in_pad, w_uq_pad, w_ukv, gq_lora, gkv_lora, gq_pad, gk_pad, *tabs)


MLA_HB = 4


def _mla_kv_kernel(ckv_ref, kpe_ref, w_ref, g_ref, *rest, use_rope):
    if use_rope:
        cos_ref, sa_ref, sb_ref, k_ref, v_ref = rest
    else:
        k_ref, v_ref = rest
    ckv = ckv_ref[...].astype(BF16)
    g = g_ref[...]
    kpe = kpe_ref[...]
    ss_pe = jnp.sum(kpe * kpe, axis=-1, keepdims=True)
    kr = kpe * g[:, LANE:]
    if use_rope:
        kr = _rope(kr, cos_ref[...], sa_ref[...], sb_ref[...], ROPE_C // 4)
    for hh in range(MLA_HB):
        res = _dot(ckv, w_ref[:, hh * 2 * LANE:(hh + 1) * 2 * LANE])
        k_nope = res[:, :LANE]
        ss = jnp.sum(k_nope * k_nope, axis=-1, keepdims=True) + ss_pe
        r = lax.rsqrt(ss * (1.0 / DQK_C) + EPS)
        k_ref[:, hh * HEAD_PAD:hh * HEAD_PAD + LANE] = (k_nope * r * g[:, :LANE]).astype(BF16)
        k_ref[:, hh * HEAD_PAD + LANE:(hh + 1) * HEAD_PAD] = (kr * r).astype(BF16)
        v_ref[:, hh * DV_C:(hh + 1) * DV_C] = res[:, LANE:].astype(BF16)


def _mla_kv(ckv, kpe_pad, tile0, rows, w_ukv, gk_pad, o, tabs):
    use_rope = tabs is not None
    in_specs = [
        pl.BlockSpec((TM, KV_LORA), lambda i, j: (tile0 + i, 0)),
        pl.BlockSpec((TM, LANE), lambda i, j: (tile0 + i, 0)),
        pl.BlockSpec((KV_LORA, MLA_HB * 2 * LANE), lambda i, j: (0, j)),
        pl.BlockSpec((None, 1, HEAD_PAD), lambda i, j: (o, 0, 0)),
    ]
    args = [ckv, kpe_pad, w_ukv, gk_pad]
    if use_rope:
        in_specs += [pl.BlockSpec((TM, LANE), lambda i, j: (i % (DEC_SEQ // TM), 0))] * 3
        args += list(tabs)
    return pl.pallas_call(
        functools.partial(_mla_kv_kernel, use_rope=use_rope),
        out_shape=(jax.ShapeDtypeStruct((rows, H_C * HEAD_PAD), BF16),
                   jax.ShapeDtypeStruct((rows, H_C * DV_C), BF16)),
        grid=(rows // TM, H_C // MLA_HB),
        in_specs=in_specs,
        out_specs=(pl.BlockSpec((TM, MLA_HB * HEAD_PAD), lambda i, j: (i, j)),
                   pl.BlockSpec((TM, MLA_HB * DV_C), lambda i, j: (i, j))),
        compiler_params=_cparams(("parallel", "parallel")),
        name="mla_kv",
    )(*args)


def _mla_attn_kernel(q_ref, k_ref, v_ref, *rest, heads, has_ctx, sub):
    if has_ctx:
        ck_ref, cv_ref, o_ref, v1_ref, cv1_ref = rest
    else:
        o_ref, v1_ref = rest
    def with_ones(dst_ref, hh, v):
        dst_ref[hh, :, :DV_C] = v
        dst_ref[hh, :, DV_C:] = jnp.ones_like(v)
        return dst_ref[hh]

    for hh in range(heads):
        ks = slice(hh * HEAD_PAD, (hh + 1) * HEAD_PAD)
        vs = slice(hh * DV_C, (hh + 1) * DV_C)
        v1 = with_ones(v1_ref, hh, v_ref[:, vs])
        if has_ctx:
            cv1 = with_ones(cv1_ref, hh, cv_ref[:, vs])
        for r in range(q_ref.shape[0] // sub):
            rs = slice(r * sub, (r + 1) * sub)
            q = q_ref[rs, ks]
            s = _dot_nt(q, k_ref[:, ks])
            s2 = _dot_nt(q, ck_ref[:, ks]) if has_ctx else None
            e, e2, _ = _softmax_exp(s, s2, False)
            o = _dot(e.astype(BF16), v1)
            if has_ctx:
                o = o + _dot(e2.astype(BF16), cv1)
            o_ref[rs, vs] = (o[:, :DV_C] * (1.0 / o[:, DV_C:])).astype(BF16)


def _mla_attn_ctx(q, k, v):
    heads = 8
    return pl.pallas_call(
        functools.partial(_mla_attn_kernel, heads=heads, has_ctx=False, sub=SEQ),
        out_shape=jax.ShapeDtypeStruct((T_CTX, H_C * DV_C), BF16),
        grid=(BATCH, H_C // heads),
        in_specs=[
            pl.BlockSpec((SEQ, heads * HEAD_PAD), lambda b, j: (b, j)),
            pl.BlockSpec((SEQ, heads * HEAD_PAD), lambda b, j: (b, j)),
            pl.BlockSpec((SEQ, heads * DV_C), lambda b, j: (b, j)),
        ],
        out_specs=pl.BlockSpec((SEQ, heads * DV_C), lambda b, j: (b, j)),
        scratch_shapes=[pltpu.VMEM((heads, SEQ, 2 * DV_C), BF16)],
        compiler_params=_cparams(("parallel", "parallel")),
        name="mla_attn_ctx",
    )(q, k, v)


def _mla_attn_smp(q, k, v, ck, cv):
    tq, sub = DEC_SEQ, 256
    nq = DEC_SEQ // tq
    q0, s0 = T_CTX // tq, T_CTX // DEC_SEQ
    return pl.pallas_call(
        functools.partial(_mla_attn_kernel, heads=1, has_ctx=True, sub=sub),
        out_shape=jax.ShapeDtypeStruct((T_SMP, H_C * DV_C), BF16),
        grid=(DEC_BATCH, H_C, nq),
        in_specs=[
            pl.BlockSpec((tq, HEAD_PAD), lambda b, h, i: (q0 + b * nq + i, h)),
            pl.BlockSpec((DEC_SEQ, HEAD_PAD), lambda b, h, i: (s0 + b, h)),
            pl.BlockSpec((DEC_SEQ, DV_C), lambda b, h, i: (s0 + b, h)),
            pl.BlockSpec((PAST_LEN, HEAD_PAD), lambda b, h, i: (b, h)),
            pl.BlockSpec((PAST_LEN, DV_C), lambda b, h, i: (b, h)),
        ],
        out_specs=pl.BlockSpec((tq, DV_C), lambda b, h, i: (b * nq + i, h)),
        scratch_shapes=[pltpu.VMEM((1, DEC_SEQ, 2 * DV_C), BF16), pltpu.VMEM((1, PAST_LEN, 2 * DV_C), BF16)],
        compiler_params=_cparams(("parallel", "parallel", "arbitrary")),
        name="mla_attn_smp",
    )(q, k, v, ck, cv)


def _axial_angles(rows, rot_dim):
    row = jnp.repeat(jnp.arange(rows, dtype=F32), GRID_W)
    col = jnp.tile(jnp.arange(GRID_W, dtype=F32), rows)
    half = rot_dim // 2
    inv_freq = ROPE_BASE ** (-jnp.arange(0, half, 2, dtype=F32) / half)
    ang_r = row[:, None] * inv_freq[None, :]
    ang_c = col[:, None] * inv_freq[None, :]
    return jnp.concatenate([ang_r, ang_r, ang_c, ang_c], axis=-1)


def _rope_tables(rows, rot_dim):
    ang = _axial_angles(rows, rot_dim)
    q = rot_dim // 4
    blk = (jnp.arange(rot_dim) // q) % 2
    cos, sin = jnp.cos(ang), jnp.sin(ang)
    sin_a = jnp.where(blk == 0, -sin, 0.0)
    sin_b = jnp.where(blk == 1, sin, 0.0)

    def full(t, fill):
        return jnp.pad(t, ((0, 0), (0, LANE - rot_dim)), constant_values=fill)

    return full(cos, 1.0), full(sin_a, 0.0), full(sin_b, 0.0)


def _joint_tables(tabs):
    def joint(t, fill):
        return jnp.concatenate([jnp.full((T_CTX, LANE), fill, F32), jnp.tile(t, (DEC_BATCH, 1))], axis=0)
    return joint(tabs[0], 1.0), joint(tabs[1], 0.0), joint(tabs[2], 0.0)


def kernel(x_prompt, x_sample, c, cache_diff_k, cache_diff_v, cache_mla_ckv, cache_mla_kpe, c_ctx, w_mod, b_mod, g_norm, w_ffn_gate, w_ffn_up, w_ffn_down, w_in_ab, w_out_ab, g_qk_diff, diff_lambda, g_diff_sub, w_in_mla, g_q_lora, w_uq, g_kv_lora, w_ukv, g_qk_mla, w_o_mla):
    rows = DEC_SEQ // GRID_W
    tabs_a = _rope_tables(rows, DH_A)
    tabs_c_joint = _joint_tables(_rope_tables(rows, ROPE_C))
    dft_ch = _dft_channel_mats()
    dft_ctx, dft_smp = _dft_seq_tables(SEQ), _dft_seq_tables(DEC_SEQ)
    p_ctx, p_smp = _Pass(False), _Pass(True)

    w_ffn = (w_ffn_gate[0, 0].astype(BF16), w_ffn_up[0, 0].astype(BF16), w_ffn_down[0, 0].astype(BF16))

    def ffn_casts(l, k):
        return [(w_ffn_gate, (l, k), "plain"), (w_ffn_up, (l, k), "plain"), (w_ffn_down, (l, k), "plain")]

    def mixer_casts(l):
        j = l // 2
        if l % 2 == 0:
            return [(w_in_ab, (j,), "plain"), (w_out_ab, (j,), "plain")]
        return [(w_in_mla, (j,), "tail"), (w_uq, (j,), "heads"), (w_ukv, (j,), "plain"), (w_o_mla, (j,), "plain")]

    g_qk_mla_pad = jnp.pad(g_qk_mla, ((0, 0), (0, 0), (0, HEAD_PAD - DQK_C)))[:, :, None, :]
    gq_mla, gk_mla = g_qk_mla_pad[:, 0], g_qk_mla_pad[:, 1]
    g_norm4 = g_norm[:, :, None, :]
    gqk_diff = g_qk_diff[:, :, None, :]
    g_sub3 = g_diff_sub[:, None, :]
    g_q_lora3, g_kv_lora3 = g_q_lora[:, None, :], g_kv_lora[:, None, :]

    cond8 = jnp.concatenate([c_ctx[None, :], c, jnp.zeros((8 - N_COND, D_MODEL), F32)], axis=0)
    mod = _modulation(cond8, w_mod, b_mod)[:, :N_COND].reshape(DEPTH, N_COND, N_MOD, D_MODEL)

    x_ctx, x_smp = x_prompt.reshape(T_CTX, D_MODEL), x_sample.reshape(T_SMP, D_MODEL)
    n_ctx, n_smp, n_all = T_CTX // FFN_TM, T_SMP // FFN_TM, T_ALL // FFN_TM
    new_kv = None
    mla_ckv, mla_kpe = [], []

    for l in range(DEPTH):
        mod_l = mod[l]
        if l == 0:
            h = _prenorm((x_ctx, x_smp), mod_l, g_norm4[l])
            xc, *cast = _ffn(h, x_ctx, 0, 0, n_ctx, mod_l, w_ffn, 0, ffn_casts(l, 1) + mixer_casts(l)[:1])
            xs, w_out0 = _ffn(h, x_smp, 0, n_ctx, n_smp, mod_l, w_ffn, 0, mixer_casts(l)[1:])
            w_ffn, w_mix, xs_l = cast[:3], [cast[3], w_out0], (xc, xs)
        else:
            h = _prenorm((x,), mod_l, g_norm4[l])
            x, *w_ffn = _ffn(h, x, 0, 0, n_all, mod_l, w_ffn, 0, ffn_casts(l, 1))
            xs_l = (x, x)
        if l % 2 == 0:
            e = l // 2
            lam_init = 0.8 - 0.6 * math.exp(-0.3 * l)
            w_in_b, w_out_b = w_mix
            q_c, k_c, v_c, f_c, *new_kv = _inproj_even(p_ctx, xs_l[0], mod_l, g_norm4[l], w_in_b, gqk_diff, e,
                                                      None, new_kv)
            q_s, k_s, v_s, f_s = _inproj_even(p_smp, xs_l[1], mod_l, g_norm4[l], w_in_b, gqk_diff, e, tabs_a, None)
            o_ctx = _diff_attn_ctx(q_c, k_c, v_c, diff_lambda, g_sub3, e, lam_init)
            o_smp = _diff_attn_smp(q_s, k_s, v_s, cache_diff_k, cache_diff_v, diff_lambda, g_sub3, e, lam_init)
            fo_ctx = _fourier(f_c, SEQ, dft_ch, dft_ctx)
            fo_smp = _fourier(f_s, DEC_SEQ, dft_ch, dft_smp)
            x, h = _outproj(xs_l if l == 0 else (x,), mod_l, g_norm4[l],
                            [(o_ctx, o_smp, w_out_b, 0), (fo_ctx, fo_smp, w_out_b, W_A // W_B)])
        else:
            o = l // 2
            w_in_mla_b, w_uq_b, w_ukv_b, w_o_b = w_mix
            q, k, v, ckvn, kpe_pad = _mla_proj(x, mod_l, g_norm4[l], w_in_mla_b, w_uq_b, w_ukv_b, g_q_lora3,
                                               g_kv_lora3, gq_mla, gk_mla, o, tabs_c_joint)
            cache_ckv = cache_mla_ckv[:, o].reshape(DEC_BATCH * PAST_LEN, KV_LORA)
            cache_kpe = jnp.pad(cache_mla_kpe[:, o].reshape(DEC_BATCH * PAST_LEN, ROPE_C),
                                ((0, 0), (0, LANE - ROPE_C)))
            ck, cv = _mla_kv(cache_ckv, cache_kpe, 0, DEC_BATCH * PAST_LEN, w_ukv_b, gk_mla, o, None)
            a_ctx = _mla_attn_ctx(q, k, v)
            a_smp = _mla_attn_smp(q, k, v, ck, cv)
            x, h = _outproj((x,), mod_l, g_norm4[l], [(a_ctx, a_smp, w_o_b, 0)])
            mla_ckv.append(ckvn[:T_CTX].reshape(BATCH, SEQ, KV_LORA))
            mla_kpe.append(kpe_pad[:T_CTX, :ROPE_C].reshape(BATCH, SEQ, ROPE_C))
        if l + 1 < DEPTH:
            x, *cast = _ffn(h, x, 0, 0, n_all, mod_l, w_ffn, 1, ffn_casts(l + 1, 0) + mixer_casts(l + 1))
            w_ffn, w_mix = cast[:3], cast[3:]
        else:
            (y_ctx,) = _ffn(h, x, 0, 0, n_ctx, mod_l, w_ffn, 1, [])
            (y_smp,) = _ffn(h, x, n_ctx, n_ctx, n_smp, mod_l, w_ffn, 1, [])

    y_prompt = y_ctx.reshape(BATCH, SEQ, D_MODEL)
    y_sample = y_smp.reshape(DEC_BATCH, DEC_SEQ, D_MODEL)
    return (y_prompt, y_sample, new_kv[0], new_kv[1], jnp.stack(mla_ckv, axis=1), jnp.stack(mla_kpe, axis=1))
```

```python
import functools
import math

import jax
import jax.numpy as jnp
from jax import lax
from jax.experimental import pallas as pl
from jax.experimental.pallas import tpu as pltpu

F32 = jnp.float32
BF16 = jnp.bfloat16

D_MODEL = 2048
BATCH = 32
SEQ = 256
DEPTH = 4
DEC_BATCH = 2
DEC_SEQ = 2048
PAST_LEN = 256
GRID_W = 64
N_EVEN = 2
N_ODD = 2
H_A = 6
DH_A = 128
W_A = H_A * 2 * DH_A
G_B = 4
DG_B = 128
W_B = G_B * DG_B
H_C = 16
Q_LORA = 1536
KV_LORA = 512
NOPE_C = 128
ROPE_C = 64
DQK_C = NOPE_C + ROPE_C
DV_C = 128
D_FF = 5632
N_MOD = 9
ROPE_BASE = 10000.0
EPS = 1e-6
LOG2E = 1.4426950408889634

T_CTX = BATCH * SEQ
T_SMP = DEC_BATCH * DEC_SEQ
T_ALL = T_CTX + T_SMP
N_COND = 1 + DEC_BATCH
LANE = 128
HEAD_PAD = 256
TM = 512

VMEM_LIMIT = 52 * 1024 * 1024
FFN_TM = 1024
FFN_VMEM_LIMIT = 60 * 1024 * 1024


def _cparams(sem):
    return pltpu.CompilerParams(dimension_semantics=sem, vmem_limit_bytes=VMEM_LIMIT)


def _cond_row(i, tm=TM):
    n_ctx = T_CTX // tm
    per = DEC_SEQ // tm
    return jnp.where(i < n_ctx, 0, 1 + (i - n_ctx) // per)


class _Pass:
    def __init__(self, latent):
        self.latent = latent
        self.rows = T_SMP if latent else T_CTX
        self.tile0 = (T_CTX // TM) if latent else 0
        self.tiles = self.rows // TM
        self.seq = DEC_SEQ if latent else SEQ
        self.batch = DEC_BATCH if latent else BATCH

    def cond(self, i):
        return 1 + i // (DEC_SEQ // TM) if self.latent else 0


def _dot(a, b):
    return jnp.dot(a, b, preferred_element_type=F32)


def _dot_nt(a, b):
    return lax.dot_general(a, b, (((1,), (1,)), ((), ())), preferred_element_type=F32)


def _silu(x):
    return x / (1.0 + jnp.exp(-x))


def _rms(x, denom):
    return lax.rsqrt(jnp.sum(x * x, axis=-1, keepdims=True) * (1.0 / denom) + EPS)


def _norm_mod(x, g, mod_ref, base):
    sh = mod_ref[base:base + 1, :]
    sc = mod_ref[base + 1:base + 2, :]
    return (x * _rms(x, D_MODEL) * g) * (1.0 + sc) + sh


def _rope(y, cos, sin_a, sin_b, q):
    return y * cos + pltpu.roll(y, LANE - q, 1) * sin_a + pltpu.roll(y, q, 1) * sin_b


def _softmax_exp(s, s2, need_sum):
    m = jnp.max(s, axis=-1, keepdims=True)
    if s2 is not None:
        m = jnp.maximum(m, jnp.max(s2, axis=-1, keepdims=True))
    e = jnp.exp2(s - m)
    e2 = jnp.exp2(s2 - m) if s2 is not None else None
    if not need_sum:
        return e, e2, None
    l = jnp.sum(e, axis=-1, keepdims=True)
    if s2 is not None:
        l = l + jnp.sum(e2, axis=-1, keepdims=True)
    return e, e2, 1.0 / l


def _mod_kernel(c_ref, w_ref, b_ref, o_ref):
    a = _silu(c_ref[...]).astype(BF16)
    o_ref[...] = _dot(a, w_ref[...].astype(BF16)) + b_ref[...]


def _modulation(cond8, w_mod, b_mod):
    tn = 1024
    n = N_MOD * D_MODEL
    return pl.pallas_call(
        _mod_kernel,
        out_shape=jax.ShapeDtypeStruct((DEPTH, 8, n), F32),
        grid=(DEPTH, n // tn),
        in_specs=[
            pl.BlockSpec((8, D_MODEL), lambda l, j: (0, 0)),
            pl.BlockSpec((None, D_MODEL, tn), lambda l, j: (l, 0, j)),
            pl.BlockSpec((None, 1, tn), lambda l, j: (l, 0, j)),
        ],
        out_specs=pl.BlockSpec((None, 8, tn), lambda l, j: (l, 0, j)),
        compiler_params=_cparams(("parallel", "parallel")),
        name="modulation",
    )(cond8, w_mod, b_mod.reshape(DEPTH, 1, n))


FFN_TF = 512
FFN_XC = 256
FFN_DC = 512
BF16_ROWS = 16


def _prenorm_kernel(*refs, split):
    if split:
        xc_ref, xs_ref, mod_ref, g_ref, h_ref = refs
        x = jnp.where(pl.program_id(0) < T_CTX // TM, xc_ref[...], xs_ref[...])
    else:
        x_ref, mod_ref, g_ref, h_ref = refs
        x = x_ref[...]
    h_ref[...] = _norm_mod(x, g_ref[...], mod_ref, 0).astype(BF16)


def _prenorm(xs, mod_l, g):
    n_ctx = T_CTX // TM
    if len(xs) == 2:
        x_specs = [pl.BlockSpec((TM, D_MODEL), lambda i: (jnp.minimum(i, n_ctx - 1), 0)),
                   pl.BlockSpec((TM, D_MODEL), lambda i: (jnp.maximum(i - n_ctx, 0), 0))]
    else:
        x_specs = [pl.BlockSpec((TM, D_MODEL), lambda i: (i, 0))]
    return pl.pallas_call(
        functools.partial(_prenorm_kernel, split=len(xs) == 2),
        out_shape=jax.ShapeDtypeStruct((T_ALL, D_MODEL), BF16),
        grid=(T_ALL // TM,),
        in_specs=x_specs + [
            pl.BlockSpec((None, N_MOD, D_MODEL), lambda i: (_cond_row(i), 0, 0)),
            pl.BlockSpec((None, 1, D_MODEL), lambda i: (0, 0, 0)),
        ],
        out_specs=pl.BlockSpec((TM, D_MODEL), lambda i: (i, 0)),
        compiler_params=_cparams(("parallel",)),
        name="prenorm",
    )(*xs, mod_l, g)


def _cast_weight(src_ref, dst_ref, kind):
    w = src_ref[...]
    if kind == "plain":
        dst_ref[...] = w.astype(BF16)
    elif kind == "tail":
        n = w.shape[1]
        dst_ref[:, :n] = w.astype(BF16)
        dst_ref[:, n:] = jnp.zeros((w.shape[0], dst_ref.shape[1] - n), BF16)
    else:
        for hh in range(H_C):
            dst_ref[:, hh * HEAD_PAD:hh * HEAD_PAD + DQK_C] = w[:, hh * DQK_C:(hh + 1) * DQK_C].astype(BF16)
            dst_ref[:, hh * HEAD_PAD + DQK_C:(hh + 1) * HEAD_PAD] = jnp.zeros((w.shape[0], HEAD_PAD - DQK_C), BF16)


def _ffn_kernel(*refs, base, cast_kinds):
    it = iter(refs)
    h_ref, x_ref, mod_ref, wg_ref, wu_ref, wd_ref = [next(it) for _ in range(6)]
    srcs = [next(it) for _ in cast_kinds]
    o_ref = next(it)
    dsts = [next(it) for _ in cast_kinds]
    f = pl.program_id(1)

    @pl.when(f == 0)
    def _():
        o_ref[...] = jnp.zeros_like(o_ref)

    h = h_ref[...]
    gate = _dot(h, wg_ref[...])
    up = _dot(h, wu_ref[...])
    a = (_silu(gate) * up).astype(BF16)
    for c in range(D_MODEL // FFN_DC):
        cs = slice(c * FFN_DC, (c + 1) * FFN_DC)
        o_ref[:, cs] += (0.5 * mod_ref[base + 2:base + 3, cs]) * _dot(a, wd_ref[:, cs])

    for src, dst, kind in zip(srcs, dsts, cast_kinds):
        _cast_weight(src, dst, kind)

    for c in range(D_MODEL // FFN_XC):
        @pl.when(f == c)
        def _(c=c):
            o_ref[:, c * FFN_XC:(c + 1) * FFN_XC] += x_ref[...]


def _cast_rows(total, steps):
    rows = BF16_ROWS
    while total % rows or total // rows > steps:
        rows += BF16_ROWS
    return rows


def _ffn(h, x, x_tile0, tile0, tiles, mod_l, w_b, k, casts):
    tm, tf = FFN_TM, FFN_TF
    nf = D_FF // tf
    assert nf >= D_MODEL // FFN_XC
    wg, wu, wd = w_b
    in_specs = [
        pl.BlockSpec((tm, D_MODEL), lambda i, f: (tile0 + i, 0)),
        pl.BlockSpec((tm, FFN_XC), lambda i, f: (x_tile0 + i, jnp.minimum(f, D_MODEL // FFN_XC - 1))),
        pl.BlockSpec((None, N_MOD, D_MODEL), lambda i, f: (_cond_row(tile0 + i, tm), 0, 0)),
        pl.BlockSpec((D_MODEL, tf), lambda i, f: (0, f)),
        pl.BlockSpec((D_MODEL, tf), lambda i, f: (0, f)),
        pl.BlockSpec((tf, D_MODEL), lambda i, f: (f, 0)),
    ]
    args = [h, x, mod_l, wg, wu, wd]
    out_shape = [jax.ShapeDtypeStruct((tiles * tm, D_MODEL), F32)]
    out_specs = [pl.BlockSpec((tm, D_MODEL), lambda i, f: (i, 0))]
    for w, lead, kind in casts:
        n_rows, n_cols = w.shape[-2:]
        out_cols = {"plain": n_cols, "tail": -(-n_cols // LANE) * LANE, "heads": H_C * HEAD_PAD}[kind]
        rows = _cast_rows(n_rows, tiles * nf)
        last = n_rows // rows - 1
        in_specs.append(pl.BlockSpec((None,) * len(lead) + (rows, n_cols),
                                     lambda i, f, lead=lead, last=last: (*lead, jnp.minimum(i * nf + f, last), 0)))
        args.append(w)
        out_shape.append(jax.ShapeDtypeStruct((n_rows, out_cols), BF16))
        out_specs.append(pl.BlockSpec((rows, out_cols), lambda i, f, last=last: (jnp.minimum(i * nf + f, last), 0)))
    return pl.pallas_call(
        functools.partial(_ffn_kernel, base=6 * k, cast_kinds=tuple(kind for _, _, kind in casts)),
        out_shape=out_shape,
        grid=(tiles, nf),
        in_specs=in_specs,
        out_specs=out_specs,
        compiler_params=pltpu.CompilerParams(dimension_semantics=("arbitrary", "arbitrary"),
                                             vmem_limit_bytes=FFN_VMEM_LIMIT),
        name="ffn",
    )(*args)


IN_TM = SEQ


def _inproj_even_kernel(*refs, latent, e):
    x_ref, mod_ref, g_ref, w_ref, gq_ref, gk_ref = refs[:6]
    refs = refs[6:]
    if latent:
        cos_ref, sa_ref, sb_ref, q_ref, k_ref, v_ref, f_ref = refs
    else:
        q_ref, k_ref, v_ref, f_ref, nk_ref, nv_ref = refs[-6:]
        if e > 0:
            nk_ref[:e] = refs[0][...]
            nv_ref[:e] = refs[1][...]
        nk_ref, nv_ref = nk_ref.at[e], nv_ref.at[e]
    dh2 = 2 * DH_A
    h = _norm_mod(x_ref[...], g_ref[...], mod_ref, 3).astype(BF16)
    gains = (gq_ref[...] * (DH_A ** -0.5 * LOG2E), gk_ref[...])
    for part, (g, o_ref) in enumerate(zip(gains, (q_ref, k_ref))):
        for hh in range(H_A):
            res = _dot(h, w_ref[:, part * W_A + hh * dh2:part * W_A + (hh + 1) * dh2])
            for m in range(2):
                xc = res[:, m * DH_A:(m + 1) * DH_A]
                y = xc * _rms(xc, DH_A) * g
                if latent:
                    y = _rope(y, cos_ref[...], sa_ref[...], sb_ref[...], DH_A // 4)
                o_ref[:, hh * dh2 + m * DH_A:hh * dh2 + (m + 1) * DH_A] = y.astype(BF16)
                if part == 1 and not latent:
                    nk_ref[hh, :, m * DH_A:(m + 1) * DH_A] = y
    for hh in range(H_A):
        res = _dot(h, w_ref[:, 2 * W_A + hh * dh2:2 * W_A + (hh + 1) * dh2])
        v_ref[:, hh * dh2:(hh + 1) * dh2] = res.astype(BF16)
        if not latent:
            nv_ref[hh] = res
    f_ref[...] = _dot(h, w_ref[:, 3 * W_A:]).astype(BF16)


def _inproj_even(ps, x, mod_l, g_norm_l, w_in, gqk, e, tabs, prev_kv):
    tm = IN_TM
    t0 = ps.tile0 * (TM // tm)
    xt0 = t0 if x.shape[0] == T_ALL else 0
    in_specs = [
        pl.BlockSpec((tm, D_MODEL), lambda i: (xt0 + i, 0)),
        pl.BlockSpec((None, N_MOD, D_MODEL), lambda i: (_cond_row(t0 + i, tm), 0, 0)),
        pl.BlockSpec((None, 1, D_MODEL), lambda i: (1, 0, 0)),
        pl.BlockSpec((D_MODEL, 3 * W_A + W_B), lambda i: (0, 0), pipeline_mode=pl.Buffered(1)),
        pl.BlockSpec((None, None, 1, DH_A), lambda i: (e, 0, 0, 0)),
        pl.BlockSpec((None, None, 1, DH_A), lambda i: (e, 1, 0, 0)),
    ]
    args = [x, mod_l, g_norm_l, w_in, gqk, gqk]
    out_shape = [jax.ShapeDtypeStruct((ps.rows, W_A), BF16)] * 3 + [jax.ShapeDtypeStruct((ps.rows, W_B), BF16)]
    out_specs = [pl.BlockSpec((tm, W_A), lambda i: (i, 0))] * 3 + [pl.BlockSpec((tm, W_B), lambda i: (i, 0))]
    if ps.latent:
        tab_spec = pl.BlockSpec((tm, LANE), lambda i: (i % (DEC_SEQ // tm), 0))
        in_specs += [tab_spec] * 3
        args += list(tabs)
    else:
        kv_block = lambda n: pl.BlockSpec((None, n, H_A, SEQ, 2 * DH_A), lambda i: (i, 0, 0, 0, 0))
        out_shape += [jax.ShapeDtypeStruct((BATCH, e + 1, H_A, SEQ, 2 * DH_A), F32)] * 2
        out_specs += [kv_block(e + 1)] * 2
        if e > 0:
            in_specs += [kv_block(e)] * 2
            args += list(prev_kv)
    return pl.pallas_call(
        functools.partial(_inproj_even_kernel, latent=ps.latent, e=e),
        out_shape=out_shape,
        grid=(ps.rows // tm,),
        in_specs=in_specs,
        out_specs=out_specs,
        compiler_params=_cparams(("parallel",)),
        name="inproj_even",
    )(*args)


def _diff_attn_kernel(lam_ref, gsub_ref, q_ref, k_ref, v_ref, *rest, lam_init, heads, has_ctx, sub):
    if has_ctx:
        ck_ref, cv_ref, o_ref = rest
    else:
        (o_ref,) = rest
    lv = lam_ref[...]
    lam = (jnp.exp(jnp.sum(lv[0:1] * lv[1:2], axis=-1, keepdims=True))
           - jnp.exp(jnp.sum(lv[2:3] * lv[3:4], axis=-1, keepdims=True)) + lam_init)
    dh2 = 2 * DH_A
    gsub = gsub_ref[...] * (1.0 - lam_init)
    for hh in range(heads):
        hs = slice(hh * dh2, (hh + 1) * dh2)
        k = k_ref[:, hs]
        v = v_ref[:, hs]
        if has_ctx:
            ck = ck_ref[hh].astype(BF16)
            cv = cv_ref[hh].astype(BF16)
        for r in range(q_ref.shape[0] // sub):
            rs = slice(r * sub, (r + 1) * sub)
            q = q_ref[rs, hs]
            o = None
            for m in range(2):
                sl = slice(m * DH_A, (m + 1) * DH_A)
                s = _dot_nt(q[:, sl], k[:, sl])
                s2 = _dot_nt(q[:, sl], ck[:, sl]) if has_ctx else None
                e, e2, inv = _softmax_exp(s, s2, True)
                om = _dot(e.astype(BF16), v)
                if has_ctx:
                    om = om + _dot(e2.astype(BF16), cv)
                o = om * inv if m == 0 else o - om * (lam * inv)
            o_ref[rs, hs] = (o * _rms(o, dh2) * gsub).astype(BF16)


def _diff_attn_ctx(q, k, v, lam_vecs, gsub, e, lam_init):
    spec = pl.BlockSpec((SEQ, W_A), lambda b: (b, 0))
    return pl.pallas_call(
        functools.partial(_diff_attn_kernel, lam_init=lam_init, heads=H_A, has_ctx=False, sub=SEQ),
        out_shape=jax.ShapeDtypeStruct((T_CTX, W_A), BF16),
        grid=(BATCH,),
        in_specs=[
            pl.BlockSpec((None, 4, DH_A), lambda b: (e, 0, 0)),
            pl.BlockSpec((None, 1, 2 * DH_A), lambda b: (e, 0, 0)),
            spec, spec, spec,
        ],
        out_specs=spec,
        compiler_params=_cparams(("parallel",)),
        name="diff_attn_ctx",
    )(lam_vecs, gsub, q, k, v)


def _diff_attn_smp(q, k, v, cache_k, cache_v, lam_vecs, gsub, e, lam_init):
    tq, sub = 1024, 256
    w = 2 * DH_A
    nq = DEC_SEQ // tq
    cache_spec = pl.BlockSpec((None, None, 1, PAST_LEN, w), lambda b, h, i: (b, e, h, 0, 0))
    return pl.pallas_call(
        functools.partial(_diff_attn_kernel, lam_init=lam_init, heads=1, has_ctx=True, sub=sub),
        out_shape=jax.ShapeDtypeStruct((T_SMP, W_A), BF16),
        grid=(DEC_BATCH, H_A, nq),
        in_specs=[
            pl.BlockSpec((None, 4, DH_A), lambda b, h, i: (e, 0, 0)),
            pl.BlockSpec((None, 1, w), lambda b, h, i: (e, 0, 0)),
            pl.BlockSpec((tq, w), lambda b, h, i: (b * nq + i, h)),
            pl.BlockSpec((DEC_SEQ, w), lambda b, h, i: (b, h)),
            pl.BlockSpec((DEC_SEQ, w), lambda b, h, i: (b, h)),
            cache_spec, cache_spec,
        ],
        out_specs=pl.BlockSpec((tq, w), lambda b, h, i: (b * nq + i, h)),
        compiler_params=_cparams(("parallel", "parallel", "arbitrary")),
        name="diff_attn_smp",
    )(lam_vecs, gsub, q, k, v, cache_k, cache_v)


DFT_LO = 64


def _fourier_kernel(x_ref, bc_ref, bs_ref, c0_ref, s0_ref, c1_ref, s1_ref, o_ref, y1_ref, y2_ref, *, scale, tr):
    r = pl.program_id(1)

    @pl.when(r == 0)
    def _():
        x = x_ref[...]
        y1_ref[...] = _dot(x, bc_ref[...]).astype(BF16)
        y2_ref[...] = _dot(x, bs_ref[...]).astype(BF16)

    c0, s0 = c0_ref[...], s0_ref[...]
    cs, ss = [], []
    for jj in range(tr // DFT_LO):
        j1 = r * (tr // DFT_LO) + jj
        c1, s1 = c1_ref[pl.ds(j1, 1), :], s1_ref[pl.ds(j1, 1), :]
        cs.append((c1 * c0 - s1 * s0).astype(BF16))
        ss.append((s1 * c0 + c1 * s0).astype(BF16))
    out = _dot(jnp.concatenate(cs, axis=0), y1_ref[...]) - _dot(jnp.concatenate(ss, axis=0), y2_ref[...])
    o_ref[...] = (out * scale).astype(BF16)


def _fourier(f, s, dft_ch, dft_seq):
    tr = 256
    nb = f.shape[0] // s
    bc, bs = dft_ch
    full = lambda a: pl.BlockSpec(a.shape, lambda b, r: (0, 0))
    return pl.pallas_call(
        functools.partial(_fourier_kernel, scale=1.0 / math.sqrt(s * DG_B), tr=tr),
        out_shape=jax.ShapeDtypeStruct(f.shape, BF16),
        grid=(nb, s // tr),
        in_specs=[pl.BlockSpec((s, W_B), lambda b, r: (b, 0)), full(bc), full(bs)] + [full(t) for t in dft_seq],
        out_specs=pl.BlockSpec((tr, W_B), lambda b, r: (b * (s // tr) + r, 0)),
        scratch_shapes=[pltpu.VMEM((s, W_B), BF16), pltpu.VMEM((s, W_B), BF16)],
        compiler_params=_cparams(("parallel", "arbitrary")),
        name="fourier",
    )(f, bc, bs, *dft_seq)


def _dft_seq_tables(n):
    a0, a1 = _dft_angles(DFT_LO, 1, n), _dft_angles(n // DFT_LO, DFT_LO, n)
    return jnp.cos(a0), jnp.sin(a0), jnp.cos(a1), jnp.sin(a1)


def _dft_angles(rows, stride, n):
    j = jnp.arange(rows, dtype=jnp.int32)[:, None] * stride
    k = jnp.arange(n, dtype=jnp.int32)[None, :]
    return ((j * k) % n).astype(F32) * (2.0 * math.pi / n)


def _dft_mats(n):
    lo = min(n, 64)
    a0 = _dft_angles(lo, 1, n)
    c0, s0 = jnp.cos(a0), jnp.sin(a0)
    if lo == n:
        return c0, s0
    a1 = _dft_angles(n // lo, lo, n)
    c1, s1 = jnp.cos(a1)[:, None, :], jnp.sin(a1)[:, None, :]
    c = c1 * c0[None] - s1 * s0[None]
    s = s1 * c0[None] + c1 * s0[None]
    return c.reshape(n, n), s.reshape(n, n)


def _dft_channel_mats():
    c, s = _dft_mats(DG_B)
    eye = jnp.eye(G_B, dtype=F32)
    return jnp.kron(eye, c).astype(BF16), jnp.kron(eye, s).astype(BF16)


def _outproj_kernel(*refs, n_act, n_x):
    is_ctx = pl.program_id(0) < T_CTX // TM
    x = refs[0][...] if n_x == 1 else jnp.where(is_ctx, refs[0][...], refs[1][...])
    mod_ref, g_ref = refs[n_x:n_x + 2]
    act = refs[n_x + 2:n_x + 2 + 3 * n_act]
    o_ref, hn_ref = refs[n_x + 2 + 3 * n_act:]
    y = None
    for a in range(n_act):
        ctx_ref, smp_ref, w_ref = act[3 * a:3 * a + 3]
        lhs = jnp.where(is_ctx, ctx_ref[...], smp_ref[...])
        t = _dot(lhs, w_ref[...])
        y = t if y is None else y + t
    x_new = x + mod_ref[5:6, :] * y
    o_ref[...] = x_new
    hn_ref[...] = _norm_mod(x_new, g_ref[...], mod_ref, 6).astype(BF16)


def _outproj(xs, mod_l, g_norm_l, acts):
    n_ctx = T_CTX // TM
    row_spec = pl.BlockSpec((TM, D_MODEL), lambda i: (i, 0))
    ctx_rows = lambda n: pl.BlockSpec((TM, n), lambda i: (jnp.minimum(i, n_ctx - 1), 0))
    smp_rows = lambda n: pl.BlockSpec((TM, n), lambda i: (jnp.maximum(i - n_ctx, 0), 0))
    in_specs = ([row_spec] if len(xs) == 1 else [ctx_rows(D_MODEL), smp_rows(D_MODEL)]) + [
        pl.BlockSpec((None, N_MOD, D_MODEL), lambda i: (_cond_row(i), 0, 0)),
        pl.BlockSpec((None, 1, D_MODEL), lambda i: (2, 0, 0)),
    ]
    args = [*xs, mod_l, g_norm_l]
    for ctx, smp, w, rb in acts:
        kdim = ctx.shape[1]
        in_specs += [ctx_rows(kdim), smp_rows(kdim), pl.BlockSpec((kdim, D_MODEL), lambda i, rb=rb: (rb, 0))]
        args += [ctx, smp, w]
    return pl.pallas_call(
        functools.partial(_outproj_kernel, n_act=len(acts), n_x=len(xs)),
        out_shape=(jax.ShapeDtypeStruct((T_ALL, D_MODEL), F32), jax.ShapeDtypeStruct((T_ALL, D_MODEL), BF16)),
        grid=(T_ALL // TM,),
        in_specs=in_specs,
        out_specs=(row_spec, row_spec),
        compiler_params=_cparams(("parallel",)),
        name="outproj",
    )(*args)


MLA_TM = 256


def _mla_proj_kernel(x_ref, mod_ref, g_ref, win_ref, wuq_ref, wukv_ref, gql_ref, gkvl_ref, gq_ref, gk_ref,
                     cos_ref, sa_ref, sb_ref, q_ref, k_ref, v_ref, ckv_ref, kpe_ref):
    h = _norm_mod(x_ref[...], g_ref[...], mod_ref, 3).astype(BF16)
    res = _dot(h, win_ref[...])
    cq = res[:, :Q_LORA]
    cqn = (cq * _rms(cq, Q_LORA) * gql_ref[...]).astype(BF16)
    ckv = res[:, Q_LORA:Q_LORA + KV_LORA]
    ckvn = ckv * _rms(ckv, KV_LORA) * gkvl_ref[...]
    ckv_ref[...] = ckvn
    kpe = res[:, Q_LORA + KV_LORA:]
    kpe_ref[...] = kpe
    cos, sa, sb = cos_ref[...], sa_ref[...], sb_ref[...]
    gq, gk = gq_ref[...] * (DQK_C ** -0.5 * LOG2E), gk_ref[...]
    ss_pe = jnp.sum(kpe * kpe, axis=-1, keepdims=True)
    kr = _rope(kpe * gk[:, LANE:], cos, sa, sb, ROPE_C // 4)
    ckvb = ckvn.astype(BF16)
    for hh in range(H_C):
        hs = slice(hh * HEAD_PAD, (hh + 1) * HEAD_PAD)
        lo = slice(hh * HEAD_PAD, hh * HEAD_PAD + LANE)
        hi = slice(hh * HEAD_PAD + LANE, (hh + 1) * HEAD_PAD)
        xh = _dot(cqn, wuq_ref[:, hs])
        y = xh * _rms(xh, DQK_C) * gq
        q_ref[:, lo] = y[:, :LANE].astype(BF16)
        q_ref[:, hi] = _rope(y[:, LANE:], cos, sa, sb, ROPE_C // 4).astype(BF16)
        kv = _dot(ckvb, wukv_ref[:, hs])
        k_nope = kv[:, :LANE]
        ss = jnp.sum(k_nope * k_nope, axis=-1, keepdims=True) + ss_pe
        r = lax.rsqrt(ss * (1.0 / DQK_C) + EPS)
        k_ref[:, lo] = (k_nope * r * gk[:, :LANE]).astype(BF16)
        k_ref[:, hi] = (kr * r).astype(BF16)
        v_ref[:, hh * DV_C:(hh + 1) * DV_C] = kv[:, LANE:].astype(BF16)


def _mla_proj(x, mod_l, g_norm_l, w_in_pad, w_uq_pad, w_ukv, gq_lora, gkv_lora, gq_pad, gk_pad, o, tabs):
    tm = MLA_TM
    n_in = Q_LORA + KV_LORA + LANE
    resident = dict(pipeline_mode=pl.Buffered(1))
    tab_spec = pl.BlockSpec((tm, LANE), lambda i: (i, 0))
    row = lambda n: pl.BlockSpec((tm, n), lambda i: (i, 0))
    return pl.pallas_call(
        _mla_proj_kernel,
        out_shape=(jax.ShapeDtypeStruct((T_ALL, H_C * HEAD_PAD), BF16),
                   jax.ShapeDtypeStruct((T_ALL, H_C * HEAD_PAD), BF16),
                   jax.ShapeDtypeStruct((T_ALL, H_C * DV_C), BF16),
                   jax.ShapeDtypeStruct((T_ALL, KV_LORA), F32),
                   jax.ShapeDtypeStruct((T_ALL, LANE), F32)),
        grid=(T_ALL // tm,),
        in_specs=[
            row(D_MODEL),
            pl.BlockSpec((None, N_MOD, D_MODEL), lambda i: (_cond_row(i, tm), 0, 0)),
            pl.BlockSpec((None, 1, D_MODEL), lambda i: (1, 0, 0)),
            pl.BlockSpec((D_MODEL, n_in), lambda i: (0, 0), **resident),
            pl.BlockSpec((Q_LORA, H_C * HEAD_PAD), lambda i: (0, 0), **resident),
            pl.BlockSpec((KV_LORA, H_C * HEAD_PAD), lambda i: (0, 0), **resident),
            pl.BlockSpec((None, 1, Q_LORA), lambda i: (o, 0, 0)),
            pl.BlockSpec((None, 1, KV_LORA), lambda i: (o, 0, 0)),
            pl.BlockSpec((None, 1, HEAD_PAD), lambda i: (o, 0, 0)),
            pl.BlockSpec((None, 1, HEAD_PAD), lambda i: (o, 0, 0)),
            tab_spec, tab_spec, tab_spec,
        ],
        out_specs=(row(H_C * HEAD_PAD), row(H_C * HEAD_PAD), row(H_C * DV_C), row(KV_LORA), row(LANE)),
        compiler_params=_cparams(("parallel",)),
        name="mla_proj",
    )(x, mod_l, g_norm_l, w_in_pad, w_uq_pad, w_ukv, gq_lora, gkv_lora, gq_pad, gk_pad, *tabs)


MLA_HB = 4


def _mla_kv_kernel(ckv_ref, kpe_ref, w_ref, g_ref, *rest, use_rope):
    if use_rope:
        cos_ref, sa_ref, sb_ref, k_ref, v_ref = rest
    else:
        k_ref, v_ref = rest
    ckv = ckv_ref[...].astype(BF16)
    g = g_ref[...]
    kpe = kpe_ref[...]
    ss_pe = jnp.sum(kpe * kpe, axis=-1, keepdims=True)
    kr = kpe * g[:, LANE:]
    if use_rope:
        kr = _rope(kr, cos_ref[...], sa_ref[...], sb_ref[...], ROPE_C // 4)
    for hh in range(MLA_HB):
        res = _dot(ckv, w_ref[:, hh * 2 * LANE:(hh + 1) * 2 * LANE])
        k_nope = res[:, :LANE]
        ss = jnp.sum(k_nope * k_nope, axis=-1, keepdims=True) + ss_pe
        r = lax.rsqrt(ss * (1.0 / DQK_C) + EPS)
        k_ref[:, hh * HEAD_PAD:hh * HEAD_PAD + LANE] = (k_nope * r * g[:, :LANE]).astype(BF16)
        k_ref[:, hh * HEAD_PAD + LANE:(hh + 1) * HEAD_PAD] = (kr * r).astype(BF16)
        v_ref[:, hh * DV_C:(hh + 1) * DV_C] = res[:, LANE:].astype(BF16)


def _mla_kv(ckv, kpe_pad, tile0, rows, w_ukv, gk_pad, o, tabs):
    use_rope = tabs is not None
    in_specs = [
        pl.BlockSpec((TM, KV_LORA), lambda i, j: (tile0 + i, 0)),
        pl.BlockSpec((TM, LANE), lambda i, j: (tile0 + i, 0)),
        pl.BlockSpec((KV_LORA, MLA_HB * 2 * LANE), lambda i, j: (0, j)),
        pl.BlockSpec((None, 1, HEAD_PAD), lambda i, j: (o, 0, 0)),
    ]
    args = [ckv, kpe_pad, w_ukv, gk_pad]
    if use_rope:
        in_specs += [pl.BlockSpec((TM, LANE), lambda i, j: (i % (DEC_SEQ // TM), 0))] * 3
        args += list(tabs)
    return pl.pallas_call(
        functools.partial(_mla_kv_kernel, use_rope=use_rope),
        out_shape=(jax.ShapeDtypeStruct((rows, H_C * HEAD_PAD), BF16),
                   jax.ShapeDtypeStruct((rows, H_C * DV_C), BF16)),
        grid=(rows // TM, H_C // MLA_HB),
        in_specs=in_specs,
        out_specs=(pl.BlockSpec((TM, MLA_HB * HEAD_PAD), lambda i, j: (i, j)),
                   pl.BlockSpec((TM, MLA_HB * DV_C), lambda i, j: (i, j))),
        compiler_params=_cparams(("parallel", "parallel")),
        name="mla_kv",
    )(*args)


def _mla_attn_kernel(q_ref, k_ref, v_ref, *rest, heads, has_ctx, sub):
    if has_ctx:
        ck_ref, cv_ref, o_ref, v1_ref, cv1_ref = rest
    else:
        o_ref, v1_ref = rest
    def with_ones(dst_ref, hh, v):
        dst_ref[hh, :, :DV_C] = v
        dst_ref[hh, :, DV_C:] = jnp.ones_like(v)
        return dst_ref[hh]

    for hh in range(heads):
        ks = slice(hh * HEAD_PAD, (hh + 1) * HEAD_PAD)
        vs = slice(hh * DV_C, (hh + 1) * DV_C)
        v1 = with_ones(v1_ref, hh, v_ref[:, vs])
        if has_ctx:
            cv1 = with_ones(cv1_ref, hh, cv_ref[:, vs])
        for r in range(q_ref.shape[0] // sub):
            rs = slice(r * sub, (r + 1) * sub)
            q = q_ref[rs, ks]
            s = _dot_nt(q, k_ref[:, ks])
            s2 = _dot_nt(q, ck_ref[:, ks]) if has_ctx else None
            e, e2, _ = _softmax_exp(s, s2, False)
            o = _dot(e.astype(BF16), v1)
            if has_ctx:
                o = o + _dot(e2.astype(BF16), cv1)
            o_ref[rs, vs] = (o[:, :DV_C] * (1.0 / o[:, DV_C:])).astype(BF16)


def _mla_attn_ctx(q, k, v):
    heads = H_C
    return pl.pallas_call(
        functools.partial(_mla_attn_kernel, heads=heads, has_ctx=False, sub=SEQ),
        out_shape=jax.ShapeDtypeStruct((T_CTX, H_C * DV_C), BF16),
        grid=(BATCH, H_C // heads),
        in_specs=[
            pl.BlockSpec((SEQ, heads * HEAD_PAD), lambda b, j: (b, j)),
            pl.BlockSpec((SEQ, heads * HEAD_PAD), lambda b, j: (b, j)),
            pl.BlockSpec((SEQ, heads * DV_C), lambda b, j: (b, j)),
        ],
        out_specs=pl.BlockSpec((SEQ, heads * DV_C), lambda b, j: (b, j)),
        scratch_shapes=[pltpu.VMEM((heads, SEQ, 2 * DV_C), BF16)],
        compiler_params=_cparams(("parallel", "parallel")),
        name="mla_attn_ctx",
    )(q, k, v)


def _mla_attn_smp(q, k, v, ck, cv):
    tq, sub = DEC_SEQ, 256
    nq = DEC_SEQ // tq
    q0, s0 = T_CTX // tq, T_CTX // DEC_SEQ
    return pl.pallas_call(
        functools.partial(_mla_attn_kernel, heads=1, has_ctx=True, sub=sub),
        out_shape=jax.ShapeDtypeStruct((T_SMP, H_C * DV_C), BF16),
        grid=(DEC_BATCH, H_C, nq),
        in_specs=[
            pl.BlockSpec((tq, HEAD_PAD), lambda b, h, i: (q0 + b * nq + i, h)),
            pl.BlockSpec((DEC_SEQ, HEAD_PAD), lambda b, h, i: (s0 + b, h)),
            pl.BlockSpec((DEC_SEQ, DV_C), lambda b, h, i: (s0 + b, h)),
            pl.BlockSpec((PAST_LEN, HEAD_PAD), lambda b, h, i: (b, h)),
            pl.BlockSpec((PAST_LEN, DV_C), lambda b, h, i: (b, h)),
        ],
        out_specs=pl.BlockSpec((tq, DV_C), lambda b, h, i: (b * nq + i, h)),
        scratch_shapes=[pltpu.VMEM((1, DEC_SEQ, 2 * DV_C), BF16), pltpu.VMEM((1, PAST_LEN, 2 * DV_C), BF16)],
        compiler_params=_cparams(("parallel", "parallel", "arbitrary")),
        name="mla_attn_smp",
    )(q, k, v, ck, cv)


def _axial_angles(rows, rot_dim):
    row = jnp.repeat(jnp.arange(rows, dtype=F32), GRID_W)
    col = jnp.tile(jnp.arange(GRID_W, dtype=F32), rows)
    half = rot_dim // 2
    inv_freq = ROPE_BASE ** (-jnp.arange(0, half, 2, dtype=F32) / half)
    ang_r = row[:, None] * inv_freq[None, :]
    ang_c = col[:, None] * inv_freq[None, :]
    return jnp.concatenate([ang_r, ang_r, ang_c, ang_c], axis=-1)


def _rope_tables(rows, rot_dim):
    ang = _axial_angles(rows, rot_dim)
    q = rot_dim // 4
    blk = (jnp.arange(rot_dim) // q) % 2
    cos, sin = jnp.cos(ang), jnp.sin(ang)
    sin_a = jnp.where(blk == 0, -sin, 0.0)
    sin_b = jnp.where(blk == 1, sin, 0.0)

    def full(t, fill):
        return jnp.pad(t, ((0, 0), (0, LANE - rot_dim)), constant_values=fill)

    return full(cos, 1.0), full(sin_a, 0.0), full(sin_b, 0.0)


def _joint_tables(tabs):
    def joint(t, fill):
        return jnp.concatenate([jnp.full((T_CTX, LANE), fill, F32), jnp.tile(t, (DEC_BATCH, 1))], axis=0)
    return joint(tabs[0], 1.0), joint(tabs[1], 0.0), joint(tabs[2], 0.0)


def kernel(x_prompt, x_sample, c, cache_diff_k, cache_diff_v, cache_mla_ckv, cache_mla_kpe, c_ctx, w_mod, b_mod, g_norm, w_ffn_gate, w_ffn_up, w_ffn_down, w_in_ab, w_out_ab, g_qk_diff, diff_lambda, g_diff_sub, w_in_mla, g_q_lora, w_uq, g_kv_lora, w_ukv, g_qk_mla, w_o_mla):
    rows = DEC_SEQ // GRID_W
    tabs_a = _rope_tables(rows, DH_A)
    tabs_c_joint = _joint_tables(_rope_tables(rows, ROPE_C))
    dft_ch = _dft_channel_mats()
    dft_ctx, dft_smp = _dft_seq_tables(SEQ), _dft_seq_tables(DEC_SEQ)
    p_ctx, p_smp = _Pass(False), _Pass(True)

    w_ffn = (w_ffn_gate[0, 0].astype(BF16), w_ffn_up[0, 0].astype(BF16), w_ffn_down[0, 0].astype(BF16))

    def ffn_casts(l, k):
        return [(w_ffn_gate, (l, k), "plain"), (w_ffn_up, (l, k), "plain"), (w_ffn_down, (l, k), "plain")]

    def mixer_casts(l):
        j = l // 2
        if l % 2 == 0:
            return [(w_in_ab, (j,), "plain"), (w_out_ab, (j,), "plain")]
        return [(w_in_mla, (j,), "tail"), (w_uq, (j,), "heads"), (w_ukv, (j,), "plain"), (w_o_mla, (j,), "plain")]

    g_qk_mla_pad = jnp.pad(g_qk_mla, ((0, 0), (0, 0), (0, HEAD_PAD - DQK_C)))[:, :, None, :]
    gq_mla, gk_mla = g_qk_mla_pad[:, 0], g_qk_mla_pad[:, 1]
    g_norm4 = g_norm[:, :, None, :]
    gqk_diff = g_qk_diff[:, :, None, :]
    g_sub3 = g_diff_sub[:, None, :]
    g_q_lora3, g_kv_lora3 = g_q_lora[:, None, :], g_kv_lora[:, None, :]

    cond8 = jnp.concatenate([c_ctx[None, :], c, jnp.zeros((8 - N_COND, D_MODEL), F32)], axis=0)
    mod = _modulation(cond8, w_mod, b_mod)[:, :N_COND].reshape(DEPTH, N_COND, N_MOD, D_MODEL)

    x_ctx, x_smp = x_prompt.reshape(T_CTX, D_MODEL), x_sample.reshape(T_SMP, D_MODEL)
    n_ctx, n_smp, n_all = T_CTX // FFN_TM, T_SMP // FFN_TM, T_ALL // FFN_TM
    new_kv = None
    mla_ckv, mla_kpe = [], []

    for l in range(DEPTH):
        mod_l = mod[l]
        if l == 0:
            h = _prenorm((x_ctx, x_smp), mod_l, g_norm4[l])
            xc, *cast = _ffn(h, x_ctx, 0, 0, n_ctx, mod_l, w_ffn, 0, ffn_casts(l, 1) + mixer_casts(l)[:1])
            xs, w_out0 = _ffn(h, x_smp, 0, n_ctx, n_smp, mod_l, w_ffn, 0, mixer_casts(l)[1:])
            w_ffn, w_mix, xs_l = cast[:3], [cast[3], w_out0], (xc, xs)
        else:
            h = _prenorm((x,), mod_l, g_norm4[l])
            x, *w_ffn = _ffn(h, x, 0, 0, n_all, mod_l, w_ffn, 0, ffn_casts(l, 1))
            xs_l = (x, x)
        if l % 2 == 0:
            e = l // 2
            lam_init = 0.8 - 0.6 * math.exp(-0.3 * l)
            w_in_b, w_out_b = w_mix
            q_c, k_c, v_c, f_c, *new_kv = _inproj_even(p_ctx, xs_l[0], mod_l, g_norm4[l], w_in_b, gqk_diff, e,
                                                      None, new_kv)
            q_s, k_s, v_s, f_s = _inproj_even(p_smp, xs_l[1], mod_l, g_norm4[l], w_in_b, gqk_diff, e, tabs_a, None)
            o_ctx = _diff_attn_ctx(q_c, k_c, v_c, diff_lambda, g_sub3, e, lam_init)
            o_smp = _diff_attn_smp(q_s, k_s, v_s, cache_diff_k, cache_diff_v, diff_lambda, g_sub3, e, lam_init)
            fo_ctx = _fourier(f_c, SEQ, dft_ch, dft_ctx)
            fo_smp = _fourier(f_s, DEC_SEQ, dft_ch, dft_smp)
            x, h = _outproj(xs_l if l == 0 else (x,), mod_l, g_norm4[l],
                            [(o_ctx, o_smp, w_out_b, 0), (fo_ctx, fo_smp, w_out_b, W_A // W_B)])
        else:
            o = l // 2
            w_in_mla_b, w_uq_b, w_ukv_b, w_o_b = w_mix
            q, k, v, ckvn, kpe_pad = _mla_proj(x, mod_l, g_norm4[l], w_in_mla_b, w_uq_b, w_ukv_b, g_q_lora3,
                                               g_kv_lora3, gq_mla, gk_mla, o, tabs_c_joint)
            cache_ckv = cache_mla_ckv[:, o].reshape(DEC_BATCH * PAST_LEN, KV_LORA)
            cache_kpe = jnp.pad(cache_mla_kpe[:, o].reshape(DEC_BATCH * PAST_LEN, ROPE_C),
                                ((0, 0), (0, LANE - ROPE_C)))
            ck, cv = _mla_kv(cache_ckv, cache_kpe, 0, DEC_BATCH * PAST_LEN, w_ukv_b, gk_mla, o, None)
            a_ctx = _mla_attn_ctx(q, k, v)
            a_smp = _mla_attn_smp(q, k, v, ck, cv)
            x, h = _outproj((x,), mod_l, g_norm4[l], [(a_ctx, a_smp, w_o_b, 0)])
            mla_ckv.append(ckvn[:T_CTX].reshape(BATCH, SEQ, KV_LORA))
            mla_kpe.append(kpe_pad[:T_CTX, :ROPE_C].reshape(BATCH, SEQ, ROPE_C))
        if l + 1 < DEPTH:
            x, *cast = _ffn(h, x, 0, 0, n_all, mod_l, w_ffn, 1, ffn_casts(l + 1, 0) + mixer_casts(l + 1))
            w_ffn, w_mix = cast[:3], cast[3:]
        else:
            (y_ctx,) = _ffn(h, x, 0, 0, n_ctx, mod_l, w_ffn, 1, [])
            (y_smp,) = _ffn(h, x, n_ctx, n_ctx, n_smp, mod_l, w_ffn, 1, [])

    y_prompt = y_ctx.reshape(BATCH, SEQ, D_MODEL)
    y_sample = y_smp.reshape(DEC_BATCH, DEC_SEQ, D_MODEL)
    return (y_prompt, y_sample, new_kv[0], new_kv[1], jnp.stack(mla_ckv, axis=1), jnp.stack(mla_kpe, axis=1))
```

```python
import functools
import math

import jax
import jax.numpy as jnp
from jax import lax
from jax.experimental import pallas as pl
from jax.experimental.pallas import tpu as pltpu

F32 = jnp.float32
BF16 = jnp.bfloat16

D_MODEL = 2048
BATCH = 32
SEQ = 256
DEPTH = 4
DEC_BATCH = 2
DEC_SEQ = 2048
PAST_LEN = 256
GRID_W = 64
N_EVEN = 2
N_ODD = 2
H_A = 6
DH_A = 128
W_A = H_A * 2 * DH_A
G_B = 4
DG_B = 128
W_B = G_B * DG_B
H_C = 16
Q_LORA = 1536
KV_LORA = 512
NOPE_C = 128
ROPE_C = 64
DQK_C = NOPE_C + ROPE_C
DV_C = 128
D_FF = 5632
N_MOD = 9
ROPE_BASE = 10000.0
EPS = 1e-6
LOG2E = 1.4426950408889634

T_CTX = BATCH * SEQ
T_SMP = DEC_BATCH * DEC_SEQ
T_ALL = T_CTX + T_SMP
N_COND = 1 + DEC_BATCH
LANE = 128
HEAD_PAD = 256
TM = 512

VMEM_LIMIT = 52 * 1024 * 1024
FFN_TM = 1024
FFN_VMEM_LIMIT = 60 * 1024 * 1024


def _cparams(sem):
    return pltpu.CompilerParams(dimension_semantics=sem, vmem_limit_bytes=VMEM_LIMIT)


def _cond_row(i, tm=TM):
    n_ctx = T_CTX // tm
    per = DEC_SEQ // tm
    return jnp.where(i < n_ctx, 0, 1 + (i - n_ctx) // per)


class _Pass:
    def __init__(self, latent):
        self.latent = latent
        self.rows = T_SMP if latent else T_CTX
        self.tile0 = (T_CTX // TM) if latent else 0
        self.tiles = self.rows // TM
        self.seq = DEC_SEQ if latent else SEQ
        self.batch = DEC_BATCH if latent else BATCH

    def cond(self, i):
        return 1 + i // (DEC_SEQ // TM) if self.latent else 0


def _dot(a, b):
    return jnp.dot(a, b, preferred_element_type=F32)


def _dot_nt(a, b):
    return lax.dot_general(a, b, (((1,), (1,)), ((), ())), preferred_element_type=F32)


def _silu(x):
    return x / (1.0 + jnp.exp(-x))


def _rms(x, denom):
    return lax.rsqrt(jnp.sum(x * x, axis=-1, keepdims=True) * (1.0 / denom) + EPS)


def _norm_mod(x, g, mod_ref, base):
    sh = mod_ref[base:base + 1, :]
    sc = mod_ref[base + 1:base + 2, :]
    return (x * _rms(x, D_MODEL) * g) * (1.0 + sc) + sh


def _rope(y, cos, sin_a, sin_b, q):
    return y * cos + pltpu.roll(y, LANE - q, 1) * sin_a + pltpu.roll(y, q, 1) * sin_b


def _softmax_exp(s, s2, need_sum):
    m = jnp.max(s, axis=-1, keepdims=True)
    if s2 is not None:
        m = jnp.maximum(m, jnp.max(s2, axis=-1, keepdims=True))
    e = jnp.exp2(s - m)
    e2 = jnp.exp2(s2 - m) if s2 is not None else None
    if not need_sum:
        return e, e2, None
    l = jnp.sum(e, axis=-1, keepdims=True)
    if s2 is not None:
        l = l + jnp.sum(e2, axis=-1, keepdims=True)
    return e, e2, 1.0 / l


def _mod_kernel(c_ref, w_ref, b_ref, o_ref):
    a = _silu(c_ref[...]).astype(BF16)
    o_ref[...] = _dot(a, w_ref[...].astype(BF16)) + b_ref[...]


def _modulation(cond8, w_mod, b_mod):
    tn = 1024
    n = N_MOD * D_MODEL
    return pl.pallas_call(
        _mod_kernel,
        out_shape=jax.ShapeDtypeStruct((DEPTH, 8, n), F32),
        grid=(DEPTH, n // tn),
        in_specs=[
            pl.BlockSpec((8, D_MODEL), lambda l, j: (0, 0)),
            pl.BlockSpec((None, D_MODEL, tn), lambda l, j: (l, 0, j)),
            pl.BlockSpec((None, 1, tn), lambda l, j: (l, 0, j)),
        ],
        out_specs=pl.BlockSpec((None, 8, tn), lambda l, j: (l, 0, j)),
        compiler_params=_cparams(("parallel", "parallel")),
        name="modulation",
    )(cond8, w_mod, b_mod.reshape(DEPTH, 1, n))


FFN_TF = 512
FFN_XC = 256
FFN_DC = 512
BF16_ROWS = 16


def _prenorm_kernel(*refs, split):
    if split:
        xc_ref, xs_ref, mod_ref, g_ref, h_ref = refs
        x = jnp.where(pl.program_id(0) < T_CTX // TM, xc_ref[...], xs_ref[...])
    else:
        x_ref, mod_ref, g_ref, h_ref = refs
        x = x_ref[...]
    h_ref[...] = _norm_mod(x, g_ref[...], mod_ref, 0).astype(BF16)


def _prenorm(xs, mod_l, g):
    n_ctx = T_CTX // TM
    if len(xs) == 2:
        x_specs = [pl.BlockSpec((TM, D_MODEL), lambda i: (jnp.minimum(i, n_ctx - 1), 0)),
                   pl.BlockSpec((TM, D_MODEL), lambda i: (jnp.maximum(i - n_ctx, 0), 0))]
    else:
        x_specs = [pl.BlockSpec((TM, D_MODEL), lambda i: (i, 0))]
    return pl.pallas_call(
        functools.partial(_prenorm_kernel, split=len(xs) == 2),
        out_shape=jax.ShapeDtypeStruct((T_ALL, D_MODEL), BF16),
        grid=(T_ALL // TM,),
        in_specs=x_specs + [
            pl.BlockSpec((None, N_MOD, D_MODEL), lambda i: (_cond_row(i), 0, 0)),
            pl.BlockSpec((None, 1, D_MODEL), lambda i: (0, 0, 0)),
        ],
        out_specs=pl.BlockSpec((TM, D_MODEL), lambda i: (i, 0)),
        compiler_params=_cparams(("parallel",)),
        name="prenorm",
    )(*xs, mod_l, g)


def _cast_weight(src_ref, dst_ref, kind):
    w = src_ref[...]
    if kind == "plain":
        dst_ref[...] = w.astype(BF16)
    elif kind == "tail":
        n = w.shape[1]
        dst_ref[:, :n] = w.astype(BF16)
        dst_ref[:, n:] = jnp.zeros((w.shape[0], dst_ref.shape[1] - n), BF16)
    else:
        for hh in range(H_C):
            dst_ref[:, hh * HEAD_PAD:hh * HEAD_PAD + DQK_C] = w[:, hh * DQK_C:(hh + 1) * DQK_C].astype(BF16)
            dst_ref[:, hh * HEAD_PAD + DQK_C:(hh + 1) * HEAD_PAD] = jnp.zeros((w.shape[0], HEAD_PAD - DQK_C), BF16)


def _ffn_kernel(*refs, base, cast_kinds):
    it = iter(refs)
    h_ref, x_ref, mod_ref, wg_ref, wu_ref, wd_ref = [next(it) for _ in range(6)]
    srcs = [next(it) for _ in cast_kinds]
    o_ref = next(it)
    dsts = [next(it) for _ in cast_kinds]
    f = pl.program_id(1)

    @pl.when(f == 0)
    def _():
        o_ref[...] = jnp.zeros_like(o_ref)

    h = h_ref[...]
    gate = _dot(h, wg_ref[...])
    up = _dot(h, wu_ref[...])
    a = (_silu(gate) * up).astype(BF16)
    for c in range(D_MODEL // FFN_DC):
        cs = slice(c * FFN_DC, (c + 1) * FFN_DC)
        o_ref[:, cs] += (0.5 * mod_ref[base + 2:base + 3, cs]) * _dot(a, wd_ref[:, cs])

    for src, dst, kind in zip(srcs, dsts, cast_kinds):
        _cast_weight(src, dst, kind)

    for c in range(D_MODEL // FFN_XC):
        @pl.when(f == c)
        def _(c=c):
            o_ref[:, c * FFN_XC:(c + 1) * FFN_XC] += x_ref[...]


def _cast_rows(total, steps):
    rows = BF16_ROWS
    while total % rows or total // rows > steps:
        rows += BF16_ROWS
    return rows


def _ffn(h, x, x_tile0, tile0, tiles, mod_l, w_b, k, casts):
    tm, tf = FFN_TM, FFN_TF
    nf = D_FF // tf
    assert nf >= D_MODEL // FFN_XC
    wg, wu, wd = w_b
    in_specs = [
        pl.BlockSpec((tm, D_MODEL), lambda i, f: (tile0 + i, 0)),
        pl.BlockSpec((tm, FFN_XC), lambda i, f: (x_tile0 + i, jnp.minimum(f, D_MODEL // FFN_XC - 1))),
        pl.BlockSpec((None, N_MOD, D_MODEL), lambda i, f: (_cond_row(tile0 + i, tm), 0, 0)),
        pl.BlockSpec((D_MODEL, tf), lambda i, f: (0, f)),
        pl.BlockSpec((D_MODEL, tf), lambda i, f: (0, f)),
        pl.BlockSpec((tf, D_MODEL), lambda i, f: (f, 0)),
    ]
    args = [h, x, mod_l, wg, wu, wd]
    out_shape = [jax.ShapeDtypeStruct((tiles * tm, D_MODEL), F32)]
    out_specs = [pl.BlockSpec((tm, D_MODEL), lambda i, f: (i, 0))]
    for w, lead, kind in casts:
        n_rows, n_cols = w.shape[-2:]
        out_cols = {"plain": n_cols, "tail": -(-n_cols // LANE) * LANE, "heads": H_C * HEAD_PAD}[kind]
        rows = _cast_rows(n_rows, tiles * nf)
        last = n_rows // rows - 1
        in_specs.append(pl.BlockSpec((None,) * len(lead) + (rows, n_cols),
                                     lambda i, f, lead=lead, last=last: (*lead, jnp.minimum(i * nf + f, last), 0)))
        args.append(w)
        out_shape.append(jax.ShapeDtypeStruct((n_rows, out_cols), BF16))
        out_specs.append(pl.BlockSpec((rows, out_cols), lambda i, f, last=last: (jnp.minimum(i * nf + f, last), 0)))
    return pl.pallas_call(
        functools.partial(_ffn_kernel, base=6 * k, cast_kinds=tuple(kind for _, _, kind in casts)),
        out_shape=out_shape,
        grid=(tiles, nf),
        in_specs=in_specs,
        out_specs=out_specs,
        compiler_params=pltpu.CompilerParams(dimension_semantics=("arbitrary", "arbitrary"),
                                             vmem_limit_bytes=FFN_VMEM_LIMIT),
        name="ffn",
    )(*args)


IN_TM = SEQ


def _inproj_even_kernel(*refs, latent, e):
    x_ref, mod_ref, g_ref, w_ref, gq_ref, gk_ref = refs[:6]
    refs = refs[6:]
    if latent:
        cos_ref, sa_ref, sb_ref, q_ref, k_ref, v_ref, f_ref = refs
    else:
        q_ref, k_ref, v_ref, f_ref, nk_ref, nv_ref = refs[-6:]
        if e > 0:
            nk_ref[:e] = refs[0][...]
            nv_ref[:e] = refs[1][...]
        nk_ref, nv_ref = nk_ref.at[e], nv_ref.at[e]
    dh2 = 2 * DH_A
    h = _norm_mod(x_ref[...], g_ref[...], mod_ref, 3).astype(BF16)
    gains = (gq_ref[...] * (DH_A ** -0.5 * LOG2E), gk_ref[...])
    for part, (g, o_ref) in enumerate(zip(gains, (q_ref, k_ref))):
        for hh in range(H_A):
            res = _dot(h, w_ref[:, part * W_A + hh * dh2:part * W_A + (hh + 1) * dh2])
            for m in range(2):
                xc = res[:, m * DH_A:(m + 1) * DH_A]
                y = xc * _rms(xc, DH_A) * g
                if latent:
                    y = _rope(y, cos_ref[...], sa_ref[...], sb_ref[...], DH_A // 4)
                o_ref[:, hh * dh2 + m * DH_A:hh * dh2 + (m + 1) * DH_A] = y.astype(BF16)
                if part == 1 and not latent:
                    nk_ref[hh, :, m * DH_A:(m + 1) * DH_A] = y
    for hh in range(H_A):
        res = _dot(h, w_ref[:, 2 * W_A + hh * dh2:2 * W_A + (hh + 1) * dh2])
        v_ref[:, hh * dh2:(hh + 1) * dh2] = res.astype(BF16)
        if not latent:
            nv_ref[hh] = res
    f_ref[...] = _dot(h, w_ref[:, 3 * W_A:]).astype(BF16)


def _inproj_even(ps, x, mod_l, g_norm_l, w_in, gqk, e, tabs, prev_kv):
    tm = IN_TM
    t0 = ps.tile0 * (TM // tm)
    xt0 = t0 if x.shape[0] == T_ALL else 0
    in_specs = [
        pl.BlockSpec((tm, D_MODEL), lambda i: (xt0 + i, 0)),
        pl.BlockSpec((None, N_MOD, D_MODEL), lambda i: (_cond_row(t0 + i, tm), 0, 0)),
        pl.BlockSpec((None, 1, D_MODEL), lambda i: (1, 0, 0)),
        pl.BlockSpec((D_MODEL, 3 * W_A + W_B), lambda i: (0, 0), pipeline_mode=pl.Buffered(1)),
        pl.BlockSpec((None, None, 1, DH_A), lambda i: (e, 0, 0, 0)),
        pl.BlockSpec((None, None, 1, DH_A), lambda i: (e, 1, 0, 0)),
    ]
    args = [x, mod_l, g_norm_l, w_in, gqk, gqk]
    out_shape = [jax.ShapeDtypeStruct((ps.rows, W_A), BF16)] * 3 + [jax.ShapeDtypeStruct((ps.rows, W_B), BF16)]
    out_specs = [pl.BlockSpec((tm, W_A), lambda i: (i, 0))] * 3 + [pl.BlockSpec((tm, W_B), lambda i: (i, 0))]
    if ps.latent:
        tab_spec = pl.BlockSpec((tm, LANE), lambda i: (i % (DEC_SEQ // tm), 0))
        in_specs += [tab_spec] * 3
        args += list(tabs)
    else:
        kv_block = lambda n: pl.BlockSpec((None, n, H_A, SEQ, 2 * DH_A), lambda i: (i, 0, 0, 0, 0))
        out_shape += [jax.ShapeDtypeStruct((BATCH, e + 1, H_A, SEQ, 2 * DH_A), F32)] * 2
        out_specs += [kv_block(e + 1)] * 2
        if e > 0:
            in_specs += [kv_block(e)] * 2
            args += list(prev_kv)
    return pl.pallas_call(
        functools.partial(_inproj_even_kernel, latent=ps.latent, e=e),
        out_shape=out_shape,
        grid=(ps.rows // tm,),
        in_specs=in_specs,
        out_specs=out_specs,
        compiler_params=_cparams(("parallel",)),
        name="inproj_even",
    )(*args)


def _diff_attn_kernel(lam_ref, gsub_ref, q_ref, k_ref, v_ref, *rest, lam_init, heads, has_ctx, sub):
    if has_ctx:
        ck_ref, cv_ref, o_ref = rest
    else:
        (o_ref,) = rest
    lv = lam_ref[...]
    lam = (jnp.exp(jnp.sum(lv[0:1] * lv[1:2], axis=-1, keepdims=True))
           - jnp.exp(jnp.sum(lv[2:3] * lv[3:4], axis=-1, keepdims=True)) + lam_init)
    dh2 = 2 * DH_A
    gsub = gsub_ref[...] * (1.0 - lam_init)
    for hh in range(heads):
        hs = slice(hh * dh2, (hh + 1) * dh2)
        k = k_ref[:, hs]
        v = v_ref[:, hs]
        if has_ctx:
            ck = ck_ref[hh].astype(BF16)
            cv = cv_ref[hh].astype(BF16)
        for r in range(q_ref.shape[0] // sub):
            rs = slice(r * sub, (r + 1) * sub)
            q = q_ref[rs, hs]
            o = None
            for m in range(2):
                sl = slice(m * DH_A, (m + 1) * DH_A)
                s = _dot_nt(q[:, sl], k[:, sl])
                s2 = _dot_nt(q[:, sl], ck[:, sl]) if has_ctx else None
                e, e2, inv = _softmax_exp(s, s2, True)
                om = _dot(e.astype(BF16), v)
                if has_ctx:
                    om = om + _dot(e2.astype(BF16), cv)
                o = om * inv if m == 0 else o - om * (lam * inv)
            o_ref[rs, hs] = (o * _rms(o, dh2) * gsub).astype(BF16)


def _diff_attn_ctx(q, k, v, lam_vecs, gsub, e, lam_init):
    spec = pl.BlockSpec((SEQ, W_A), lambda b: (b, 0))
    return pl.pallas_call(
        functools.partial(_diff_attn_kernel, lam_init=lam_init, heads=H_A, has_ctx=False, sub=SEQ),
        out_shape=jax.ShapeDtypeStruct((T_CTX, W_A), BF16),
        grid=(BATCH,),
        in_specs=[
            pl.BlockSpec((None, 4, DH_A), lambda b: (e, 0, 0)),
            pl.BlockSpec((None, 1, 2 * DH_A), lambda b: (e, 0, 0)),
            spec, spec, spec,
        ],
        out_specs=spec,
        compiler_params=_cparams(("parallel",)),
        name="diff_attn_ctx",
    )(lam_vecs, gsub, q, k, v)


def _diff_attn_smp(q, k, v, cache_k, cache_v, lam_vecs, gsub, e, lam_init):
    tq, sub = DEC_SEQ, 256
    w = 2 * DH_A
    nq = DEC_SEQ // tq
    cache_spec = pl.BlockSpec((None, None, 1, PAST_LEN, w), lambda b, h, i: (b, e, h, 0, 0))
    return pl.pallas_call(
        functools.partial(_diff_attn_kernel, lam_init=lam_init, heads=1, has_ctx=True, sub=sub),
        out_shape=jax.ShapeDtypeStruct((T_SMP, W_A), BF16),
        grid=(DEC_BATCH, H_A, nq),
        in_specs=[
            pl.BlockSpec((None, 4, DH_A), lambda b, h, i: (e, 0, 0)),
            pl.BlockSpec((None, 1, w), lambda b, h, i: (e, 0, 0)),
            pl.BlockSpec((tq, w), lambda b, h, i: (b * nq + i, h)),
            pl.BlockSpec((DEC_SEQ, w), lambda b, h, i: (b, h)),
            pl.BlockSpec((DEC_SEQ, w), lambda b, h, i: (b, h)),
            cache_spec, cache_spec,
        ],
        out_specs=pl.BlockSpec((tq, w), lambda b, h, i: (b * nq + i, h)),
        compiler_params=_cparams(("parallel", "parallel", "arbitrary")),
        name="diff_attn_smp",
    )(lam_vecs, gsub, q, k, v, cache_k, cache_v)


DFT_LO = 64


def _fourier_kernel(x_ref, bc_ref, bs_ref, c0_ref, s0_ref, c1_ref, s1_ref, o_ref, y1_ref, y2_ref, *, scale, tr):
    r = pl.program_id(1)

    @pl.when(r == 0)
    def _():
        x = x_ref[...]
        y1_ref[...] = _dot(x, bc_ref[...]).astype(BF16)
        y2_ref[...] = _dot(x, bs_ref[...]).astype(BF16)

    c0, s0 = c0_ref[...], s0_ref[...]
    cs, ss = [], []
    for jj in range(tr // DFT_LO):
        j1 = r * (tr // DFT_LO) + jj
        c1, s1 = c1_ref[pl.ds(j1, 1), :], s1_ref[pl.ds(j1, 1), :]
        cs.append((c1 * c0 - s1 * s0).astype(BF16))
        ss.append((s1 * c0 + c1 * s0).astype(BF16))
    out = _dot(jnp.concatenate(cs, axis=0), y1_ref[...]) - _dot(jnp.concatenate(ss, axis=0), y2_ref[...])
    o_ref[...] = (out * scale).astype(BF16)


def _fourier(f, s, dft_ch, dft_seq):
    tr = 256
    nb = f.shape[0] // s
    bc, bs = dft_ch
    full = lambda a: pl.BlockSpec(a.shape, lambda b, r: (0, 0))
    return pl.pallas_call(
        functools.partial(_fourier_kernel, scale=1.0 / math.sqrt(s * DG_B), tr=tr),
        out_shape=jax.ShapeDtypeStruct(f.shape, BF16),
        grid=(nb, s // tr),
        in_specs=[pl.BlockSpec((s, W_B), lambda b, r: (b, 0)), full(bc), full(bs)] + [full(t) for t in dft_seq],
        out_specs=pl.BlockSpec((tr, W_B), lambda b, r: (b * (s // tr) + r, 0)),
        scratch_shapes=[pltpu.VMEM((s, W_B), BF16), pltpu.VMEM((s, W_B), BF16)],
        compiler_params=_cparams(("parallel", "arbitrary")),
        name="fourier",
    )(f, bc, bs, *dft_seq)


def _dft_seq_tables(n):
    a0, a1 = _dft_angles(DFT_LO, 1, n), _dft_angles(n // DFT_LO, DFT_LO, n)
    return jnp.cos(a0), jnp.sin(a0), jnp.cos(a1), jnp.sin(a1)


def _dft_angles(rows, stride, n):
    j = jnp.arange(rows, dtype=jnp.int32)[:, None] * stride
    k = jnp.arange(n, dtype=jnp.int32)[None, :]
    return ((j * k) % n).astype(F32) * (2.0 * math.pi / n)


def _dft_mats(n):
    lo = min(n, 64)
    a0 = _dft_angles(lo, 1, n)
    c0, s0 = jnp.cos(a0), jnp.sin(a0)
    if lo == n:
        return c0, s0
    a1 = _dft_angles(n // lo, lo, n)
    c1, s1 = jnp.cos(a1)[:, None, :], jnp.sin(a1)[:, None, :]
    c = c1 * c0[None] - s1 * s0[None]
    s = s1 * c0[None] + c1 * s0[None]
    return c.reshape(n, n), s.reshape(n, n)


def _dft_channel_mats():
    c, s = _dft_mats(DG_B)
    eye = jnp.eye(G_B, dtype=F32)
    return jnp.kron(eye, c).astype(BF16), jnp.kron(eye, s).astype(BF16)


def _outproj_kernel(*refs, n_act, n_x):
    is_ctx = pl.program_id(0) < T_CTX // TM
    x = refs[0][...] if n_x == 1 else jnp.where(is_ctx, refs[0][...], refs[1][...])
    mod_ref, g_ref = refs[n_x:n_x + 2]
    act = refs[n_x + 2:n_x + 2 + 3 * n_act]
    o_ref, hn_ref = refs[n_x + 2 + 3 * n_act:]
    y = None
    for a in range(n_act):
        ctx_ref, smp_ref, w_ref = act[3 * a:3 * a + 3]
        lhs = jnp.where(is_ctx, ctx_ref[...], smp_ref[...])
        t = _dot(lhs, w_ref[...])
        y = t if y is None else y + t
    x_new = x + mod_ref[5:6, :] * y
    o_ref[...] = x_new
    hn_ref[...] = _norm_mod(x_new, g_ref[...], mod_ref, 6).astype(BF16)


def _outproj(xs, mod_l, g_norm_l, acts):
    n_ctx = T_CTX // TM
    row_spec = pl.BlockSpec((TM, D_MODEL), lambda i: (i, 0))
    ctx_rows = lambda n: pl.BlockSpec((TM, n), lambda i: (jnp.minimum(i, n_ctx - 1), 0))
    smp_rows = lambda n: pl.BlockSpec((TM, n), lambda i: (jnp.maximum(i - n_ctx, 0), 0))
    in_specs = ([row_spec] if len(xs) == 1 else [ctx_rows(D_MODEL), smp_rows(D_MODEL)]) + [
        pl.BlockSpec((None, N_MOD, D_MODEL), lambda i: (_cond_row(i), 0, 0)),
        pl.BlockSpec((None, 1, D_MODEL), lambda i: (2, 0, 0)),
    ]
    args = [*xs, mod_l, g_norm_l]
    for ctx, smp, w, rb in acts:
        kdim = ctx.shape[1]
        in_specs += [ctx_rows(kdim), smp_rows(kdim), pl.BlockSpec((kdim, D_MODEL), lambda i, rb=rb: (rb, 0))]
        args += [ctx, smp, w]
    return pl.pallas_call(
        functools.partial(_outproj_kernel, n_act=len(acts), n_x=len(xs)),
        out_shape=(jax.ShapeDtypeStruct((T_ALL, D_MODEL), F32), jax.ShapeDtypeStruct((T_ALL, D_MODEL), BF16)),
        grid=(T_ALL // TM,),
        in_specs=in_specs,
        out_specs=(row_spec, row_spec),
        compiler_params=_cparams(("parallel",)),
        name="outproj",
    )(*args)


MLA_TM = 256


def _mla_proj_kernel(x_ref, mod_ref, g_ref, win_ref, wuq_ref, wukv_ref, gql_ref, gkvl_ref, gq_ref, gk_ref,
                     cos_ref, sa_ref, sb_ref, q_ref, k_ref, v_ref, ckv_ref, kpe_ref):
    h = _norm_mod(x_ref[...], g_ref[...], mod_ref, 3).astype(BF16)
    res = _dot(h, win_ref[...])
    cq = res[:, :Q_LORA]
    cqn = (cq * _rms(cq, Q_LORA) * gql_ref[...]).astype(BF16)
    ckv = res[:, Q_LORA:Q_LORA + KV_LORA]
    ckvn = ckv * _rms(ckv, KV_LORA) * gkvl_ref[...]
    ckv_ref[...] = ckvn
    kpe = res[:, Q_LORA + KV_LORA:]
    kpe_ref[...] = kpe
    cos, sa, sb = cos_ref[...], sa_ref[...], sb_ref[...]
    gq, gk = gq_ref[...] * (DQK_C ** -0.5 * LOG2E), gk_ref[...]
    ss_pe = jnp.sum(kpe * kpe, axis=-1, keepdims=True)
    kr = _rope(kpe * gk[:, LANE:], cos, sa, sb, ROPE_C // 4)
    ckvb = ckvn.astype(BF16)
    for hh in range(H_C):
        hs = slice(hh * HEAD_PAD, (hh + 1) * HEAD_PAD)
        lo = slice(hh * HEAD_PAD, hh * HEAD_PAD + LANE)
        hi = slice(hh * HEAD_PAD + LANE, (hh + 1) * HEAD_PAD)
        xh = _dot(cqn, wuq_ref[:, hs])
        y = xh * _rms(xh, DQK_C) * gq
        q_ref[:, lo] = y[:, :LANE].astype(BF16)
        q_ref[:, hi] = _rope(y[:, LANE:], cos, sa, sb, ROPE_C // 4).astype(BF16)
        kv = _dot(ckvb, wukv_ref[:, hs])
        k_nope = kv[:, :LANE]
        ss = jnp.sum(k_nope * k_nope, axis=-1, keepdims=True) + ss_pe
        r = lax.rsqrt(ss * (1.0 / DQK_C) + EPS)
        k_ref[:, lo] = (k_nope * r * gk[:, :LANE]).astype(BF16)
        k_ref[:, hi] = (kr * r).astype(BF16)
        v_ref[:, hh * DV_C:(hh + 1) * DV_C] = kv[:, LANE:].astype(BF16)


def _mla_proj(x, mod_l, g_norm_l, w_in_pad, w_uq_pad, w_ukv, gq_lora, gkv_lora, gq_pad, gk_pad, o, tabs):
    tm = MLA_TM
    n_in = Q_LORA + KV_LORA + LANE
    resident = dict(pipeline_mode=pl.Buffered(1))
    tab_spec = pl.BlockSpec((tm, LANE), lambda i: (i, 0))
    row = lambda n: pl.BlockSpec((tm, n), lambda i: (i, 0))
    return pl.pallas_call(
        _mla_proj_kernel,
        out_shape=(jax.ShapeDtypeStruct((T_ALL, H_C * HEAD_PAD), BF16),
                   jax.ShapeDtypeStruct((T_ALL, H_C * HEAD_PAD), BF16),
                   jax.ShapeDtypeStruct((T_ALL, H_C * DV_C), BF16),
                   jax.ShapeDtypeStruct((T_ALL, KV_LORA), F32),
                   jax.ShapeDtypeStruct((T_ALL, LANE), F32)),
        grid=(T_ALL // tm,),
        in_specs=[
            row(D_MODEL),
            pl.BlockSpec((None, N_MOD, D_MODEL), lambda i: (_cond_row(i, tm), 0, 0)),
            pl.BlockSpec((None, 1, D_MODEL), lambda i: (1, 0, 0)),
            pl.BlockSpec((D_MODEL, n_in), lambda i: (0, 0), **resident),
            pl.BlockSpec((Q_LORA, H_C * HEAD_PAD), lambda i: (0, 0), **resident),
            pl.BlockSpec((KV_LORA, H_C * HEAD_PAD), lambda i: (0, 0), **resident),
            pl.BlockSpec((None, 1, Q_LORA), lambda i: (o, 0, 0)),
            pl.BlockSpec((None, 1, KV_LORA), lambda i: (o, 0, 0)),
            pl.BlockSpec((None, 1, HEAD_PAD), lambda i: (o, 0, 0)),
            pl.BlockSpec((None, 1, HEAD_PAD), lambda i: (o, 0, 0)),
            tab_spec, tab_spec, tab_spec,
        ],
        out_specs=(row(H_C * HEAD_PAD), row(H_C * HEAD_PAD), row(H_C * DV_C), row(KV_LORA), row(LANE)),
        compiler_params=_cparams(("parallel",)),
        name="mla_proj",
    )(x, mod_l, g_norm_l, w_in_pad, w_uq_pad, w_ukv, gq_lora, gkv_lora, gq_pad, gk_pad, *tabs)


MLA_HB = 4


def _mla_kv_kernel(ckv_ref, kpe_ref, w_ref, g_ref, *rest, use_rope):
    if use_rope:
        cos_ref, sa_ref, sb_ref, k_ref, v_ref = rest
    else:
        k_ref, v_ref = rest
    ckv = ckv_ref[...].astype(BF16)
    g = g_ref[...]
    kpe = kpe_ref[...]
    ss_pe = jnp.sum(kpe * kpe, axis=-1, keepdims=True)
    kr = kpe * g[:, LANE:]
    if use_rope:
        kr = _rope(kr, cos_ref[...], sa_ref[...], sb_ref[...], ROPE_C // 4)
    for hh in range(MLA_HB):
        res = _dot(ckv, w_ref[:, hh * 2 * LANE:(hh + 1) * 2 * LANE])
        k_nope = res[:, :LANE]
        ss = jnp.sum(k_nope * k_nope, axis=-1, keepdims=True) + ss_pe
        r = lax.rsqrt(ss * (1.0 / DQK_C) + EPS)
        k_ref[:, hh * HEAD_PAD:hh * HEAD_PAD + LANE] = (k_nope * r * g[:, :LANE]).astype(BF16)
        k_ref[:, hh * HEAD_PAD + LANE:(hh + 1) * HEAD_PAD] = (kr * r).astype(BF16)
        v_ref[:, hh * DV_C:(hh + 1) * DV_C] = res[:, LANE:].astype(BF16)


def _mla_kv(ckv, kpe_pad, tile0, rows, w_ukv, gk_pad, o, tabs):
    use_rope = tabs is not None
    in_specs = [
        pl.BlockSpec((TM, KV_LORA), lambda i, j: (tile0 + i, 0)),
        pl.BlockSpec((TM, LANE), lambda i, j: (tile0 + i, 0)),
        pl.BlockSpec((KV_LORA, MLA_HB * 2 * LANE), lambda i, j: (0, j)),
        pl.BlockSpec((None, 1, HEAD_PAD), lambda i, j: (o, 0, 0)),
    ]
    args = [ckv, kpe_pad, w_ukv, gk_pad]
    if use_rope:
        in_specs += [pl.BlockSpec((TM, LANE), lambda i, j: (i % (DEC_SEQ // TM), 0))] * 3
        args += list(tabs)
    return pl.pallas_call(
        functools.partial(_mla_kv_kernel, use_rope=use_rope),
        out_shape=(jax.ShapeDtypeStruct((rows, H_C * HEAD_PAD), BF16),
                   jax.ShapeDtypeStruct((rows, H_C * DV_C), BF16)),
        grid=(rows // TM, H_C // MLA_HB),
        in_specs=in_specs,
        out_specs=(pl.BlockSpec((TM, MLA_HB * HEAD_PAD), lambda i, j: (i, j)),
                   pl.BlockSpec((TM, MLA_HB * DV_C), lambda i, j: (i, j))),
        compiler_params=_cparams(("parallel", "parallel")),
        name="mla_kv",
    )(*args)


def _mla_attn_kernel(q_ref, k_ref, v_ref, *rest, heads, has_ctx, sub):
    if has_ctx:
        ck_ref, cv_ref, o_ref, v1_ref, cv1_ref = rest
    else:
        o_ref, v1_ref = rest
    def with_ones(dst_ref, hh, v):
        dst_ref[hh, :, :DV_C] = v
        dst_ref[hh, :, DV_C:] = jnp.ones_like(v)
        return dst_ref[hh]

    for hh in range(heads):
        ks = slice(hh * HEAD_PAD, (hh + 1) * HEAD_PAD)
        vs = slice(hh * DV_C, (hh + 1) * DV_C)
        v1 = with_ones(v1_ref, hh, v_ref[:, vs])
        if has_ctx:
            cv1 = with_ones(cv1_ref, hh, cv_ref[:, vs])
        for r in range(q_ref.shape[0] // sub):
            rs = slice(r * sub, (r + 1) * sub)
            q = q_ref[rs, ks]
            s = _dot_nt(q, k_ref[:, ks])
            s2 = _dot_nt(q, ck_ref[:, ks]) if has_ctx else None
            e, e2, _ = _softmax_exp(s, s2, False)
            o = _dot(e.astype(BF16), v1)
            if has_ctx:
                o = o + _dot(e2.astype(BF16), cv1)
            o_ref[rs, vs] = (o[:, :DV_C] * (1.0 / o[:, DV_C:])).astype(BF16)


def _mla_attn_ctx(q, k, v):
    heads = H_C
    return pl.pallas_call(
        functools.partial(_mla_attn_kernel, heads=heads, has_ctx=False, sub=SEQ),
        out_shape=jax.ShapeDtypeStruct((T_CTX, H_C * DV_C), BF16),
        grid=(BATCH, H_C // heads),
        in_specs=[
            pl.BlockSpec((SEQ, heads * HEAD_PAD), lambda b, j: (b, j)),
            pl.BlockSpec((SEQ, heads * HEAD_PAD), lambda b, j: (b, j)),
            pl.BlockSpec((SEQ, heads * DV_C), lambda b, j: (b, j)),
        ],
        out_specs=pl.BlockSpec((SEQ, heads * DV_C), lambda b, j: (b, j)),
        scratch_shapes=[pltpu.VMEM((heads, SEQ, 2 * DV_C), BF16)],
        compiler_params=_cparams(("parallel", "parallel")),
        name="mla_attn_ctx",
    )(q, k, v)


def _mla_attn_smp(q, k, v, ck, cv):
    tq, sub = DEC_SEQ, 256
    nq = DEC_SEQ // tq
    q0, s0 = T_CTX // tq, T_CTX // DEC_SEQ
    return pl.pallas_call(
        functools.partial(_mla_attn_kernel, heads=1, has_ctx=True, sub=sub),
        out_shape=jax.ShapeDtypeStruct((T_SMP, H_C * DV_C), BF16),
        grid=(DEC_BATCH, H_C, nq),
        in_specs=[
            pl.BlockSpec((tq, HEAD_PAD), lambda b, h, i: (q0 + b * nq + i, h)),
            pl.BlockSpec((DEC_SEQ, HEAD_PAD), lambda b, h, i: (s0 + b, h)),
            pl.BlockSpec((DEC_SEQ, DV_C), lambda b, h, i: (s0 + b, h)),
            pl.BlockSpec((PAST_LEN, HEAD_PAD), lambda b, h, i: (b, h)),
            pl.BlockSpec((PAST_LEN, DV_C), lambda b, h, i: (b, h)),
        ],
        out_specs=pl.BlockSpec((tq, DV_C), lambda b, h, i: (b * nq + i, h)),
        scratch_shapes=[pltpu.VMEM((1, DEC_SEQ, 2 * DV_C), BF16), pltpu.VMEM((1, PAST_LEN, 2 * DV_C), BF16)],
        compiler_params=_cparams(("parallel", "parallel", "arbitrary")),
        name="mla_attn_smp",
    )(q, k, v, ck, cv)


def _axial_angles(rows, rot_dim):
    row = jnp.repeat(jnp.arange(rows, dtype=F32), GRID_W)
    col = jnp.tile(jnp.arange(GRID_W, dtype=F32), rows)
    half = rot_dim // 2
    inv_freq = ROPE_BASE ** (-jnp.arange(0, half, 2, dtype=F32) / half)
    ang_r = row[:, None] * inv_freq[None, :]
    ang_c = col[:, None] * inv_freq[None, :]
    return jnp.concatenate([ang_r, ang_r, ang_c, ang_c], axis=-1)


def _rope_tables(rows, rot_dim):
    ang = _axial_angles(rows, rot_dim)
    q = rot_dim // 4
    blk = (jnp.arange(rot_dim) // q) % 2
    cos, sin = jnp.cos(ang), jnp.sin(ang)
    sin_a = jnp.where(blk == 0, -sin, 0.0)
    sin_b = jnp.where(blk == 1, sin, 0.0)

    def full(t, fill):
        return jnp.pad(t, ((0, 0), (0, LANE - rot_dim)), constant_values=fill)

    return full(cos, 1.0), full(sin_a, 0.0), full(sin_b, 0.0)


def _joint_tables(tabs):
    def joint(t, fill):
        return jnp.concatenate([jnp.full((T_CTX, LANE), fill, F32), jnp.tile(t, (DEC_BATCH, 1))], axis=0)
    return joint(tabs[0], 1.0), joint(tabs[1], 0.0), joint(tabs[2], 0.0)


def kernel(x_prompt, x_sample, c, cache_diff_k, cache_diff_v, cache_mla_ckv, cache_mla_kpe, c_ctx, w_mod, b_mod, g_norm, w_ffn_gate, w_ffn_up, w_ffn_down, w_in_ab, w_out_ab, g_qk_diff, diff_lambda, g_diff_sub, w_in_mla, g_q_lora, w_uq, g_kv_lora, w_ukv, g_qk_mla, w_o_mla):
    rows = DEC_SEQ // GRID_W
    tabs_a = _rope_tables(rows, DH_A)
    tabs_c_joint = _joint_tables(_rope_tables(rows, ROPE_C))
    dft_ch = _dft_channel_mats()
    dft_ctx, dft_smp = _dft_seq_tables(SEQ), _dft_seq_tables(DEC_SEQ)
    p_ctx, p_smp = _Pass(False), _Pass(True)

    w_ffn = (w_ffn_gate[0, 0].astype(BF16), w_ffn_up[0, 0].astype(BF16), w_ffn_down[0, 0].astype(BF16))

    def ffn_casts(l, k):
        return [(w_ffn_gate, (l, k), "plain"), (w_ffn_up, (l, k), "plain"), (w_ffn_down, (l, k), "plain")]

    def mixer_casts(l):
        j = l // 2
        if l % 2 == 0:
            return [(w_in_ab, (j,), "plain"), (w_out_ab, (j,), "plain")]
        return [(w_in_mla, (j,), "tail"), (w_uq, (j,), "heads"), (w_ukv, (j,), "plain"), (w_o_mla, (j,), "plain")]

    g_qk_mla_pad = jnp.pad(g_qk_mla, ((0, 0), (0, 0), (0, HEAD_PAD - DQK_C)))[:, :, None, :]
    gq_mla, gk_mla = g_qk_mla_pad[:, 0], g_qk_mla_pad[:, 1]
    g_norm4 = g_norm[:, :, None, :]
    gqk_diff = g_qk_diff[:, :, None, :]
    g_sub3 = g_diff_sub[:, None, :]
    g_q_lora3, g_kv_lora3 = g_q_lora[:, None, :], g_kv_lora[:, None, :]

    cond8 = jnp.concatenate([c_ctx[None, :], c, jnp.zeros((8 - N_COND, D_MODEL), F32)], axis=0)
    mod = _modulation(cond8, w_mod, b_mod)[:, :N_COND].reshape(DEPTH, N_COND, N_MOD, D_MODEL)

    x_ctx, x_smp = x_prompt.reshape(T_CTX, D_MODEL), x_sample.reshape(T_SMP, D_MODEL)
    n_ctx, n_smp, n_all = T_CTX // FFN_TM, T_SMP // FFN_TM, T_ALL // FFN_TM
    new_kv = None
    mla_ckv, mla_kpe = [], []

    for l in range(DEPTH):
        mod_l = mod[l]
        if l == 0:
            h = _prenorm((x_ctx, x_smp), mod_l, g_norm4[l])
            xc, *cast = _ffn(h, x_ctx, 0, 0, n_ctx, mod_l, w_ffn, 0, ffn_casts(l, 1) + mixer_casts(l)[:1])
            xs, w_out0 = _ffn(h, x_smp, 0, n_ctx, n_smp, mod_l, w_ffn, 0, mixer_casts(l)[1:])
            w_ffn, w_mix, xs_l = cast[:3], [cast[3], w_out0], (xc, xs)
        else:
            h = _prenorm((x,), mod_l, g_norm4[l])
            x, *w_ffn = _ffn(h, x, 0, 0, n_all, mod_l, w_ffn, 0, ffn_casts(l, 1))
            xs_l = (x, x)
        if l % 2 == 0:
            e = l // 2
            lam_init = 0.8 - 0.6 * math.exp(-0.3 * l)
            w_in_b, w_out_b = w_mix
            q_c, k_c, v_c, f_c, *new_kv = _inproj_even(p_ctx, xs_l[0], mod_l, g_norm4[l], w_in_b, gqk_diff, e,
                                                      None, new_kv)
            q_s, k_s, v_s, f_s = _inproj_even(p_smp, xs_l[1], mod_l, g_norm4[l], w_in_b, gqk_diff, e, tabs_a, None)
            o_ctx = _diff_attn_ctx(q_c, k_c, v_c, diff_lambda, g_sub3, e, lam_init)
            o_smp = _diff_attn_smp(q_s, k_s, v_s, cache_diff_k, cache_diff_v, diff_lambda, g_sub3, e, lam_init)
            fo_ctx = _fourier(f_c, SEQ, dft_ch, dft_ctx)
            fo_smp = _fourier(f_s, DEC_SEQ, dft_ch, dft_smp)
            x, h = _outproj(xs_l if l == 0 else (x,), mod_l, g_norm4[l],
                            [(o_ctx, o_smp, w_out_b, 0), (fo_ctx, fo_smp, w_out_b, W_A // W_B)])
        else:
            o = l // 2
            w_in_mla_b, w_uq_b, w_ukv_b, w_o_b = w_mix
            q, k, v, ckvn, kpe_pad = _mla_proj(x, mod_l, g_norm4[l], w_in_mla_b, w_uq_b, w_ukv_b, g_q_lora3,
                                               g_kv_lora3, gq_mla, gk_mla, o, tabs_c_joint)
            cache_ckv = cache_mla_ckv[:, o].reshape(DEC_BATCH * PAST_LEN, KV_LORA)
            cache_kpe = jnp.pad(cache_mla_kpe[:, o].reshape(DEC_BATCH * PAST_LEN, ROPE_C),
                                ((0, 0), (0, LANE - ROPE_C)))
            ck, cv = _mla_kv(cache_ckv, cache_kpe, 0, DEC_BATCH * PAST_LEN, w_ukv_b, gk_mla, o, None)
            a_ctx = _mla_attn_ctx(q, k, v)
            a_smp = _mla_attn_smp(q, k, v, ck, cv)
            x, h = _outproj((x,), mod_l, g_norm4[l], [(a_ctx, a_smp, w_o_b, 0)])
            mla_ckv.append(ckvn[:T_CTX].reshape(BATCH, SEQ, KV_LORA))
            mla_kpe.append(kpe_pad[:T_CTX, :ROPE_C].reshape(BATCH, SEQ, ROPE_C))
        if l + 1 < DEPTH:
            x, *cast = _ffn(h, x, 0, 0, n_all, mod_l, w_ffn, 1, ffn_casts(l + 1, 0) + mixer_casts(l + 1))
            w_ffn, w_mix = cast[:3], cast[3:]
        else:
            (y_ctx,) = _ffn(h, x, 0, 0, n_ctx, mod_l, w_ffn, 1, [])
            (y_smp,) = _ffn(h, x, n_ctx, n_ctx, n_smp, mod_l, w_ffn, 1, [])

    y_prompt = y_ctx.reshape(BATCH, SEQ, D_MODEL)
    y_sample = y_smp.reshape(DEC_BATCH, DEC_SEQ, D_MODEL)
    return (y_prompt, y_sample, new_kv[0], new_kv[1], jnp.stack(mla_ckv, axis=1), jnp.stack(mla_kpe, axis=1))
```

```python
import functools
import math

import jax
import jax.numpy as jnp
from jax import lax
from jax.experimental import pallas as pl
from jax.experimental.pallas import tpu as pltpu

F32 = jnp.float32
BF16 = jnp.bfloat16

D_MODEL = 2048
BATCH = 32
SEQ = 256
DEPTH = 4
DEC_BATCH = 2
DEC_SEQ = 2048
PAST_LEN = 256
GRID_W = 64
N_EVEN = 2
N_ODD = 2
H_A = 6
DH_A = 128
W_A = H_A * 2 * DH_A
G_B = 4
DG_B = 128
W_B = G_B * DG_B
H_C = 16
Q_LORA = 1536
KV_LORA = 512
NOPE_C = 128
ROPE_C = 64
DQK_C = NOPE_C + ROPE_C
DV_C = 128
D_FF = 5632
N_MOD = 9
ROPE_BASE = 10000.0
EPS = 1e-6
LOG2E = 1.4426950408889634

T_CTX = BATCH * SEQ
T_SMP = DEC_BATCH * DEC_SEQ
T_ALL = T_CTX + T_SMP
N_COND = 1 + DEC_BATCH
LANE = 128
HEAD_PAD = 256
TM = 512

VMEM_LIMIT = 52 * 1024 * 1024
FFN_TM = 1024
FFN_VMEM_LIMIT = 60 * 1024 * 1024


def _cparams(sem):
    return pltpu.CompilerParams(dimension_semantics=sem, vmem_limit_bytes=VMEM_LIMIT)


def _cond_row(i, tm=TM):
    n_ctx = T_CTX // tm
    per = DEC_SEQ // tm
    return jnp.where(i < n_ctx, 0, 1 + (i - n_ctx) // per)


class _Pass:
    def __init__(self, latent):
        self.latent = latent
        self.rows = T_SMP if latent else T_CTX
        self.tile0 = (T_CTX // TM) if latent else 0
        self.tiles = self.rows // TM
        self.seq = DEC_SEQ if latent else SEQ
        self.batch = DEC_BATCH if latent else BATCH

    def cond(self, i):
        return 1 + i // (DEC_SEQ // TM) if self.latent else 0


def _dot(a, b):
    return jnp.dot(a, b, preferred_element_type=F32)


def _dot_nt(a, b):
    return lax.dot_general(a, b, (((1,), (1,)), ((), ())), preferred_element_type=F32)


def _silu(x):
    return x / (1.0 + jnp.exp(-x))


def _rms(x, denom):
    return lax.rsqrt(jnp.sum(x * x, axis=-1, keepdims=True) * (1.0 / denom) + EPS)


def _norm_mod(x, g, mod_ref, base):
    sh = mod_ref[base:base + 1, :]
    sc = mod_ref[base + 1:base + 2, :]
    return (x * _rms(x, D_MODEL) * g) * (1.0 + sc) + sh


def _rope(y, cos, sin_a, sin_b, q):
    return y * cos + pltpu.roll(y, LANE - q, 1) * sin_a + pltpu.roll(y, q, 1) * sin_b


def _softmax_exp(s, s2, need_sum):
    m = jnp.max(s, axis=-1, keepdims=True)
    if s2 is not None:
        m = jnp.maximum(m, jnp.max(s2, axis=-1, keepdims=True))
    e = jnp.exp2(s - m)
    e2 = jnp.exp2(s2 - m) if s2 is not None else None
    if not need_sum:
        return e, e2, None
    l = jnp.sum(e, axis=-1, keepdims=True)
    if s2 is not None:
        l = l + jnp.sum(e2, axis=-1, keepdims=True)
    return e, e2, 1.0 / l


def _mod_kernel(c_ref, w_ref, b_ref, o_ref):
    a = _silu(c_ref[...]).astype(BF16)
    o_ref[...] = _dot(a, w_ref[...].astype(BF16)) + b_ref[...]


def _modulation(cond8, w_mod, b_mod):
    tn = 1024
    n = N_MOD * D_MODEL
    return pl.pallas_call(
        _mod_kernel,
        out_shape=jax.ShapeDtypeStruct((DEPTH, 8, n), F32),
        grid=(DEPTH, n // tn),
        in_specs=[
            pl.BlockSpec((8, D_MODEL), lambda l, j: (0, 0)),
            pl.BlockSpec((None, D_MODEL, tn), lambda l, j: (l, 0, j)),
            pl.BlockSpec((None, 1, tn), lambda l, j: (l, 0, j)),
        ],
        out_specs=pl.BlockSpec((None, 8, tn), lambda l, j: (l, 0, j)),
        compiler_params=_cparams(("parallel", "parallel")),
        name="modulation",
    )(cond8, w_mod, b_mod.reshape(DEPTH, 1, n))


FFN_TF = 512
FFN_XC = 256
FFN_DC = 512
BF16_ROWS = 16


def _prenorm_kernel(*refs, split):
    if split:
        xc_ref, xs_ref, mod_ref, g_ref, h_ref = refs
        x = jnp.where(pl.program_id(0) < T_CTX // TM, xc_ref[...], xs_ref[...])
    else:
        x_ref, mod_ref, g_ref, h_ref = refs
        x = x_ref[...]
    h_ref[...] = _norm_mod(x, g_ref[...], mod_ref, 0).astype(BF16)


PRE_SLOTS = 3


def _prenorm_ring_kernel(x_hbm, mod_ref, g_ref, h_ref, buf, sem):
    i = pl.program_id(0)
    n = pl.num_programs(0)

    def fetch(t):
        slot = lax.rem(t, PRE_SLOTS)
        return pltpu.make_async_copy(x_hbm.at[pl.ds(t * TM, TM), :], buf.at[slot], sem.at[slot])

    @pl.when(i == 0)
    def _():
        for t in range(PRE_SLOTS - 1):
            fetch(t).start()

    @pl.when(i + PRE_SLOTS - 1 < n)
    def _():
        fetch(i + PRE_SLOTS - 1).start()

    fetch(i).wait()
    x = buf[lax.rem(i, PRE_SLOTS)]
    h_ref[...] = _norm_mod(x, g_ref[...], mod_ref, 0).astype(BF16)


def _prenorm_ring(x, mod_l, g):
    assert T_ALL // TM >= PRE_SLOTS - 1
    return pl.pallas_call(
        _prenorm_ring_kernel,
        out_shape=jax.ShapeDtypeStruct((T_ALL, D_MODEL), BF16),
        grid=(T_ALL // TM,),
        in_specs=[
            pl.BlockSpec(memory_space=pl.ANY),
            pl.BlockSpec((None, N_MOD, D_MODEL), lambda i: (_cond_row(i), 0, 0)),
            pl.BlockSpec((None, 1, D_MODEL), lambda i: (0, 0, 0)),
        ],
        out_specs=pl.BlockSpec((TM, D_MODEL), lambda i: (i, 0)),
        scratch_shapes=[pltpu.VMEM((PRE_SLOTS, TM, D_MODEL), F32), pltpu.SemaphoreType.DMA((PRE_SLOTS,))],
        compiler_params=_cparams(("arbitrary",)),
        name="prenorm_ring",
    )(x, mod_l, g)


def _prenorm(xs, mod_l, g):
    n_ctx = T_CTX // TM
    if len(xs) == 1:
        return _prenorm_ring(xs[0], mod_l, g)
    if len(xs) == 2:
        x_specs = [pl.BlockSpec((TM, D_MODEL), lambda i: (jnp.minimum(i, n_ctx - 1), 0)),
                   pl.BlockSpec((TM, D_MODEL), lambda i: (jnp.maximum(i - n_ctx, 0), 0))]
    else:
        x_specs = [pl.BlockSpec((TM, D_MODEL), lambda i: (i, 0))]
    return pl.pallas_call(
        functools.partial(_prenorm_kernel, split=len(xs) == 2),
        out_shape=jax.ShapeDtypeStruct((T_ALL, D_MODEL), BF16),
        grid=(T_ALL // TM,),
        in_specs=x_specs + [
            pl.BlockSpec((None, N_MOD, D_MODEL), lambda i: (_cond_row(i), 0, 0)),
            pl.BlockSpec((None, 1, D_MODEL), lambda i: (0, 0, 0)),
        ],
        out_specs=pl.BlockSpec((TM, D_MODEL), lambda i: (i, 0)),
        compiler_params=_cparams(("parallel",)),
        name="prenorm",
    )(*xs, mod_l, g)


def _cast_weight(src_ref, dst_ref, kind):
    w = src_ref[...]
    if kind == "plain":
        dst_ref[...] = w.astype(BF16)
    elif kind == "tail":
        n = w.shape[1]
        dst_ref[:, :n] = w.astype(BF16)
        dst_ref[:, n:] = jnp.zeros((w.shape[0], dst_ref.shape[1] - n), BF16)
    else:
        for hh in range(H_C):
            dst_ref[:, hh * HEAD_PAD:hh * HEAD_PAD + DQK_C] = w[:, hh * DQK_C:(hh + 1) * DQK_C].astype(BF16)
            dst_ref[:, hh * HEAD_PAD + DQK_C:(hh + 1) * HEAD_PAD] = jnp.zeros((w.shape[0], HEAD_PAD - DQK_C), BF16)


def _ffn_kernel(*refs, base, cast_kinds):
    it = iter(refs)
    h_ref, x_ref, mod_ref, wg_ref, wu_ref, wd_ref = [next(it) for _ in range(6)]
    srcs = [next(it) for _ in cast_kinds]
    o_ref = next(it)
    dsts = [next(it) for _ in cast_kinds]
    f = pl.program_id(1)

    @pl.when(f == 0)
    def _():
        o_ref[...] = jnp.zeros_like(o_ref)

    h = h_ref[...]
    gate = _dot(h, wg_ref[...])
    up = _dot(h, wu_ref[...])
    a = (_silu(gate) * up).astype(BF16)
    for c in range(D_MODEL // FFN_DC):
        cs = slice(c * FFN_DC, (c + 1) * FFN_DC)
        o_ref[:, cs] += (0.5 * mod_ref[base + 2:base + 3, cs]) * _dot(a, wd_ref[:, cs])

    for src, dst, kind in zip(srcs, dsts, cast_kinds):
        _cast_weight(src, dst, kind)

    for c in range(D_MODEL // FFN_XC):
        @pl.when(f == c)
        def _(c=c):
            o_ref[:, c * FFN_XC:(c + 1) * FFN_XC] += x_ref[...]


def _cast_rows(total, steps):
    rows = BF16_ROWS
    while total % rows or total // rows > steps:
        rows += BF16_ROWS
    return rows


def _ffn(h, x, x_tile0, tile0, tiles, mod_l, w_b, k, casts):
    tm, tf = FFN_TM, FFN_TF
    nf = D_FF // tf
    assert nf >= D_MODEL // FFN_XC
    wg, wu, wd = w_b
    in_specs = [
        pl.BlockSpec((tm, D_MODEL), lambda i, f: (tile0 + i, 0)),
        pl.BlockSpec((tm, FFN_XC), lambda i, f: (x_tile0 + i, jnp.minimum(f, D_MODEL // FFN_XC - 1))),
        pl.BlockSpec((None, N_MOD, D_MODEL), lambda i, f: (_cond_row(tile0 + i, tm), 0, 0)),
        pl.BlockSpec((D_MODEL, tf), lambda i, f: (0, f)),
        pl.BlockSpec((D_MODEL, tf), lambda i, f: (0, f)),
        pl.BlockSpec((tf, D_MODEL), lambda i, f: (f, 0)),
    ]
    args = [h, x, mod_l, wg, wu, wd]
    out_shape = [jax.ShapeDtypeStruct((tiles * tm, D_MODEL), F32)]
    out_specs = [pl.BlockSpec((tm, D_MODEL), lambda i, f: (i, 0))]
    for w, lead, kind in casts:
        n_rows, n_cols = w.shape[-2:]
        out_cols = {"plain": n_cols, "tail": -(-n_cols // LANE) * LANE, "heads": H_C * HEAD_PAD}[kind]
        rows = _cast_rows(n_rows, tiles * nf)
        last = n_rows // rows - 1
        in_specs.append(pl.BlockSpec((None,) * len(lead) + (rows, n_cols),
                                     lambda i, f, lead=lead, last=last: (*lead, jnp.minimum(i * nf + f, last), 0)))
        args.append(w)
        out_shape.append(jax.ShapeDtypeStruct((n_rows, out_cols), BF16))
        out_specs.append(pl.BlockSpec((rows, out_cols), lambda i, f, last=last: (jnp.minimum(i * nf + f, last), 0)))
    return pl.pallas_call(
        functools.partial(_ffn_kernel, base=6 * k, cast_kinds=tuple(kind for _, _, kind in casts)),
        out_shape=out_shape,
        grid=(tiles, nf),
        in_specs=in_specs,
        out_specs=out_specs,
        compiler_params=pltpu.CompilerParams(dimension_semantics=("arbitrary", "arbitrary"),
                                             vmem_limit_bytes=FFN_VMEM_LIMIT),
        name="ffn",
    )(*args)


IN_TM = SEQ


def _inproj_even_kernel(*refs, latent, e):
    x_ref, mod_ref, g_ref, w_ref, gq_ref, gk_ref = refs[:6]
    refs = refs[6:]
    if latent:
        cos_ref, sa_ref, sb_ref, q_ref, k_ref, v_ref, f_ref = refs
    else:
        q_ref, k_ref, v_ref, f_ref, nk_ref, nv_ref = refs[-6:]
        if e > 0:
            nk_ref[:e] = refs[0][...]
            nv_ref[:e] = refs[1][...]
        nk_ref, nv_ref = nk_ref.at[e], nv_ref.at[e]
    dh2 = 2 * DH_A
    h = _norm_mod(x_ref[...], g_ref[...], mod_ref, 3).astype(BF16)
    gains = (gq_ref[...] * (DH_A ** -0.5 * LOG2E), gk_ref[...])
    for part, (g, o_ref) in enumerate(zip(gains, (q_ref, k_ref))):
        for hh in range(H_A):
            res = _dot(h, w_ref[:, part * W_A + hh * dh2:part * W_A + (hh + 1) * dh2])
            for m in range(2):
                xc = res[:, m * DH_A:(m + 1) * DH_A]
                y = xc * _rms(xc, DH_A) * g
                if latent:
                    y = _rope(y, cos_ref[...], sa_ref[...], sb_ref[...], DH_A // 4)
                o_ref[:, hh * dh2 + m * DH_A:hh * dh2 + (m + 1) * DH_A] = y.astype(BF16)
                if part == 1 and not latent:
                    nk_ref[hh, :, m * DH_A:(m + 1) * DH_A] = y
    for hh in range(H_A):
        res = _dot(h, w_ref[:, 2 * W_A + hh * dh2:2 * W_A + (hh + 1) * dh2])
        v_ref[:, hh * dh2:(hh + 1) * dh2] = res.astype(BF16)
        if not latent:
            nv_ref[hh] = res
    f_ref[...] = _dot(h, w_ref[:, 3 * W_A:]).astype(BF16)


def _inproj_even(ps, x, mod_l, g_norm_l, w_in, gqk, e, tabs, prev_kv):
    tm = IN_TM
    t0 = ps.tile0 * (TM // tm)
    xt0 = t0 if x.shape[0] == T_ALL else 0
    in_specs = [
        pl.BlockSpec((tm, D_MODEL), lambda i: (xt0 + i, 0)),
        pl.BlockSpec((None, N_MOD, D_MODEL), lambda i: (_cond_row(t0 + i, tm), 0, 0)),
        pl.BlockSpec((None, 1, D_MODEL), lambda i: (1, 0, 0)),
        pl.BlockSpec((D_MODEL, 3 * W_A + W_B), lambda i: (0, 0), pipeline_mode=pl.Buffered(1)),
        pl.BlockSpec((None, None, 1, DH_A), lambda i: (e, 0, 0, 0)),
        pl.BlockSpec((None, None, 1, DH_A), lambda i: (e, 1, 0, 0)),
    ]
    args = [x, mod_l, g_norm_l, w_in, gqk, gqk]
    out_shape = [jax.ShapeDtypeStruct((ps.rows, W_A), BF16)] * 3 + [jax.ShapeDtypeStruct((ps.rows, W_B), BF16)]
    out_specs = [pl.BlockSpec((tm, W_A), lambda i: (i, 0))] * 3 + [pl.BlockSpec((tm, W_B), lambda i: (i, 0))]
    if ps.latent:
        tab_spec = pl.BlockSpec((tm, LANE), lambda i: (i % (DEC_SEQ // tm), 0))
        in_specs += [tab_spec] * 3
        args += list(tabs)
    else:
        kv_block = lambda n: pl.BlockSpec((None, n, H_A, SEQ, 2 * DH_A), lambda i: (i, 0, 0, 0, 0))
        out_shape += [jax.ShapeDtypeStruct((BATCH, e + 1, H_A, SEQ, 2 * DH_A), F32)] * 2
        out_specs += [kv_block(e + 1)] * 2
        if e > 0:
            in_specs += [kv_block(e)] * 2
            args += list(prev_kv)
    return pl.pallas_call(
        functools.partial(_inproj_even_kernel, latent=ps.latent, e=e),
        out_shape=out_shape,
        grid=(ps.rows // tm,),
        in_specs=in_specs,
        out_specs=out_specs,
        compiler_params=_cparams(("parallel",)),
        name="inproj_even",
    )(*args)


def _diff_attn_kernel(lam_ref, gsub_ref, q_ref, k_ref, v_ref, *rest, lam_init, heads, has_ctx, sub):
    if has_ctx:
        ck_ref, cv_ref, o_ref = rest
    else:
        (o_ref,) = rest
    lv = lam_ref[...]
    lam = (jnp.exp(jnp.sum(lv[0:1] * lv[1:2], axis=-1, keepdims=True))
           - jnp.exp(jnp.sum(lv[2:3] * lv[3:4], axis=-1, keepdims=True)) + lam_init)
    dh2 = 2 * DH_A
    gsub = gsub_ref[...] * (1.0 - lam_init)
    for hh in range(heads):
        hs = slice(hh * dh2, (hh + 1) * dh2)
        k = k_ref[:, hs]
        v = v_ref[:, hs]
        if has_ctx:
            ck = ck_ref[hh].astype(BF16)
            cv = cv_ref[hh].astype(BF16)
        for r in range(q_ref.shape[0] // sub):
            rs = slice(r * sub, (r + 1) * sub)
            q = q_ref[rs, hs]
            o = None
            for m in range(2):
                sl = slice(m * DH_A, (m + 1) * DH_A)
                s = _dot_nt(q[:, sl], k[:, sl])
                s2 = _dot_nt(q[:, sl], ck[:, sl]) if has_ctx else None
                e, e2, inv = _softmax_exp(s, s2, True)
                om = _dot(e.astype(BF16), v)
                if has_ctx:
                    om = om + _dot(e2.astype(BF16), cv)
                o = om * inv if m == 0 else o - om * (lam * inv)
            o_ref[rs, hs] = (o * _rms(o, dh2) * gsub).astype(BF16)


def _diff_attn_ctx(q, k, v, lam_vecs, gsub, e, lam_init):
    spec = pl.BlockSpec((SEQ, W_A), lambda b: (b, 0))
    return pl.pallas_call(
        functools.partial(_diff_attn_kernel, lam_init=lam_init, heads=H_A, has_ctx=False, sub=SEQ),
        out_shape=jax.ShapeDtypeStruct((T_CTX, W_A), BF16),
        grid=(BATCH,),
        in_specs=[
            pl.BlockSpec((None, 4, DH_A), lambda b: (e, 0, 0)),
            pl.BlockSpec((None, 1, 2 * DH_A), lambda b: (e, 0, 0)),
            spec, spec, spec,
        ],
        out_specs=spec,
        compiler_params=_cparams(("parallel",)),
        name="diff_attn_ctx",
    )(lam_vecs, gsub, q, k, v)


def _diff_attn_smp(q, k, v, cache_k, cache_v, lam_vecs, gsub, e, lam_init):
    tq, sub = DEC_SEQ, 256
    w = 2 * DH_A
    nq = DEC_SEQ // tq
    cache_spec = pl.BlockSpec((None, None, 1, PAST_LEN, w), lambda b, h, i: (b, e, h, 0, 0))
    return pl.pallas_call(
        functools.partial(_diff_attn_kernel, lam_init=lam_init, heads=1, has_ctx=True, sub=sub),
        out_shape=jax.ShapeDtypeStruct((T_SMP, W_A), BF16),
        grid=(DEC_BATCH, H_A, nq),
        in_specs=[
            pl.BlockSpec((None, 4, DH_A), lambda b, h, i: (e, 0, 0)),
            pl.BlockSpec((None, 1, w), lambda b, h, i: (e, 0, 0)),
            pl.BlockSpec((tq, w), lambda b, h, i: (b * nq + i, h)),
            pl.BlockSpec((DEC_SEQ, w), lambda b, h, i: (b, h)),
            pl.BlockSpec((DEC_SEQ, w), lambda b, h, i: (b, h)),
            cache_spec, cache_spec,
        ],
        out_specs=pl.BlockSpec((tq, w), lambda b, h, i: (b * nq + i, h)),
        compiler_params=_cparams(("parallel", "parallel", "arbitrary")),
        name="diff_attn_smp",
    )(lam_vecs, gsub, q, k, v, cache_k, cache_v)


DFT_LO = 64


def _fourier_kernel(x_ref, bc_ref, bs_ref, c0_ref, s0_ref, c1_ref, s1_ref, o_ref, y1_ref, y2_ref, *, scale, tr):
    r = pl.program_id(1)

    @pl.when(r == 0)
    def _():
        x = x_ref[...]
        y1_ref[...] = _dot(x, bc_ref[...]).astype(BF16)
        y2_ref[...] = _dot(x, bs_ref[...]).astype(BF16)

    c0, s0 = c0_ref[...], s0_ref[...]
    cs, ss = [], []
    for jj in range(tr // DFT_LO):
        j1 = r * (tr // DFT_LO) + jj
        c1, s1 = c1_ref[pl.ds(j1, 1), :], s1_ref[pl.ds(j1, 1), :]
        cs.append((c1 * c0 - s1 * s0).astype(BF16))
        ss.append((s1 * c0 + c1 * s0).astype(BF16))
    out = _dot(jnp.concatenate(cs, axis=0), y1_ref[...]) - _dot(jnp.concatenate(ss, axis=0), y2_ref[...])
    o_ref[...] = (out * scale).astype(BF16)


def _fourier(f, s, dft_ch, dft_seq):
    tr = 256
    nb = f.shape[0] // s
    bc, bs = dft_ch
    full = lambda a: pl.BlockSpec(a.shape, lambda b, r: (0, 0))
    return pl.pallas_call(
        functools.partial(_fourier_kernel, scale=1.0 / math.sqrt(s * DG_B), tr=tr),
        out_shape=jax.ShapeDtypeStruct(f.shape, BF16),
        grid=(nb, s // tr),
        in_specs=[pl.BlockSpec((s, W_B), lambda b, r: (b, 0)), full(bc), full(bs)] + [full(t) for t in dft_seq],
        out_specs=pl.BlockSpec((tr, W_B), lambda b, r: (b * (s // tr) + r, 0)),
        scratch_shapes=[pltpu.VMEM((s, W_B), BF16), pltpu.VMEM((s, W_B), BF16)],
        compiler_params=_cparams(("parallel", "arbitrary")),
        name="fourier",
    )(f, bc, bs, *dft_seq)


def _dft_seq_tables(n):
    a0, a1 = _dft_angles(DFT_LO, 1, n), _dft_angles(n // DFT_LO, DFT_LO, n)
    return jnp.cos(a0), jnp.sin(a0), jnp.cos(a1), jnp.sin(a1)


def _dft_angles(rows, stride, n):
    j = jnp.arange(rows, dtype=jnp.int32)[:, None] * stride
    k = jnp.arange(n, dtype=jnp.int32)[None, :]
    return ((j * k) % n).astype(F32) * (2.0 * math.pi / n)


def _dft_mats(n):
    lo = min(n, 64)
    a0 = _dft_angles(lo, 1, n)
    c0, s0 = jnp.cos(a0), jnp.sin(a0)
    if lo == n:
        return c0, s0
    a1 = _dft_angles(n // lo, lo, n)
    c1, s1 = jnp.cos(a1)[:, None, :], jnp.sin(a1)[:, None, :]
    c = c1 * c0[None] - s1 * s0[None]
    s = s1 * c0[None] + c1 * s0[None]
    return c.reshape(n, n), s.reshape(n, n)


def _dft_channel_mats():
    c, s = _dft_mats(DG_B)
    eye = jnp.eye(G_B, dtype=F32)
    return jnp.kron(eye, c).astype(BF16), jnp.kron(eye, s).astype(BF16)


def _outproj_kernel(*refs, n_act, n_x):
    is_ctx = pl.program_id(0) < T_CTX // TM
    x = refs[0][...] if n_x == 1 else jnp.where(is_ctx, refs[0][...], refs[1][...])
    mod_ref, g_ref = refs[n_x:n_x + 2]
    act = refs[n_x + 2:n_x + 2 + 3 * n_act]
    o_ref, hn_ref = refs[n_x + 2 + 3 * n_act:]
    y = None
    for a in range(n_act):
        ctx_ref, smp_ref, w_ref = act[3 * a:3 * a + 3]
        lhs = jnp.where(is_ctx, ctx_ref[...], smp_ref[...])
        t = _dot(lhs, w_ref[...])
        y = t if y is None else y + t
    x_new = x + mod_ref[5:6, :] * y
    o_ref[...] = x_new
    hn_ref[...] = _norm_mod(x_new, g_ref[...], mod_ref, 6).astype(BF16)


def _outproj(xs, mod_l, g_norm_l, acts):
    n_ctx = T_CTX // TM
    row_spec = pl.BlockSpec((TM, D_MODEL), lambda i: (i, 0))
    ctx_rows = lambda n: pl.BlockSpec((TM, n), lambda i: (jnp.minimum(i, n_ctx - 1), 0))
    smp_rows = lambda n: pl.BlockSpec((TM, n), lambda i: (jnp.maximum(i - n_ctx, 0), 0))
    in_specs = ([row_spec] if len(xs) == 1 else [ctx_rows(D_MODEL), smp_rows(D_MODEL)]) + [
        pl.BlockSpec((None, N_MOD, D_MODEL), lambda i: (_cond_row(i), 0, 0)),
        pl.BlockSpec((None, 1, D_MODEL), lambda i: (2, 0, 0)),
    ]
    args = [*xs, mod_l, g_norm_l]
    for ctx, smp, w, rb in acts:
        kdim = ctx.shape[1]
        in_specs += [ctx_rows(kdim), smp_rows(kdim), pl.BlockSpec((kdim, D_MODEL), lambda i, rb=rb: (rb, 0))]
        args += [ctx, smp, w]
    return pl.pallas_call(
        functools.partial(_outproj_kernel, n_act=len(acts), n_x=len(xs)),
        out_shape=(jax.ShapeDtypeStruct((T_ALL, D_MODEL), F32), jax.ShapeDtypeStruct((T_ALL, D_MODEL), BF16)),
        grid=(T_ALL // TM,),
        in_specs=in_specs,
        out_specs=(row_spec, row_spec),
        compiler_params=_cparams(("parallel",)),
        name="outproj",
    )(*args)


MLA_TM = 256


def _mla_proj_kernel(x_ref, mod_ref, g_ref, win_ref, wuq_ref, wukv_ref, gql_ref, gkvl_ref, gq_ref, gk_ref,
                     cos_ref, sa_ref, sb_ref, q_ref, k_ref, v_ref, ckv_ref, kpe_ref):
    h = _norm_mod(x_ref[...], g_ref[...], mod_ref, 3).astype(BF16)
    res = _dot(h, win_ref[...])
    cq = res[:, :Q_LORA]
    cqn = (cq * _rms(cq, Q_LORA) * gql_ref[...]).astype(BF16)
    ckv = res[:, Q_LORA:Q_LORA + KV_LORA]
    ckvn = ckv * _rms(ckv, KV_LORA) * gkvl_ref[...]
    ckv_ref[...] = ckvn
    kpe = res[:, Q_LORA + KV_LORA:]
    kpe_ref[...] = kpe
    cos, sa, sb = cos_ref[...], sa_ref[...], sb_ref[...]
    gq, gk = gq_ref[...] * (DQK_C ** -0.5 * LOG2E), gk_ref[...]
    ss_pe = jnp.sum(kpe * kpe, axis=-1, keepdims=True)
    kr = _rope(kpe * gk[:, LANE:], cos, sa, sb, ROPE_C // 4)
    ckvb = ckvn.astype(BF16)
    for hh in range(H_C):
        hs = slice(hh * HEAD_PAD, (hh + 1) * HEAD_PAD)
        lo = slice(hh * HEAD_PAD, hh * HEAD_PAD + LANE)
        hi = slice(hh * HEAD_PAD + LANE, (hh + 1) * HEAD_PAD)
        xh = _dot(cqn, wuq_ref[:, hs])
        y = xh * _rms(xh, DQK_C) * gq
        q_ref[:, lo] = y[:, :LANE].astype(BF16)
        q_ref[:, hi] = _rope(y[:, LANE:], cos, sa, sb, ROPE_C // 4).astype(BF16)
        kv = _dot(ckvb, wukv_ref[:, hs])
        k_nope = kv[:, :LANE]
        ss = jnp.sum(k_nope * k_nope, axis=-1, keepdims=True) + ss_pe
        r = lax.rsqrt(ss * (1.0 / DQK_C) + EPS)
        k_ref[:, lo] = (k_nope * r * gk[:, :LANE]).astype(BF16)
        k_ref[:, hi] = (kr * r).astype(BF16)
        v_ref[:, hh * DV_C:(hh + 1) * DV_C] = kv[:, LANE:].astype(BF16)


def _mla_proj(x, mod_l, g_norm_l, w_in_pad, w_uq_pad, w_ukv, gq_lora, gkv_lora, gq_pad, gk_pad, o, tabs):
    tm = MLA_TM
    n_in = Q_LORA + KV_LORA + LANE
    resident = dict(pipeline_mode=pl.Buffered(1))
    tab_spec = pl.BlockSpec((tm, LANE), lambda i: (i, 0))
    row = lambda n: pl.BlockSpec((tm, n), lambda i: (i, 0))
    return pl.pallas_call(
        _mla_proj_kernel,
        out_shape=(jax.ShapeDtypeStruct((T_ALL, H_C * HEAD_PAD), BF16),
                   jax.ShapeDtypeStruct((T_ALL, H_C * HEAD_PAD), BF16),
                   jax.ShapeDtypeStruct((T_ALL, H_C * DV_C), BF16),
                   jax.ShapeDtypeStruct((T_ALL, KV_LORA), F32),
                   jax.ShapeDtypeStruct((T_ALL, LANE), F32)),
        grid=(T_ALL // tm,),
        in_specs=[
            row(D_MODEL),
            pl.BlockSpec((None, N_MOD, D_MODEL), lambda i: (_cond_row(i, tm), 0, 0)),
            pl.BlockSpec((None, 1, D_MODEL), lambda i: (1, 0, 0)),
            pl.BlockSpec((D_MODEL, n_in), lambda i: (0, 0), **resident),
            pl.BlockSpec((Q_LORA, H_C * HEAD_PAD), lambda i: (0, 0), **resident),
            pl.BlockSpec((KV_LORA, H_C * HEAD_PAD), lambda i: (0, 0), **resident),
            pl.BlockSpec((None, 1, Q_LORA), lambda i: (o, 0, 0)),
            pl.BlockSpec((None, 1, KV_LORA), lambda i: (o, 0, 0)),
            pl.BlockSpec((None, 1, HEAD_PAD), lambda i: (o, 0, 0)),
            pl.BlockSpec((None, 1, HEAD_PAD), lambda i: (o, 0, 0)),
            tab_spec, tab_spec, tab_spec,
        ],
        out_specs=(row(H_C * HEAD_PAD), row(H_C * HEAD_PAD), row(H_C * DV_C), row(KV_LORA), row(LANE)),
        compiler_params=_cparams(("parallel",)),
        name="mla_proj",
    )(x, mod_l, g_norm_l, w_in_pad, w_uq_pad, w_ukv, gq_lora, gkv_lora, gq_pad, gk_pad, *tabs)


MLA_HB = 4


def _mla_kv_kernel(ckv_ref, kpe_ref, w_ref, g_ref, *rest, use_rope):
    if use_rope:
        cos_ref, sa_ref, sb_ref, k_ref, v_ref = rest
    else:
        k_ref, v_ref = rest
    ckv = ckv_ref[...].astype(BF16)
    g = g_ref[...]
    kpe = kpe_ref[...]
    ss_pe = jnp.sum(kpe * kpe, axis=-1, keepdims=True)
    kr = kpe * g[:, LANE:]
    if use_rope:
        kr = _rope(kr, cos_ref[...], sa_ref[...], sb_ref[...], ROPE_C // 4)
    for hh in range(MLA_HB):
        res = _dot(ckv, w_ref[:, hh * 2 * LANE:(hh + 1) * 2 * LANE])
        k_nope = res[:, :LANE]
        ss = jnp.sum(k_nope * k_nope, axis=-1, keepdims=True) + ss_pe
        r = lax.rsqrt(ss * (1.0 / DQK_C) + EPS)
        k_ref[:, hh * HEAD_PAD:hh * HEAD_PAD + LANE] = (k_nope * r * g[:, :LANE]).astype(BF16)
        k_ref[:, hh * HEAD_PAD + LANE:(hh + 1) * HEAD_PAD] = (kr * r).astype(BF16)
        v_ref[:, hh * DV_C:(hh + 1) * DV_C] = res[:, LANE:].astype(BF16)


def _mla_kv(ckv, kpe_pad, tile0, rows, w_ukv, gk_pad, o, tabs):
    use_rope = tabs is not None
    in_specs = [
        pl.BlockSpec((TM, KV_LORA), lambda i, j: (tile0 + i, 0)),
        pl.BlockSpec((TM, LANE), lambda i, j: (tile0 + i, 0)),
        pl.BlockSpec((KV_LORA, MLA_HB * 2 * LANE), lambda i, j: (0, j)),
        pl.BlockSpec((None, 1, HEAD_PAD), lambda i, j: (o, 0, 0)),
    ]
    args = [ckv, kpe_pad, w_ukv, gk_pad]
    if use_rope:
        in_specs += [pl.BlockSpec((TM, LANE), lambda i, j: (i % (DEC_SEQ // TM), 0))] * 3
        args += list(tabs)
    return pl.pallas_call(
        functools.partial(_mla_kv_kernel, use_rope=use_rope),
        out_shape=(jax.ShapeDtypeStruct((rows, H_C * HEAD_PAD), BF16),
                   jax.ShapeDtypeStruct((rows, H_C * DV_C), BF16)),
        grid=(rows // TM, H_C // MLA_HB),
        in_specs=in_specs,
        out_specs=(pl.BlockSpec((TM, MLA_HB * HEAD_PAD), lambda i, j: (i, j)),
                   pl.BlockSpec((TM, MLA_HB * DV_C), lambda i, j: (i, j))),
        compiler_params=_cparams(("parallel", "parallel")),
        name="mla_kv",
    )(*args)


def _mla_attn_kernel(q_ref, k_ref, v_ref, *rest, heads, has_ctx, sub):
    if has_ctx:
        ck_ref, cv_ref, o_ref, v1_ref, cv1_ref = rest
    else:
        o_ref, v1_ref = rest
    def with_ones(dst_ref, hh, v):
        dst_ref[hh, :, :DV_C] = v
        dst_ref[hh, :, DV_C:] = jnp.ones_like(v)
        return dst_ref[hh]

    for hh in range(heads):
        ks = slice(hh * HEAD_PAD, (hh + 1) * HEAD_PAD)
        vs = slice(hh * DV_C, (hh + 1) * DV_C)
        v1 = with_ones(v1_ref, hh, v_ref[:, vs])
        if has_ctx:
            cv1 = with_ones(cv1_ref, hh, cv_ref[:, vs])
        for r in range(q_ref.shape[0] // sub):
            rs = slice(r * sub, (r + 1) * sub)
            q = q_ref[rs, ks]
            s = _dot_nt(q, k_ref[:, ks])
            s2 = _dot_nt(q, ck_ref[:, ks]) if has_ctx else None
            e, e2, _ = _softmax_exp(s, s2, False)
            o = _dot(e.astype(BF16), v1)
            if has_ctx:
                o = o + _dot(e2.astype(BF16), cv1)
            o_ref[rs, vs] = (o[:, :DV_C] * (1.0 / o[:, DV_C:])).astype(BF16)


def _mla_attn_ctx(q, k, v):
    heads = H_C
    return pl.pallas_call(
        functools.partial(_mla_attn_kernel, heads=heads, has_ctx=False, sub=SEQ),
        out_shape=jax.ShapeDtypeStruct((T_CTX, H_C * DV_C), BF16),
        grid=(BATCH, H_C // heads),
        in_specs=[
            pl.BlockSpec((SEQ, heads * HEAD_PAD), lambda b, j: (b, j)),
            pl.BlockSpec((SEQ, heads * HEAD_PAD), lambda b, j: (b, j)),
            pl.BlockSpec((SEQ, heads * DV_C), lambda b, j: (b, j)),
        ],
        out_specs=pl.BlockSpec((SEQ, heads * DV_C), lambda b, j: (b, j)),
        scratch_shapes=[pltpu.VMEM((heads, SEQ, 2 * DV_C), BF16)],
        compiler_params=_cparams(("parallel", "parallel")),
        name="mla_attn_ctx",
    )(q, k, v)


def _mla_attn_smp(q, k, v, ck, cv):
    tq, sub = DEC_SEQ, 256
    nq = DEC_SEQ // tq
    q0, s0 = T_CTX // tq, T_CTX // DEC_SEQ
    return pl.pallas_call(
        functools.partial(_mla_attn_kernel, heads=1, has_ctx=True, sub=sub),
        out_shape=jax.ShapeDtypeStruct((T_SMP, H_C * DV_C), BF16),
        grid=(DEC_BATCH, H_C, nq),
        in_specs=[
            pl.BlockSpec((tq, HEAD_PAD), lambda b, h, i: (q0 + b * nq + i, h)),
            pl.BlockSpec((DEC_SEQ, HEAD_PAD), lambda b, h, i: (s0 + b, h)),
            pl.BlockSpec((DEC_SEQ, DV_C), lambda b, h, i: (s0 + b, h)),
            pl.BlockSpec((PAST_LEN, HEAD_PAD), lambda b, h, i: (b, h)),
            pl.BlockSpec((PAST_LEN, DV_C), lambda b, h, i: (b, h)),
        ],
        out_specs=pl.BlockSpec((tq, DV_C), lambda b, h, i: (b * nq + i, h)),
        scratch_shapes=[pltpu.VMEM((1, DEC_SEQ, 2 * DV_C), BF16), pltpu.VMEM((1, PAST_LEN, 2 * DV_C), BF16)],
        compiler_params=_cparams(("parallel", "parallel", "arbitrary")),
        name="mla_attn_smp",
    )(q, k, v, ck, cv)


def _axial_angles(rows, rot_dim):
    row = jnp.repeat(jnp.arange(rows, dtype=F32), GRID_W)
    col = jnp.tile(jnp.arange(GRID_W, dtype=F32), rows)
    half = rot_dim // 2
    inv_freq = ROPE_BASE ** (-jnp.arange(0, half, 2, dtype=F32) / half)
    ang_r = row[:, None] * inv_freq[None, :]
    ang_c = col[:, None] * inv_freq[None, :]
    return jnp.concatenate([ang_r, ang_r, ang_c, ang_c], axis=-1)


def _rope_tables(rows, rot_dim):
    ang = _axial_angles(rows, rot_dim)
    q = rot_dim // 4
    blk = (jnp.arange(rot_dim) // q) % 2
    cos, sin = jnp.cos(ang), jnp.sin(ang)
    sin_a = jnp.where(blk == 0, -sin, 0.0)
    sin_b = jnp.where(blk == 1, sin, 0.0)

    def full(t, fill):
        return jnp.pad(t, ((0, 0), (0, LANE - rot_dim)), constant_values=fill)

    return full(cos, 1.0), full(sin_a, 0.0), full(sin_b, 0.0)


def _joint_tables(tabs):
    def joint(t, fill):
        return jnp.concatenate([jnp.full((T_CTX, LANE), fill, F32), jnp.tile(t, (DEC_BATCH, 1))], axis=0)
    return joint(tabs[0], 1.0), joint(tabs[1], 0.0), joint(tabs[2], 0.0)


def kernel(x_prompt, x_sample, c, cache_diff_k, cache_diff_v, cache_mla_ckv, cache_mla_kpe, c_ctx, w_mod, b_mod, g_norm, w_ffn_gate, w_ffn_up, w_ffn_down, w_in_ab, w_out_ab, g_qk_diff, diff_lambda, g_diff_sub, w_in_mla, g_q_lora, w_uq, g_kv_lora, w_ukv, g_qk_mla, w_o_mla):
    rows = DEC_SEQ // GRID_W
    tabs_a = _rope_tables(rows, DH_A)
    tabs_c_joint = _joint_tables(_rope_tables(rows, ROPE_C))
    dft_ch = _dft_channel_mats()
    dft_ctx, dft_smp = _dft_seq_tables(SEQ), _dft_seq_tables(DEC_SEQ)
    p_ctx, p_smp = _Pass(False), _Pass(True)

    w_ffn = (w_ffn_gate[0, 0].astype(BF16), w_ffn_up[0, 0].astype(BF16), w_ffn_down[0, 0].astype(BF16))

    def ffn_casts(l, k):
        return [(w_ffn_gate, (l, k), "plain"), (w_ffn_up, (l, k), "plain"), (w_ffn_down, (l, k), "plain")]

    def mixer_casts(l):
        j = l // 2
        if l % 2 == 0:
            return [(w_in_ab, (j,), "plain"), (w_out_ab, (j,), "plain")]
        return [(w_in_mla, (j,), "tail"), (w_uq, (j,), "heads"), (w_ukv, (j,), "plain"), (w_o_mla, (j,), "plain")]

    g_qk_mla_pad = jnp.pad(g_qk_mla, ((0, 0), (0, 0), (0, HEAD_PAD - DQK_C)))[:, :, None, :]
    gq_mla, gk_mla = g_qk_mla_pad[:, 0], g_qk_mla_pad[:, 1]
    g_norm4 = g_norm[:, :, None, :]
    gqk_diff = g_qk_diff[:, :, None, :]
    g_sub3 = g_diff_sub[:, None, :]
    g_q_lora3, g_kv_lora3 = g_q_lora[:, None, :], g_kv_lora[:, None, :]

    cond8 = jnp.concatenate([c_ctx[None, :], c, jnp.zeros((8 - N_COND, D_MODEL), F32)], axis=0)
    mod = _modulation(cond8, w_mod, b_mod)[:, :N_COND].reshape(DEPTH, N_COND, N_MOD, D_MODEL)

    x_ctx, x_smp = x_prompt.reshape(T_CTX, D_MODEL), x_sample.reshape(T_SMP, D_MODEL)
    n_ctx, n_smp, n_all = T_CTX // FFN_TM, T_SMP // FFN_TM, T_ALL // FFN_TM
    new_kv = None
    mla_ckv, mla_kpe = [], []

    for l in range(DEPTH):
        mod_l = mod[l]
        if l == 0:
            h = _prenorm((x_ctx, x_smp), mod_l, g_norm4[l])
            xc, *cast = _ffn(h, x_ctx, 0, 0, n_ctx, mod_l, w_ffn, 0, ffn_casts(l, 1) + mixer_casts(l)[:1])
            xs, w_out0 = _ffn(h, x_smp, 0, n_ctx, n_smp, mod_l, w_ffn, 0, mixer_casts(l)[1:])
            w_ffn, w_mix, xs_l = cast[:3], [cast[3], w_out0], (xc, xs)
        else:
            h = _prenorm((x,), mod_l, g_norm4[l])
            x, *w_ffn = _ffn(h, x, 0, 0, n_all, mod_l, w_ffn, 0, ffn_casts(l, 1))
            xs_l = (x, x)
        if l % 2 == 0:
            e = l // 2
            lam_init = 0.8 - 0.6 * math.exp(-0.3 * l)
            w_in_b, w_out_b = w_mix
            q_c, k_c, v_c, f_c, *new_kv = _inproj_even(p_ctx, xs_l[0], mod_l, g_norm4[l], w_in_b, gqk_diff, e,
                                                      None, new_kv)
            q_s, k_s, v_s, f_s = _inproj_even(p_smp, xs_l[1], mod_l, g_norm4[l], w_in_b, gqk_diff, e, tabs_a, None)
            o_ctx = _diff_attn_ctx(q_c, k_c, v_c, diff_lambda, g_sub3, e, lam_init)
            o_smp = _diff_attn_smp(q_s, k_s, v_s, cache_diff_k, cache_diff_v, diff_lambda, g_sub3, e, lam_init)
            fo_ctx = _fourier(f_c, SEQ, dft_ch, dft_ctx)
            fo_smp = _fourier(f_s, DEC_SEQ, dft_ch, dft_smp)
            x, h = _outproj(xs_l if l == 0 else (x,), mod_l, g_norm4[l],
                            [(o_ctx, o_smp, w_out_b, 0), (fo_ctx, fo_smp, w_out_b, W_A // W_B)])
        else:
            o = l // 2
            w_in_mla_b, w_uq_b, w_ukv_b, w_o_b = w_mix
            q, k, v, ckvn, kpe_pad = _mla_proj(x, mod_l, g_norm4[l], w_in_mla_b, w_uq_b, w_ukv_b, g_q_lora3,
                                               g_kv_lora3, gq_mla, gk_mla, o, tabs_c_joint)
            cache_ckv = cache_mla_ckv[:, o].reshape(DEC_BATCH * PAST_LEN, KV_LORA)
            cache_kpe = jnp.pad(cache_mla_kpe[:, o].reshape(DEC_BATCH * PAST_LEN, ROPE_C),
                                ((0, 0), (0, LANE - ROPE_C)))
            ck, cv = _mla_kv(cache_ckv, cache_kpe, 0, DEC_BATCH * PAST_LEN, w_ukv_b, gk_mla, o, None)
            a_ctx = _mla_attn_ctx(q, k, v)
            a_smp = _mla_attn_smp(q, k, v, ck, cv)
            x, h = _outproj((x,), mod_l, g_norm4[l], [(a_ctx, a_smp, w_o_b, 0)])
            mla_ckv.append(ckvn[:T_CTX].reshape(BATCH, SEQ, KV_LORA))
            mla_kpe.append(kpe_pad[:T_CTX, :ROPE_C].reshape(BATCH, SEQ, ROPE_C))
        if l + 1 < DEPTH:
            x, *cast = _ffn(h, x, 0, 0, n_all, mod_l, w_ffn, 1, ffn_casts(l + 1, 0) + mixer_casts(l + 1))
            w_ffn, w_mix = cast[:3], cast[3:]
        else:
            (y_ctx,) = _ffn(h, x, 0, 0, n_ctx, mod_l, w_ffn, 1, [])
            (y_smp,) = _ffn(h, x, n_ctx, n_ctx, n_smp, mod_l, w_ffn, 1, [])

    y_prompt = y_ctx.reshape(BATCH, SEQ, D_MODEL)
    y_sample = y_smp.reshape(DEC_BATCH, DEC_SEQ, D_MODEL)
    return (y_prompt, y_sample, new_kv[0], new_kv[1], jnp.stack(mla_ckv, axis=1), jnp.stack(mla_kpe, axis=1))
```
